```python
import math
import jax, jax.numpy as jnp
from jax import lax
import numpy as np

D_MODEL = 1024
BATCH = 16
SEQ = 2048
DEPTH = 1

ATTN_GROUPS = ((128, 1), (512, 4), (2048, 16))
HEADS_PER_GROUP = 4
HEAD_DIM = 128
N_ATTN_HEADS = HEADS_PER_GROUP * len(ATTN_GROUPS)
ATTN_OUT = HEADS_PER_GROUP * HEAD_DIM
Q_DIM = N_ATTN_HEADS * HEAD_DIM
ROPE_DIM = HEAD_DIM // 4
ROPE_THETA = 500000.0

SSM_INNER = D_MODEL
SSM_HEADDIM = 64
SSM_HEADS = SSM_INNER // SSM_HEADDIM
SSM_GROUPS = 4
SSM_STATE = 128
SSM_CONV = 4
SSM_CHUNK = 128
SSM_CONV_DIM = SSM_INNER + 2 * SSM_GROUPS * SSM_STATE

IN_DIM = 3 * Q_DIM + SSM_INNER + SSM_CONV_DIM + SSM_HEADS + 2 * D_MODEL

N_EXPERTS = 256
TOP_K = 8
N_EXPERT_GROUPS = 8
TOPK_GROUPS = 4
EXPERT_FF = 256
SHARED_FF = 256
ROUTED_SCALE = 2.5
MOE_BLOCK = 128

NORM_EPS = 1e-6
N_MOD = 6

kernel_name = "hybrid_dilated_attn_ssd_moe_block"


def rms_norm(x, w):
    xf = x.astype(jnp.float32)
    y = xf * lax.rsqrt(jnp.mean(xf * xf, axis=-1, keepdims=True) + NORM_EPS)
    return (y * w.astype(jnp.float32)).astype(x.dtype)


def partial_rope(x, positions):
    half = ROPE_DIM // 2
    inv_freq = ROPE_THETA ** (-jnp.arange(half, dtype=jnp.float32) / half)
    ang = positions.astype(jnp.float32)[..., None] * inv_freq
    cos = jnp.cos(ang)[:, :, None, :]
    sin = jnp.sin(ang)[:, :, None, :]
    xr = x[..., :ROPE_DIM].astype(jnp.float32)
    x1, x2 = xr[..., :half], xr[..., half:]
    rot = jnp.concatenate([x1 * cos - x2 * sin, x2 * cos + x1 * sin], axis=-1)
    return jnp.concatenate([rot.astype(x.dtype), x[..., ROPE_DIM:]], axis=-1)


def dilated_window_attention(q, k, v, window, dilation):
    b, s, h, e = q.shape
    blk = window // dilation
    n_sub = s // dilation
    nb = -(-n_sub // blk)
    n_pad = nb * blk

    def strided(t):
        t = t.reshape(b, n_sub, dilation, h, e).transpose(0, 2, 1, 3, 4)
        return jnp.pad(t, ((0, 0), (0, 0), (0, n_pad - n_sub), (0, 0), (0, 0)))

    def banded(t):
        t = jnp.pad(t, ((0, 0), (0, 0), (blk, 0), (0, 0), (0, 0))).reshape(b, dilation, nb + 1, blk, h, e)
        return jnp.concatenate([t[:, :, :-1], t[:, :, 1:]], axis=3)

    qb = strided(q).reshape(b, dilation, nb, blk, h, e)
    kb = banded(strided(k))
    vb = banded(strided(v))
    scores = jnp.einsum("brnqhe,brnkhe->brnhqk", qb, kb,
                        preferred_element_type=jnp.float32) * (1.0 / math.sqrt(e))
    qi = jnp.arange(blk)[:, None]
    kj = jnp.arange(2 * blk)[None, :]
    k_abs = jnp.arange(nb)[:, None, None] * blk - blk + kj[None]
    valid = (kj >= qi) & (kj <= qi + blk) & (k_abs >= 0)
    scores = jnp.where(valid[None, None, :, None], scores, -jnp.inf)
    m = jnp.max(scores, axis=-1)
    p = jnp.exp(scores - m[..., None])
    l = jnp.sum(p, axis=-1)
    o = jnp.einsum("brnhqk,brnkhe->brnqhe", p, vb.astype(jnp.float32))
    o = o / l.transpose(0, 1, 2, 4, 3)[..., None]
    o = o.reshape(b, dilation, n_pad, h, e)[:, :, :n_sub].transpose(0, 2, 1, 3, 4).reshape(b, s, h, e)

    def unblock(t):
        t = t.transpose(0, 1, 2, 4, 3).reshape(b, dilation, n_pad, h)[:, :, :n_sub]
        return t.transpose(0, 2, 1, 3).reshape(b, s, h)

    return o, unblock(m), unblock(l)


def causal_depthwise_conv(x, w, bias):
    kw, ch = w.shape
    y = lax.conv_general_dilated(x, w[:, None, :], window_strides=(1,), padding=[(kw - 1, 0)],
                                 dimension_numbers=("NWC", "WIO", "NWC"), feature_group_count=ch)
    return y + bias


def segsum(a):
    cs = jnp.cumsum(a, axis=-1)
    n = a.shape[-1]
    mask = jnp.tril(jnp.ones((n, n), dtype=bool))
    return jnp.where(mask, cs[..., :, None] - cs[..., None, :], -jnp.inf)


def ssd_chunked(xh, dt, A, Bm, Cm):
    b, s, nh, hp = xh.shape
    g, n = Bm.shape[2], Bm.shape[3]
    hpg = nh // g
    l = SSM_CHUNK
    nc = s // l
    x = (xh * dt[..., None]).reshape(b, nc, l, g, hpg, hp)
    a = (dt * A).reshape(b, nc, l, g, hpg).transpose(0, 1, 3, 4, 2)
    Bc = Bm.reshape(b, nc, l, g, n)
    Cc = Cm.reshape(b, nc, l, g, n)
    a_cs = jnp.cumsum(a, axis=-1)
    decay_in = jnp.exp(segsum(a))
    cb = jnp.einsum("bclgn,bcsgn->bcgls", Cc, Bc)
    y_diag = jnp.einsum("bcghls,bcsghp->bclghp", cb[:, :, :, None] * decay_in, x)
    decay_states = jnp.exp(a_cs[..., -1:] - a_cs)
    states = jnp.einsum("bclgn,bcghl,bclghp->bcghpn", Bc, decay_states, x)
    chunk_decay = jnp.exp(a_cs[..., -1])

    def step(h_prev, inp):
        st, dec = inp
        return h_prev * dec[..., None, None] + st, h_prev

    h0 = jnp.zeros((b, g, hpg, hp, n), dtype=jnp.float32)
    _, prev = lax.scan(step, h0, (jnp.moveaxis(states, 1, 0), jnp.moveaxis(chunk_decay, 1, 0)))
    prev = jnp.moveaxis(prev, 0, 1)
    y_off = jnp.einsum("bclgn,bcghpn,bcghl->bclghp", Cc, prev, jnp.exp(a_cs))
    return (y_diag + y_off).reshape(b, s, nh, hp)


def ssd_mixer(z, xbc, dt_raw, conv_w, conv_b, dt_bias, a_log, d_skip, ssm_norm_w):
    b, s, _ = z.shape
    xbc = jax.nn.silu(causal_depthwise_conv(xbc, conv_w, conv_b)).astype(jnp.float32)
    xs = xbc[..., :SSM_INNER]
    Bm = xbc[..., SSM_INNER:SSM_INNER + SSM_GROUPS * SSM_STATE].reshape(b, s, SSM_GROUPS, SSM_STATE)
    Cm = xbc[..., SSM_INNER + SSM_GROUPS * SSM_STATE:].reshape(b, s, SSM_GROUPS, SSM_STATE)
    xh = xs.reshape(b, s, SSM_HEADS, SSM_HEADDIM)
    dt = jax.nn.softplus(dt_raw.astype(jnp.float32) + dt_bias.astype(jnp.float32))
    A = -jnp.exp(a_log.astype(jnp.float32))
    y = ssd_chunked(xh, dt, A, Bm, Cm) + d_skip.astype(jnp.float32)[:, None] * xh
    y = y.reshape(b, s, SSM_INNER) * jax.nn.silu(z.astype(jnp.float32))
    yg = y.reshape(b, s, SSM_GROUPS, -1)
    yg = yg * lax.rsqrt(jnp.mean(yg * yg, axis=-1, keepdims=True) + NORM_EPS)
    return (yg.reshape(b, s, SSM_INNER) * ssm_norm_w.astype(jnp.float32)).astype(z.dtype)


def hybrid_mixer(u, positions, w_in, conv_w, conv_b, dt_bias, a_log, d_skip, ssm_norm_w,
                 w_branch_attn, w_branch_ssm, w_out):
    b, s, _ = u.shape
    proj = jnp.einsum("bsd,df->bsf", u, w_in)
    cuts = np.cumsum([Q_DIM, Q_DIM, Q_DIM, SSM_INNER, SSM_CONV_DIM, SSM_HEADS]).tolist()
    q, k, v, z, xbc, dt_raw, gate_logits = jnp.split(proj, cuts, axis=-1)
    q = partial_rope(q.reshape(b, s, N_ATTN_HEADS, HEAD_DIM), positions)
    k = partial_rope(k.reshape(b, s, N_ATTN_HEADS, HEAD_DIM), positions)
    v = v.reshape(b, s, N_ATTN_HEADS, HEAD_DIM)
    outs, maxes, dens = [], [], []
    for gi, (win, dil) in enumerate(ATTN_GROUPS):
        sl = slice(gi * HEADS_PER_GROUP, (gi + 1) * HEADS_PER_GROUP)
        o, m, l = dilated_window_attention(q[:, :, sl], k[:, :, sl], v[:, :, sl], win, dil)
        outs.append(o); maxes.append(m); dens.append(l)
    m_all = jnp.stack(maxes)
    wts = jnp.stack(dens) * jnp.exp(m_all - jnp.max(m_all, axis=0))
    attn = jnp.einsum("gbsh,gbshe->bshe", wts, jnp.stack(outs)) / jnp.sum(wts, axis=0)[..., None]
    attn = attn.reshape(b, s, ATTN_OUT).astype(u.dtype)
    ssm = ssd_mixer(z, xbc, dt_raw, conv_w, conv_b, dt_bias, a_log, d_skip, ssm_norm_w)
    gate_a, gate_s = jnp.split(gate_logits, 2, axis=-1)
    merged = (jax.nn.sigmoid(gate_a) * jnp.einsum("bsf,fd->bsd", attn, w_branch_attn)
              + jax.nn.sigmoid(gate_s) * jnp.einsum("bsf,fd->bsd", ssm, w_branch_ssm))
    return jnp.einsum("bsd,de->bse", merged, w_out)


def moe_ffn(u, w_router, router_bias, w_gate_e, w_up_e, w_down_e, w_gate_s, w_up_s, w_down_s):
    b, s, d = u.shape
    t = u.reshape(b * s, d)
    n_tok = b * s
    scores = jax.nn.sigmoid(jnp.dot(t, w_router, preferred_element_type=jnp.float32))
    biased = scores + router_bias.astype(jnp.float32)
    grp = biased.reshape(n_tok, N_EXPERT_GROUPS, N_EXPERTS // N_EXPERT_GROUPS)
    grp_score = jnp.sum(lax.top_k(grp, 2)[0], axis=-1)
    _, grp_idx = lax.top_k(grp_score, TOPK_GROUPS)
    grp_mask = jnp.sum(jax.nn.one_hot(grp_idx, N_EXPERT_GROUPS, dtype=jnp.float32), axis=1) > 0
    masked = jnp.where(jnp.repeat(grp_mask, N_EXPERTS // N_EXPERT_GROUPS, axis=1), biased, -jnp.inf)
    _, idx = lax.top_k(masked, TOP_K)
    wsel = jnp.take_along_axis(scores, idx, axis=1)
    wsel = wsel / jnp.sum(wsel, axis=-1, keepdims=True) * ROUTED_SCALE

    n_assign = n_tok * TOP_K
    flat_e = idx.reshape(-1)
    flat_tok = jnp.repeat(jnp.arange(n_tok, dtype=jnp.int32), TOP_K)
    flat_w = wsel.reshape(-1)
    order = jnp.argsort(flat_e)
    se = flat_e[order]
    counts = jnp.bincount(flat_e, length=N_EXPERTS)
    padded = (counts + MOE_BLOCK - 1) // MOE_BLOCK * MOE_BLOCK
    pad_end = jnp.cumsum(padded)
    pad_start = pad_end - padded
    start = jnp.cumsum(counts) - counts
    dest = pad_start[se] + (jnp.arange(n_assign) - start[se])
    n_rows = n_assign + N_EXPERTS * MOE_BLOCK
    n_blk = n_rows // MOE_BLOCK
    row_tok = jnp.zeros((n_rows,), jnp.int32).at[dest].set(flat_tok[order])
    row_w = jnp.zeros((n_rows,), jnp.float32).at[dest].set(flat_w[order])
    blk_e = jnp.minimum(jnp.searchsorted(pad_end, jnp.arange(n_blk) * MOE_BLOCK, side="right"),
                        N_EXPERTS - 1)

    def expert_block(acc, inp):
        tok, wt, e = inp
        xb = t[tok]
        hb = jax.nn.silu(xb @ w_gate_e[e]) * (xb @ w_up_e[e])
        yb = (hb @ w_down_e[e]).astype(jnp.float32) * wt[:, None]
        return acc.at[tok].add(yb), None

    routed, _ = lax.scan(expert_block, jnp.zeros((n_tok, d), jnp.float32),
                         (row_tok.reshape(n_blk, MOE_BLOCK), row_w.reshape(n_blk, MOE_BLOCK), blk_e))
    shared = (jax.nn.silu(t @ w_gate_s) * (t @ w_up_s)) @ w_down_s
    return (routed + shared.astype(jnp.float32)).astype(u.dtype).reshape(b, s, d)


def setup_inputs(seed: int = 0) -> dict:
    key = jax.random.key(seed)
    ks = jax.random.split(key, 26)
    f32 = jnp.float32
    L = DEPTH

    def nrm(k, shape, scale):
        return jax.random.normal(k, shape, f32) * scale

    dt0 = jnp.exp(jax.random.uniform(ks[10], (L, SSM_HEADS), f32) * (math.log(0.1) - math.log(0.001))
                  + math.log(0.001))
    return {
        "x": nrm(ks[0], (BATCH, SEQ, D_MODEL), 1.0),
        "c": nrm(ks[1], (BATCH, D_MODEL), 1.0),
        "positions": (jnp.arange(SEQ, dtype=jnp.int32)[None, :]
                      + jax.random.randint(ks[2], (BATCH, 1), 0, 4096, dtype=jnp.int32)),
        "w_mod": nrm(ks[3], (L, D_MODEL, N_MOD * D_MODEL), 0.5 * D_MODEL ** -0.5),
        "b_mod": nrm(ks[4], (L, N_MOD * D_MODEL), 0.02),
        "norm_mix_w": 1.0 + nrm(ks[5], (L, D_MODEL), 0.05),
        "norm_ffn_w": 1.0 + nrm(ks[6], (L, D_MODEL), 0.05),
        "w_in": nrm(ks[7], (L, D_MODEL, IN_DIM), D_MODEL ** -0.5),
        "conv_w": nrm(ks[8], (L, SSM_CONV, SSM_CONV_DIM), SSM_CONV ** -0.5),
        "conv_b": nrm(ks[9], (L, SSM_CONV_DIM), 0.02),
        "dt_bias": dt0 + jnp.log(-jnp.expm1(-dt0)),
        "a_log": jnp.log(jax.random.uniform(ks[11], (L, SSM_HEADS), f32, 1.0, 16.0)),
        "d_skip": 1.0 + nrm(ks[12], (L, SSM_HEADS), 0.1),
        "ssm_norm_w": 1.0 + nrm(ks[13], (L, SSM_INNER), 0.05),
        "w_branch_attn": nrm(ks[14], (L, ATTN_OUT, D_MODEL), ATTN_OUT ** -0.5),
        "w_branch_ssm": nrm(ks[15], (L, SSM_INNER, D_MODEL), SSM_INNER ** -0.5),
        "w_out": nrm(ks[16], (L, D_MODEL, D_MODEL), D_MODEL ** -0.5),
        "w_router": nrm(ks[17], (L, D_MODEL, N_EXPERTS), D_MODEL ** -0.5),
        "router_bias": nrm(ks[18], (L, N_EXPERTS), 0.01),
        "w_gate_e": nrm(ks[19], (L, N_EXPERTS, D_MODEL, EXPERT_FF), D_MODEL ** -0.5),
        "w_up_e": nrm(ks[20], (L, N_EXPERTS, D_MODEL, EXPERT_FF), D_MODEL ** -0.5),
        "w_down_e": nrm(ks[21], (L, N_EXPERTS, EXPERT_FF, D_MODEL), EXPERT_FF ** -0.5),
        "w_gate_s": nrm(ks[22], (L, D_MODEL, SHARED_FF), D_MODEL ** -0.5),
        "w_up_s": nrm(ks[23], (L, D_MODEL, SHARED_FF), D_MODEL ** -0.5),
        "w_down_s": nrm(ks[24], (L, SHARED_FF, D_MODEL), SHARED_FF ** -0.5),
        "norm_final_w": 1.0 + nrm(ks[25], (D_MODEL,), 0.05),
    }


def reference(x, c, positions, w_mod, b_mod, norm_mix_w, norm_ffn_w, w_in, conv_w, conv_b, dt_bias,
              a_log, d_skip, ssm_norm_w, w_branch_attn, w_branch_ssm, w_out, w_router, router_bias,
              w_gate_e, w_up_e, w_down_e, w_gate_s, w_up_s, w_down_s, norm_final_w):
    h = x
    cond = jax.nn.silu(c)
    for i in range(DEPTH):
        mod = (jnp.dot(cond, w_mod[i]) + b_mod[i])[:, None, :]
        sh1, sc1, g1, sh2, sc2, g2 = jnp.split(mod, N_MOD, axis=-1)
        u = rms_norm(h, norm_mix_w[i]) * (1.0 + sc1) + sh1
        h = h + g1 * hybrid_mixer(u, positions, w_in[i], conv_w[i], conv_b[i], dt_bias[i], a_log[i],
                                  d_skip[i], ssm_norm_w[i], w_branch_attn[i], w_branch_ssm[i], w_out[i])
        u = rms_norm(h, norm_ffn_w[i]) * (1.0 + sc2) + sh2
        h = h + g2 * moe_ffn(u, w_router[i], router_bias[i], w_gate_e[i], w_up_e[i], w_down_e[i],
                             w_gate_s[i], w_up_s[i], w_down_s[i])
    return rms_norm(h, norm_final_w)
```

```python
import functools
import math

import jax
import jax.numpy as jnp
from jax import lax
from jax.experimental import pallas as pl
from jax.experimental.pallas import tpu as pltpu

F32 = jnp.float32
BF16 = jnp.bfloat16
I32 = jnp.int32

LANE = 128
SUBLANE = 8
VMEM_LIMIT = 56 * 1024 * 1024

HEAD_DIM = 128
HEADS_PER_GROUP = 4
GROUP_W = HEADS_PER_GROUP * HEAD_DIM
ATTN_DILATIONS = (1, 4, 16)
ATTN_BLK = 128
ROPE_DIM = 32
ROPE_HALF = 16
ROPE_THETA = 500000.0
SSM_HEADS = 16
SSM_HEADDIM = 64
SSM_GROUPS = 4
SSM_STATE = 128
SSM_CONV = 4
SSM_CHUNK = 128
N_EXPERTS = 256
TOP_K = 8
N_EXPERT_GROUPS = 8
TOPK_GROUPS = 4
ROUTED_SCALE = 2.5
NORM_EPS = 1e-6
NEG = -1e30

COL_XBC, COL_GA, COL_GS, COL_Z, COL_Q, COL_K, COL_V, COL_END = 0, 2048, 3072, 4096, 5120, 6656, 8192, 9728

INPROJ_TM, INPROJ_TN = 1024, 512
MIX_TM = 256
ROUTE_TM = 512
MOE_ROWS = 128
DISPATCH_TM = 256
COMBINE_TM = 256


def _cp(sem, vmem=VMEM_LIMIT):
    return pltpu.CompilerParams(dimension_semantics=sem, vmem_limit_bytes=vmem)


def _sigmoid(x):
    return 1.0 / (1.0 + jnp.exp(-x))


def _silu(x):
    return x * _sigmoid(x)


def _mod_kernel(c_ref, w_ref, b_ref, o_ref):
    cond = _silu(c_ref[...])
    o_ref[...] = jnp.dot(cond, w_ref[...], preferred_element_type=F32) + b_ref[...]


def _modulation(c, w_mod, b_mod):
    b, d = c.shape
    n = w_mod.shape[1]
    return pl.pallas_call(
        _mod_kernel,
        grid=(n // d,),
        in_specs=[pl.BlockSpec((b, d), lambda j: (0, 0)),
                  pl.BlockSpec((d, d), lambda j: (0, j)),
                  pl.BlockSpec((1, d), lambda j: (0, j))],
        out_specs=pl.BlockSpec((b, d), lambda j: (0, j)),
        out_shape=jax.ShapeDtypeStruct((b, n), F32),
        compiler_params=_cp(("arbitrary",)),
        name="modulation",
    )(c, w_mod, b_mod.reshape(1, n))


def _rope_kernel(pos_ref, inv_ref, c_ref, s1_ref, s2_ref):
    ang = pos_ref[...].astype(F32) * inv_ref[...]
    lane = lax.broadcasted_iota(I32, ang.shape, 1)
    cos = jnp.cos(ang)
    sin = jnp.sin(ang)
    c_ref[...] = jnp.where(lane < ROPE_DIM, cos, 1.0)
    s1_ref[...] = jnp.where(lane < ROPE_HALF, -sin, 0.0)
    s2_ref[...] = jnp.where((lane >= ROPE_HALF) & (lane < ROPE_DIM), sin, 0.0)


def _rope_tables(positions):
    t = positions.size
    tm = 2048
    inv_freq = ROPE_THETA ** (-jnp.arange(ROPE_HALF, dtype=F32) / ROPE_HALF)
    inv_row = jnp.concatenate([inv_freq, inv_freq, jnp.zeros((LANE - ROPE_DIM,), F32)]).reshape(1, LANE)
    spec = pl.BlockSpec((tm, LANE), lambda i: (i, 0))
    shp = jax.ShapeDtypeStruct((t, LANE), F32)
    return pl.pallas_call(
        _rope_kernel,
        grid=(t // tm,),
        in_specs=[pl.BlockSpec((tm, 1), lambda i: (i, 0)), pl.BlockSpec((1, LANE), lambda i: (0, 0))],
        out_specs=[spec, spec, spec],
        out_shape=[shp, shp, shp],
        compiler_params=_cp(("arbitrary",)),
        name="rope_tables",
    )(positions.reshape(t, 1), inv_row)


def _inproj_kernel(x_ref, sc_ref, sh_ref, nw_ref, w_ref, wdt_ref, c_ref, s1_ref, s2_ref,
                   o_ref, dt_ref, u_scr, *, tn, q_scale):
    j = pl.program_id(1)

    @pl.when(j == 0)
    def _():
        x = x_ref[...]
        ms = jnp.mean(x * x, axis=-1, keepdims=True)
        y = x * lax.rsqrt(ms + NORM_EPS) * nw_ref[...]
        u = (y * (1.0 + sc_ref[...]) + sh_ref[...]).astype(BF16)
        u_scr[...] = u
        dt_ref[...] = jnp.dot(u, wdt_ref[...], preferred_element_type=F32)

    acc = jnp.dot(u_scr[...], w_ref[...], preferred_element_type=F32)
    is_rope = (j >= COL_Q // tn) & (j < COL_V // tn)

    @pl.when(is_rope)
    def _():
        scale = jnp.where(j < COL_K // tn, q_scale, 1.0).astype(F32)
        c = c_ref[...] * scale
        s1 = s1_ref[...] * scale
        s2 = s2_ref[...] * scale
        for h in range(tn // HEAD_DIM):
            a = acc[:, h * HEAD_DIM:(h + 1) * HEAD_DIM]
            r = a * c + pltpu.roll(a, LANE - ROPE_HALF, 1) * s1 + pltpu.roll(a, ROPE_HALF, 1) * s2
            o_ref[:, h * HEAD_DIM:(h + 1) * HEAD_DIM] = r.astype(BF16)

    @pl.when(jnp.logical_not(is_rope))
    def _():
        o_ref[...] = acc.astype(BF16)


def _inproj(x2, sc1, sh1, norm_w, w_packed, w_dt, rope_c, rope_s1, rope_s2, seq):
    t, d = x2.shape
    tm, tn = INPROJ_TM, INPROJ_TN
    n = w_packed.shape[1]
    per_b = seq // tm
    row = lambda i, j: (i, 0)
    modrow = lambda i, j: (i // per_b, 0, 0)
    const = lambda i, j: (0, 0)
    return pl.pallas_call(
        functools.partial(_inproj_kernel, tn=tn, q_scale=1.0 / math.sqrt(HEAD_DIM)),
        grid=(t // tm, n // tn),
        in_specs=[pl.BlockSpec((tm, d), row),
                  pl.BlockSpec((None, 1, d), modrow),
                  pl.BlockSpec((None, 1, d), modrow),
                  pl.BlockSpec((1, d), const),
                  pl.BlockSpec((d, tn), lambda i, j: (0, j)),
                  pl.BlockSpec((d, LANE), const),
                  pl.BlockSpec((tm, LANE), row),
                  pl.BlockSpec((tm, LANE), row),
                  pl.BlockSpec((tm, LANE), row)],
        out_specs=[pl.BlockSpec((tm, tn), lambda i, j: (i, j)),
                   pl.BlockSpec((tm, LANE), row)],
        out_shape=[jax.ShapeDtypeStruct((t, n), BF16), jax.ShapeDtypeStruct((t, LANE), F32)],
        scratch_shapes=[pltpu.VMEM((tm, d), BF16)],
        compiler_params=_cp(("arbitrary", "arbitrary")),
        name="inproj",
    )(x2, sc1, sh1, norm_w, w_packed, w_dt, rope_c, rope_s1, rope_s2)


def _attn_kernel(q_ref, k_ref, v_ref, o_ref, st_ref, *, nb):
    blk = ATTN_BLK
    qi = lax.broadcasted_iota(I32, (blk, 2 * blk), 0)
    kj = lax.broadcasted_iota(I32, (blk, 2 * blk), 1)
    band = (kj >= qi) & (kj <= qi + blk)
    qi1 = lax.broadcasted_iota(I32, (blk, blk), 0)
    kj1 = lax.broadcasted_iota(I32, (blk, blk), 1)
    causal = kj1 <= qi1
    lane = kj1

    def block(q0, k0, nk, mask):
        st = jnp.zeros((blk, LANE), F32)
        for h in range(HEADS_PER_GROUP):
            hs = slice(h * HEAD_DIM, (h + 1) * HEAD_DIM)
            q = q_ref[pl.ds(q0, blk), hs]
            k = k_ref[pl.ds(k0, nk), hs]
            v = v_ref[pl.ds(k0, nk), hs]
            s = lax.dot_general(q, k, (((1,), (1,)), ((), ())), preferred_element_type=F32)
            s = jnp.where(mask, s, NEG)
            m = jnp.max(s, axis=-1, keepdims=True)
            p = jnp.exp(s - m)
            l = jnp.sum(p, axis=-1, keepdims=True)
            o = jnp.dot(p.astype(BF16), v, preferred_element_type=F32)
            o_ref[pl.ds(q0, blk), hs] = o / l
            st = jnp.where(lane == h, m, st)
            st = jnp.where(lane == HEADS_PER_GROUP + h, l, st)
        st_ref[pl.ds(q0, blk), :] = st

    block(0, 0, blk, causal)
    if nb > 1:
        def body(n, carry):
            q0 = pl.multiple_of(n * blk, blk)
            block(q0, pl.multiple_of(q0 - blk, blk), 2 * blk, band)
            return carry
        lax.fori_loop(1, nb, body, 0)


def _attention_group(proj, g, batch, seq):
    d = ATTN_DILATIONS[g]
    n_sub = seq // d
    nb = n_sub // ATTN_BLK
    ncol = proj.shape[1] // GROUP_W
    pv = proj.reshape(batch, n_sub, d * proj.shape[1])
    qc, kc, vc = COL_Q // GROUP_W + g, COL_K // GROUP_W + g, COL_V // GROUP_W + g
    o, st = pl.pallas_call(
        functools.partial(_attn_kernel, nb=nb),
        grid=(batch, d),
        in_specs=[pl.BlockSpec((None, n_sub, GROUP_W), lambda b, r: (b, 0, r * ncol + qc)),
                  pl.BlockSpec((None, n_sub, GROUP_W), lambda b, r: (b, 0, r * ncol + kc)),
                  pl.BlockSpec((None, n_sub, GROUP_W), lambda b, r: (b, 0, r * ncol + vc))],
        out_specs=[pl.BlockSpec((None, n_sub, GROUP_W), lambda b, r: (b, 0, r)),
                   pl.BlockSpec((None, n_sub, LANE), lambda b, r: (b, 0, r))],
        out_shape=[jax.ShapeDtypeStruct((batch, n_sub, d * GROUP_W), F32),
                   jax.ShapeDtypeStruct((batch, n_sub, d * LANE), F32)],
        compiler_params=_cp(("arbitrary", "arbitrary")),
        name=f"attn_d{d}",
    )(pv, pv, pv)
    return o.reshape(batch * seq, GROUP_W), st.reshape(batch * seq, LANE)


def _ssd_kernel(xbc_ref, z_ref, dt_ref, cw_ref, cb_ref, dtb_ref, alog_ref, dsk_ref, nw_ref, expand_ref,
                out_ref, xpad, state):
    L = SSM_CHUNK
    inner = SSM_HEADS * SSM_HEADDIM
    gw = SSM_STATE
    c = pl.program_id(1)

    @pl.when(c == 0)
    def _():
        xpad[0:SUBLANE, :] = jnp.zeros((SUBLANE, xpad.shape[1]), F32)
        state[...] = jnp.zeros(state.shape, F32)

    xpad[SUBLANE:SUBLANE + L, :] = xbc_ref[...].astype(F32)
    conv = cb_ref[...] + cw_ref[0:1, :] * xpad[SUBLANE - 3:SUBLANE - 3 + L, :]
    for k in range(1, SSM_CONV):
        conv = conv + cw_ref[k:k + 1, :] * xpad[SUBLANE - 3 + k:SUBLANE - 3 + k + L, :]
    xpad[0:SUBLANE, :] = xpad[L:L + SUBLANE, :]
    act = _silu(conv)
    xs = act[:, :inner]

    lane = lax.broadcasted_iota(I32, (L, LANE), 1)
    row = lax.broadcasted_iota(I32, (L, LANE), 0)
    dtr = dt_ref[...] + dtb_ref[...]
    dt = jnp.maximum(dtr, 0.0) + jnp.log(1.0 + jnp.exp(-jnp.abs(dtr)))
    a_neg = jnp.where(lane < SSM_HEADS, -jnp.exp(alog_ref[...]), 0.0)
    a = dt * a_neg
    cs = a
    shift = 1
    while shift < L:
        cs = cs + jnp.where(row >= shift, pltpu.roll(cs, shift, 0), 0.0)
        shift *= 2
    cs_t = cs.T
    tri = row >= lane
    dt_x = jnp.dot(dt, expand_ref[...], preferred_element_type=F32, precision=lax.Precision.HIGHEST)
    xp = (xs * dt_x).astype(BF16)
    half = lane < SSM_HEADDIM
    zero_b = jnp.zeros((L, LANE), BF16)

    for g in range(SSM_GROUPS):
        bg = act[:, inner + g * gw:inner + (g + 1) * gw]
        cg = act[:, inner + SSM_GROUPS * gw + g * gw:inner + SSM_GROUPS * gw + (g + 1) * gw]
        cg_b = cg.astype(BF16)
        cb = lax.dot_general(cg_b, bg.astype(BF16), (((1,), (1,)), ((), ())), preferred_element_type=F32)
        bg_t = bg.T
        for pair in range(2):
            h0 = g * 4 + pair * 2
            pidx = h0 // 2
            xpp = xp[:, pidx * LANE:(pidx + 1) * LANE]
            rhs = jnp.concatenate([jnp.where(half, xpp, zero_b), jnp.where(half, zero_b, xpp)], axis=0)
            dec, dst, eoff, cdec = [], [], [], []
            for h in (h0, h0 + 1):
                cs_col = cs[:, h:h + 1]
                cs_row = cs_t[h:h + 1, :]
                dec.append(cb * jnp.exp(jnp.where(tri, cs_col - cs_row, NEG)))
                cs_last = cs_row[:, L - 1:L]
                dst.append(bg_t * jnp.exp(cs_last - cs_row))
                eoff.append(jnp.exp(cs_col))
                cdec.append(jnp.exp(cs_last))
            y_diag = jnp.dot(jnp.concatenate(dec, axis=1).astype(BF16), rhs, preferred_element_type=F32)
            st_new = jnp.dot(jnp.concatenate(dst, axis=1).astype(BF16), rhs, preferred_element_type=F32)
            prev = state[pidx]
            y_off = jnp.dot(cg_b, prev.astype(BF16), preferred_element_type=F32)
            y_off = y_off * jnp.where(half, eoff[0], eoff[1])
            state[pidx] = prev * jnp.where(half, cdec[0], cdec[1]) + st_new
            y = y_diag + y_off + dsk_ref[:, pidx * LANE:(pidx + 1) * LANE] * xs[:, pidx * LANE:(pidx + 1) * LANE]
            out_pair = y * _silu(z_ref[:, pidx * LANE:(pidx + 1) * LANE].astype(F32))
            xpad[SUBLANE:SUBLANE + L, pidx * LANE:(pidx + 1) * LANE] = out_pair

    gsz = inner // SSM_GROUPS
    for g in range(SSM_GROUPS):
        yg = xpad[SUBLANE:SUBLANE + L, g * gsz:(g + 1) * gsz]
        ms = jnp.mean(yg * yg, axis=-1, keepdims=True)
        out_ref[:, g * gsz:(g + 1) * gsz] = (yg * lax.rsqrt(ms + NORM_EPS) * nw_ref[:, g * gsz:(g + 1) * gsz]).astype(BF16)


def _ssd(proj, dt_raw, conv_w, conv_b, dt_bias, a_log, d_skip, ssm_norm_w, batch, seq):
    t = batch * seq
    L = SSM_CHUNK
    nc = seq // L
    inner = SSM_HEADS * SSM_HEADDIM
    cdim = conv_w.shape[1]
    pad16 = lambda v: jnp.pad(v.astype(F32), (0, LANE - SSM_HEADS)).reshape(1, LANE)
    expand = (jnp.arange(LANE)[:, None] == (jnp.arange(inner)[None, :] // SSM_HEADDIM)).astype(F32)
    dsk = jnp.repeat(d_skip.astype(F32), SSM_HEADDIM).reshape(1, inner)
    rowc = lambda b, c: (b * nc + c, 0)
    const = lambda b, c: (0, 0)
    return pl.pallas_call(
        _ssd_kernel,
        grid=(batch, nc),
        in_specs=[pl.BlockSpec((L, cdim), lambda b, c: (b * nc + c, COL_XBC // cdim)),
                  pl.BlockSpec((L, inner), lambda b, c: (b * nc + c, COL_Z // inner)),
                  pl.BlockSpec((L, LANE), rowc),
                  pl.BlockSpec((SSM_CONV, cdim), const),
                  pl.BlockSpec((1, cdim), const),
                  pl.BlockSpec((1, LANE), const),
                  pl.BlockSpec((1, LANE), const),
                  pl.BlockSpec((1, inner), const),
                  pl.BlockSpec((1, inner), const),
                  pl.BlockSpec((LANE, inner), const)],
        out_specs=pl.BlockSpec((L, inner), rowc),
        out_shape=jax.ShapeDtypeStruct((t, inner), BF16),
        scratch_shapes=[pltpu.VMEM((L + 2 * SUBLANE, cdim), F32),
                        pltpu.VMEM((SSM_HEADS // 2, SSM_STATE, 2 * SSM_HEADDIM), F32)],
        compiler_params=_cp(("arbitrary", "arbitrary")),
        name="ssd",
    )(proj, proj, dt_raw, conv_w.astype(F32), conv_b.reshape(1, cdim).astype(F32), pad16(dt_bias), pad16(a_log),
      dsk, ssm_norm_w.reshape(1, inner).astype(F32), expand)


def _mix_kernel(o0_ref, o1_ref, o2_ref, s0_ref, s1_ref, s2_ref, ssm_ref, ga_ref, gs_ref, x_ref,
                g1_ref, sc2_ref, sh2_ref, g2_ref, nw_ref, wba_ref, wbs_ref, wo_ref, wrt_ref,
                wgs_ref, wus_ref, wds_ref, base_ref, u2_ref, sct_ref):
    o_refs = (o0_ref, o1_ref, o2_ref)
    s_refs = (s0_ref, s1_ref, s2_ref)
    heads = []
    for h in range(HEADS_PER_GROUP):
        ms = [s[:, h:h + 1] for s in s_refs]
        ls = [s[:, HEADS_PER_GROUP + h:HEADS_PER_GROUP + h + 1] for s in s_refs]
        mx = jnp.maximum(jnp.maximum(ms[0], ms[1]), ms[2])
        wts = [l * jnp.exp(m - mx) for m, l in zip(ms, ls)]
        hs = slice(h * HEAD_DIM, (h + 1) * HEAD_DIM)
        num = wts[0] * o_refs[0][:, hs] + wts[1] * o_refs[1][:, hs] + wts[2] * o_refs[2][:, hs]
        heads.append((num / (wts[0] + wts[1] + wts[2])).astype(BF16))
    attn = jnp.concatenate(heads, axis=1)
    ya = jnp.dot(attn, wba_ref[...], preferred_element_type=F32)
    ys = jnp.dot(ssm_ref[...], wbs_ref[...], preferred_element_type=F32)
    merged = _sigmoid(ga_ref[...].astype(F32)) * ya + _sigmoid(gs_ref[...].astype(F32)) * ys
    mix = jnp.dot(merged.astype(BF16), wo_ref[...], preferred_element_type=F32)
    h1 = x_ref[...] + g1_ref[...] * mix
    ms2 = jnp.mean(h1 * h1, axis=-1, keepdims=True)
    u2 = h1 * lax.rsqrt(ms2 + NORM_EPS) * nw_ref[...] * (1.0 + sc2_ref[...]) + sh2_ref[...]
    u2_ref[...] = u2
    logits_t = lax.dot_general(wrt_ref[...], u2, (((1,), (1,)), ((), ())), preferred_element_type=F32,
                               precision=lax.Precision.HIGHEST)
    sct_ref[...] = _sigmoid(logits_t)
    u2b = u2.astype(BF16)
    hs_ = _silu(jnp.dot(u2b, wgs_ref[...], preferred_element_type=F32)) * jnp.dot(u2b, wus_ref[...], preferred_element_type=F32)
    shared = jnp.dot(hs_.astype(BF16), wds_ref[...], preferred_element_type=F32)
    base_ref[...] = h1 + g2_ref[...] * shared


def _mix(o_list, st_list, ssm, proj, x2, g1, sc2, sh2, g2, norm_w, w_ba, w_bs, w_o, w_rt, w_gs, w_us, w_ds, seq):
    t, d = x2.shape
    tm = MIX_TM
    per_b = seq // tm
    row = lambda i: (i, 0)
    modrow = lambda i: (i // per_b, 0, 0)
    const = lambda i: (0, 0)
    full = lambda a: pl.BlockSpec(a.shape, const)
    mod = pl.BlockSpec((None, 1, d), modrow)
    return pl.pallas_call(
        _mix_kernel,
        grid=(t // tm,),
        in_specs=[pl.BlockSpec((tm, GROUP_W), row)] * 3 + [pl.BlockSpec((tm, LANE), row)] * 3 + [
            pl.BlockSpec((tm, d), row),
            pl.BlockSpec((tm, d), lambda i: (i, COL_GA // d)),
            pl.BlockSpec((tm, d), lambda i: (i, COL_GS // d)),
            pl.BlockSpec((tm, d), row),
            mod, mod, mod, mod, full(norm_w), full(w_ba), full(w_bs), full(w_o), full(w_rt),
            full(w_gs), full(w_us), full(w_ds)],
        out_specs=[pl.BlockSpec((tm, d), row), pl.BlockSpec((tm, d), row),
                   pl.BlockSpec((N_EXPERTS, tm), lambda i: (0, i))],
        out_shape=[jax.ShapeDtypeStruct((t, d), F32), jax.ShapeDtypeStruct((t, d), F32),
                   jax.ShapeDtypeStruct((N_EXPERTS, t), F32)],
        compiler_params=_cp(("arbitrary",)),
        name="mix",
    )(*o_list, *st_list, ssm, proj, proj, x2, g1, sc2, sh2, g2, norm_w, w_ba, w_bs, w_o, w_rt, w_gs, w_us, w_ds)


def _route_kernel(sct_ref, bias_ref, idx_ref, w_ref, rank_ref, cnt_ref, run):
    i = pl.program_id(0)
    tm = sct_ref.shape[1]
    per_g = N_EXPERTS // N_EXPERT_GROUPS

    @pl.when(i == 0)
    def _():
        run[...] = jnp.zeros(run.shape, F32)

    s = sct_ref[...]
    biased = s + bias_ref[...]
    io_g = lax.broadcasted_iota(I32, (per_g, tm), 0).astype(F32)
    gscore = []
    for g in range(N_EXPERT_GROUPS):
        bgp = biased[g * per_g:(g + 1) * per_g, :]
        m1 = jnp.max(bgp, axis=0, keepdims=True)
        first = jnp.min(jnp.where(bgp == m1, io_g, float(per_g)), axis=0, keepdims=True)
        m2 = jnp.max(jnp.where(io_g == first, NEG, bgp), axis=0, keepdims=True)
        gscore.append(m1 + m2)
    gs = jnp.concatenate(gscore, axis=0)
    io8 = lax.broadcasted_iota(I32, (N_EXPERT_GROUPS, tm), 0).astype(F32)
    gsel = jnp.zeros((N_EXPERT_GROUPS, tm), F32)
    cur = gs
    for _ in range(TOPK_GROUPS):
        mx = jnp.max(cur, axis=0, keepdims=True)
        fi = jnp.min(jnp.where(cur == mx, io8, float(N_EXPERT_GROUPS)), axis=0, keepdims=True)
        hit = io8 == fi
        gsel = jnp.where(hit, 1.0, gsel)
        cur = jnp.where(hit, NEG, cur)
    masked = jnp.concatenate(
        [jnp.where(gsel[g:g + 1, :] > 0.5, biased[g * per_g:(g + 1) * per_g, :], NEG) for g in range(N_EXPERT_GROUPS)],
        axis=0)
    io_e = lax.broadcasted_iota(I32, (N_EXPERTS, tm), 0).astype(F32)
    member = jnp.zeros((N_EXPERTS, tm), F32)
    idxs, ws = [], []
    for _ in range(TOP_K):
        mx = jnp.max(masked, axis=0, keepdims=True)
        fi = jnp.min(jnp.where(masked == mx, io_e, float(N_EXPERTS)), axis=0, keepdims=True)
        hit = io_e == fi
        idxs.append(fi)
        ws.append(jnp.sum(jnp.where(hit, s, 0.0), axis=0, keepdims=True))
        member = jnp.where(hit, 1.0, member)
        masked = jnp.where(hit, NEG, masked)
    wsum = ws[0]
    for k in range(1, TOP_K):
        wsum = wsum + ws[k]
    tr = lax.broadcasted_iota(I32, (tm, tm), 0)
    tc = lax.broadcasted_iota(I32, (tm, tm), 1)
    upper = jnp.where(tr < tc, 1.0, 0.0).astype(BF16)
    rank_full = jnp.dot(member.astype(BF16), upper, preferred_element_type=F32) + run[:, 0:1]
    ranks = [jnp.sum(jnp.where(io_e == fi, rank_full, 0.0), axis=0, keepdims=True) for fi in idxs]
    idx_ref[...] = jnp.concatenate(idxs, axis=0).astype(I32)
    w_ref[...] = jnp.concatenate([w / wsum * ROUTED_SCALE for w in ws], axis=0)
    rank_ref[...] = jnp.concatenate(ranks, axis=0).astype(I32)
    new_run = run[...] + jnp.sum(member, axis=1, keepdims=True)
    run[...] = new_run
    cnt_ref[...] = new_run


def _route(scores_t, router_bias):
    e, t = scores_t.shape
    tm = ROUTE_TM
    tok = pl.BlockSpec((TOP_K, tm), lambda i: (0, i))
    return pl.pallas_call(
        _route_kernel,
        grid=(t // tm,),
        in_specs=[pl.BlockSpec((e, tm), lambda i: (0, i)), pl.BlockSpec((e, 1), lambda i: (0, 0))],
        out_specs=[tok, tok, tok, pl.BlockSpec((e, LANE), lambda i: (0, 0))],
        out_shape=[jax.ShapeDtypeStruct((TOP_K, t), I32), jax.ShapeDtypeStruct((TOP_K, t), F32),
                   jax.ShapeDtypeStruct((TOP_K, t), I32), jax.ShapeDtypeStruct((e, LANE), F32)],
        scratch_shapes=[pltpu.VMEM((e, LANE), F32)],
        compiler_params=_cp(("arbitrary",)),
        name="route",
    )(scores_t, router_bias.reshape(e, 1).astype(F32))


def _dispatch_kernel(pend_ref, npad_ref, dest_ref, u_ref, xs_ref, zero_scr, sem, zsem):
    i = pl.program_id(0)
    tm = u_ref.shape[0]
    rb = zero_scr.shape[0]

    @pl.when(i == 0)
    def _():
        zero_scr[...] = jnp.zeros(zero_scr.shape, F32)

        def zcopy(e):
            return pltpu.make_async_copy(zero_scr, xs_ref.at[pl.ds(pl.multiple_of(pend_ref[e] - rb, rb), rb)], zsem)

        def zstart(e, c):
            @pl.when(npad_ref[e] > 0)
            def _():
                zcopy(e).start()
            return c

        def zwait(e, c):
            @pl.when(npad_ref[e] > 0)
            def _():
                zcopy(e).wait()
            return c

        lax.fori_loop(0, N_EXPERTS, zstart, 0)
        lax.fori_loop(0, N_EXPERTS, zwait, 0)

    def row_copy(t, k):
        return pltpu.make_async_copy(u_ref.at[pl.ds(t, 1)], xs_ref.at[pl.ds(dest_ref[k, t], 1)], sem)

    def start(t, c):
        for k in range(TOP_K):
            row_copy(t, k).start()
        return c

    def wait(t, c):
        for k in range(TOP_K):
            row_copy(t, k).wait()
        return c

    lax.fori_loop(0, tm, start, 0)
    lax.fori_loop(0, tm, wait, 0)


def _dispatch(u2, dest, pad_end, padded, n_rows):
    t, d = u2.shape
    tm = DISPATCH_TM
    gs = pltpu.PrefetchScalarGridSpec(
        num_scalar_prefetch=2,
        grid=(t // tm,),
        in_specs=[pl.BlockSpec((TOP_K, tm), lambda i, pe, npd: (0, i), memory_space=pltpu.SMEM),
                  pl.BlockSpec((tm, d), lambda i, pe, npd: (i, 0))],
        out_specs=pl.BlockSpec(memory_space=pl.ANY),
        scratch_shapes=[pltpu.VMEM((MOE_ROWS, d), F32), pltpu.SemaphoreType.DMA, pltpu.SemaphoreType.DMA],
    )
    return pl.pallas_call(
        _dispatch_kernel,
        grid_spec=gs,
        out_shape=jax.ShapeDtypeStruct((n_rows, d), F32),
        compiler_params=_cp(("arbitrary",)),
        name="dispatch",
    )(pad_end, padded, dest, u2)


def _expert_kernel(be_ref, nu_ref, xs_ref, wg_ref, wu_ref, wd_ref, ys_ref):
    i = pl.program_id(0)

    @pl.when(i < nu_ref[0])
    def _():
        xb = xs_ref[...].astype(BF16)
        g = jnp.dot(xb, wg_ref[...].astype(BF16), preferred_element_type=F32)
        u = jnp.dot(xb, wu_ref[...].astype(BF16), preferred_element_type=F32)
        h = (_silu(g) * u).astype(BF16)
        ys_ref[...] = jnp.dot(h, wd_ref[...].astype(BF16), preferred_element_type=F32)

    @pl.when(i >= nu_ref[0])
    def _():
        ys_ref[...] = jnp.zeros(ys_ref.shape, F32)


def _experts(xs, blk_e, n_used, w_gate_e, w_up_e, w_down_e):
    n_rows, d = xs.shape
    ff = w_gate_e.shape[2]
    rb = MOE_ROWS
    gs = pltpu.PrefetchScalarGridSpec(
        num_scalar_prefetch=2,
        grid=(n_rows // rb,),
        in_specs=[pl.BlockSpec((rb, d), lambda i, be, nu: (jnp.minimum(i, nu[0] - 1), 0)),
                  pl.BlockSpec((None, d, ff), lambda i, be, nu: (be[i], 0, 0)),
                  pl.BlockSpec((None, d, ff), lambda i, be, nu: (be[i], 0, 0)),
                  pl.BlockSpec((None, ff, d), lambda i, be, nu: (be[i], 0, 0))],
        out_specs=pl.BlockSpec((rb, d), lambda i, be, nu: (i, 0)),
    )
    return pl.pallas_call(
        _expert_kernel,
        grid_spec=gs,
        out_shape=jax.ShapeDtypeStruct((n_rows, d), F32),
        compiler_params=_cp(("arbitrary",)),
        name="experts",
    )(blk_e, n_used, xs, w_gate_e, w_up_e, w_down_e)


def _combine_kernel(dest_ref, wt_ref, base_ref, g2_ref, nfw_ref, ys_ref, out_ref, buf, sem):
    tm = base_ref.shape[0]

    def row_copy(t, k):
        return pltpu.make_async_copy(ys_ref.at[pl.ds(dest_ref[k, t], 1)], buf.at[k, pl.ds(t, 1)], sem)

    def start(t, c):
        for k in range(TOP_K):
            row_copy(t, k).start()
        return c

    def wait(t, c):
        for k in range(TOP_K):
            row_copy(t, k).wait()
        return c

    lax.fori_loop(0, tm, start, 0)
    lax.fori_loop(0, tm, wait, 0)
    routed = buf[0] * wt_ref[:, 0:1]
    for k in range(1, TOP_K):
        routed = routed + buf[k] * wt_ref[:, k:k + 1]
    h2 = base_ref[...] + g2_ref[...] * routed
    ms = jnp.mean(h2 * h2, axis=-1, keepdims=True)
    out_ref[...] = h2 * lax.rsqrt(ms + NORM_EPS) * nfw_ref[...]


def _combine(ys, dest, w_tok, base, g2, norm_final_w, seq):
    t, d = base.shape
    tm = COMBINE_TM
    per_b = seq // tm
    gs = pltpu.PrefetchScalarGridSpec(
        num_scalar_prefetch=0,
        grid=(t // tm,),
        in_specs=[pl.BlockSpec((TOP_K, tm), lambda i: (0, i), memory_space=pltpu.SMEM),
                  pl.BlockSpec((tm, TOP_K), lambda i: (i, 0)),
                  pl.BlockSpec((tm, d), lambda i: (i, 0)),
                  pl.BlockSpec((None, 1, d), lambda i: (i // per_b, 0, 0)),
                  pl.BlockSpec((1, d), lambda i: (0, 0)),
                  pl.BlockSpec(memory_space=pl.ANY)],
        out_specs=pl.BlockSpec((tm, d), lambda i: (i, 0)),
        scratch_shapes=[pltpu.VMEM((TOP_K, tm, d), F32), pltpu.SemaphoreType.DMA],
    )
    return pl.pallas_call(
        _combine_kernel,
        grid_spec=gs,
        out_shape=jax.ShapeDtypeStruct((t, d), F32),
        compiler_params=_cp(("arbitrary",)),
        name="combine",
    )(dest, w_tok, base, g2, norm_final_w, ys)


def kernel(x, c, positions, w_mod, b_mod, norm_mix_w, norm_ffn_w, w_in, conv_w, conv_b, dt_bias, a_log, d_skip,
           ssm_norm_w, w_branch_attn, w_branch_ssm, w_out, w_router, router_bias, w_gate_e, w_up_e, w_down_e,
           w_gate_s, w_up_s, w_down_s, norm_final_w):
    batch, seq, d = x.shape
    t = batch * seq
    assert w_mod.shape[0] == 1, "one layer"
    assert seq % (ATTN_DILATIONS[-1] * ATTN_BLK) == 0 and seq % INPROJ_TM == 0

    mod = _modulation(c, w_mod[0], b_mod[0])
    sh1, sc1, g1, sh2, sc2, g2 = [m.reshape(batch, 1, d) for m in jnp.split(mod, 6, axis=-1)]
    rope_c, rope_s1, rope_s2 = _rope_tables(positions)

    wi = w_in[0]
    q_dim = 3 * GROUP_W
    o_z = 3 * q_dim
    o_xbc = o_z + d
    o_dt = o_xbc + conv_w.shape[2]
    o_g = o_dt + SSM_HEADS
    w_packed = jnp.concatenate([wi[:, o_xbc:o_dt], wi[:, o_g:], wi[:, o_z:o_xbc], wi[:, :o_z]], axis=1).astype(BF16)
    assert w_packed.shape[1] == COL_END
    w_dt = jnp.pad(wi[:, o_dt:o_g], ((0, 0), (0, LANE - SSM_HEADS))).astype(BF16)

    x2 = x.reshape(t, d)
    proj, dt_raw = _inproj(x2, sc1, sh1, norm_mix_w.reshape(1, d), w_packed, w_dt, rope_c, rope_s1, rope_s2, seq)

    o_list, st_list = [], []
    for g in range(len(ATTN_DILATIONS)):
        o, st = _attention_group(proj, g, batch, seq)
        o_list.append(o)
        st_list.append(st)
    ssm = _ssd(proj, dt_raw, conv_w[0], conv_b[0], dt_bias[0], a_log[0], d_skip[0], ssm_norm_w[0], batch, seq)

    base, u2, scores_t = _mix(
        o_list, st_list, ssm, proj, x2, g1, sc2, sh2, g2, norm_ffn_w.reshape(1, d),
        w_branch_attn[0].astype(BF16), w_branch_ssm[0].astype(BF16), w_out[0].astype(BF16),
        w_router[0].T, w_gate_s[0].astype(BF16), w_up_s[0].astype(BF16), w_down_s[0].astype(BF16), seq)

    idx, w_sel, rank, counts = _route(scores_t, router_bias[0])

    rb = MOE_ROWS
    cnt = counts[:, 0].astype(I32)
    padded = (cnt + rb - 1) // rb * rb
    pad_end = jnp.cumsum(padded).astype(I32)
    pad_start = pad_end - padded
    n_rows = t * TOP_K + N_EXPERTS * rb
    n_blk = n_rows // rb
    dest = jnp.take(pad_start, idx) + rank
    blk_e = jnp.minimum(jnp.searchsorted(pad_end, jnp.arange(n_blk, dtype=I32) * rb, side="right"),
                        N_EXPERTS - 1).astype(I32)
    n_used = (pad_end[-1:] // rb).astype(I32)

    xs = _dispatch(u2, dest, pad_end, padded, n_rows)
    ys = _experts(xs, blk_e, n_used, w_gate_e[0], w_up_e[0], w_down_e[0])
    out = _combine(ys, dest, w_sel.T, base, g2, norm_final_w.reshape(1, d), seq)
    return out.reshape(batch, seq, d)
```

```python
import functools
import math

import jax
import jax.numpy as jnp
from jax import lax
from jax.experimental import pallas as pl
from jax.experimental.pallas import tpu as pltpu

F32 = jnp.float32
BF16 = jnp.bfloat16
I32 = jnp.int32
U32 = jnp.uint32

LANE = 128
SUBLANE = 8
VMEM_LIMIT = 56 * 1024 * 1024

HEAD_DIM = 128
HEADS_PER_GROUP = 4
GROUP_W = HEADS_PER_GROUP * HEAD_DIM
ATTN_DILATIONS = (1, 4, 16)
ATTN_BLK = 128
ROPE_DIM = 32
ROPE_HALF = 16
ROPE_THETA = 500000.0
SSM_HEADS = 16
SSM_HEADDIM = 64
SSM_GROUPS = 4
SSM_STATE = 128
SSM_CONV = 4
SSM_CHUNK = 128
N_EXPERTS = 256
TOP_K = 8
N_EXPERT_GROUPS = 8
TOPK_GROUPS = 4
ROUTED_SCALE = 2.5
NORM_EPS = 1e-6
NEG = -1e30

COL_XBC, COL_GA, COL_GS, COL_Z, COL_Q0 = 0, 2048, 3072, 4096, 5120
MAIN_W = 6656
QKV_W = 3 * GROUP_W

INPROJ_TM, INPROJ_TN = 1024, 512
MIX_TM = 256
ROUTE_TM = 512
MOE_ROWS = 256
DISPATCH_TM = 256
COMBINE_TM = 256
HALF_D = 512


def _cp(sem, vmem=VMEM_LIMIT):
    return pltpu.CompilerParams(dimension_semantics=sem, vmem_limit_bytes=vmem)


def _sigmoid(x):
    return 1.0 / (1.0 + jnp.exp(-x))


def _silu(x):
    return x * _sigmoid(x)


def _pack_pair(a, b):
    ua = lax.bitcast_convert_type(a.astype(BF16).astype(F32), U32)
    ub = lax.bitcast_convert_type(b.astype(BF16).astype(F32), U32)
    return (ua >> 16) | ub


def _unpack_pair(w):
    lo = lax.bitcast_convert_type(w << 16, F32)
    hi = lax.bitcast_convert_type(w & jnp.uint32(0xFFFF0000), F32)
    return lo, hi


def _mod_kernel(c_ref, w_ref, b_ref, o_ref):
    cond = _silu(c_ref[...])
    o_ref[...] = jnp.dot(cond, w_ref[...], preferred_element_type=F32) + b_ref[...]


def _modulation(c, w_mod, b_mod):
    b, d = c.shape
    n = w_mod.shape[1]
    return pl.pallas_call(
        _mod_kernel,
        grid=(n // d,),
        in_specs=[pl.BlockSpec((b, d), lambda j: (0, 0)),
                  pl.BlockSpec((d, d), lambda j: (0, j)),
                  pl.BlockSpec((1, d), lambda j: (0, j))],
        out_specs=pl.BlockSpec((b, d), lambda j: (0, j)),
        out_shape=jax.ShapeDtypeStruct((b, n), F32),
        compiler_params=_cp(("arbitrary",)),
        name="modulation",
    )(c, w_mod, b_mod.reshape(1, n))


def _rope_kernel(pos_ref, inv_ref, c_ref, s1_ref, s2_ref):
    ang = pos_ref[...].astype(F32) * inv_ref[...]
    lane = lax.broadcasted_iota(I32, ang.shape, 1)
    cos = jnp.cos(ang)
    sin = jnp.sin(ang)
    c_ref[...] = jnp.where(lane < ROPE_DIM, cos, 1.0)
    s1_ref[...] = jnp.where(lane < ROPE_HALF, -sin, 0.0)
    s2_ref[...] = jnp.where((lane >= ROPE_HALF) & (lane < ROPE_DIM), sin, 0.0)


def _rope_tables(positions):
    t = positions.size
    tm = 2048
    inv_freq = ROPE_THETA ** (-jnp.arange(ROPE_HALF, dtype=F32) / ROPE_HALF)
    inv_row = jnp.concatenate([inv_freq, inv_freq, jnp.zeros((LANE - ROPE_DIM,), F32)]).reshape(1, LANE)
    spec = pl.BlockSpec((tm, LANE), lambda i: (i, 0))
    shp = jax.ShapeDtypeStruct((t, LANE), F32)
    return pl.pallas_call(
        _rope_kernel,
        grid=(t // tm,),
        in_specs=[pl.BlockSpec((tm, 1), lambda i: (i, 0)), pl.BlockSpec((1, LANE), lambda i: (0, 0))],
        out_specs=[spec, spec, spec],
        out_shape=[shp, shp, shp],
        compiler_params=_cp(("arbitrary",)),
        name="rope_tables",
    )(positions.reshape(t, 1), inv_row)


def _inproj_kernel(x_ref, sc_ref, sh_ref, nw_ref, w_ref, wdt_ref, c_ref, s1_ref, s2_ref,
                   main_ref, g1_ref, g2_ref, dt_ref, u_scr, rope_scr, uc_scr, *, tn, q_scale):
    j = pl.program_id(1)
    tm = x_ref.shape[0]
    n_main = MAIN_W // tn
    n_qkv = QKV_W // tn

    @pl.when(j == 0)
    def _():
        x = x_ref[...]
        ms = jnp.mean(x * x, axis=-1, keepdims=True)
        y = x * lax.rsqrt(ms + NORM_EPS) * nw_ref[...]
        uf = y * (1.0 + sc_ref[...]) + sh_ref[...]
        u = uf.astype(BF16)
        u_scr[0] = u
        dt_ref[...] = jnp.dot(u, wdt_ref[...], preferred_element_type=F32)
        for ti, tab in enumerate((c_ref, s1_ref, s2_ref)):
            rope_scr[0, ti] = tab[...]
        n_chunk = uf.shape[1] // LANE
        for cc in range(n_chunk):
            uc_scr[cc] = uf[:, cc * LANE:(cc + 1) * LANE]
        for o, d in ((1, ATTN_DILATIONS[1]), (2, ATTN_DILATIONS[2])):
            rows = tm // d
            for r in range(d):
                for cc in range(n_chunk):
                    u_scr[o, r * rows:(r + 1) * rows, cc * LANE:(cc + 1) * LANE] = (
                        uc_scr[cc, pl.ds(r, rows, stride=d), :].astype(BF16))
                for ti, tab in enumerate((c_ref, s1_ref, s2_ref)):
                    rope_scr[o, ti, r * rows:(r + 1) * rows, :] = tab[pl.ds(r, rows, stride=d), :]

    order = jnp.where(j < n_main, 0, jnp.where(j < n_main + n_qkv, 1, 2))
    acc = jnp.dot(u_scr[order], w_ref[...], preferred_element_type=F32)
    jq = jnp.where(j < n_main, j - COL_Q0 // tn, jnp.where(j < n_main + n_qkv, j - n_main, j - n_main - n_qkv))
    is_rope = (jq == 0) | (jq == 1)

    def emit(val):
        @pl.when(order == 0)
        def _():
            main_ref[...] = val.astype(BF16)

        @pl.when(order == 1)
        def _():
            g1_ref[...] = val.astype(BF16).reshape(g1_ref.shape)

        @pl.when(order == 2)
        def _():
            g2_ref[...] = val.astype(BF16).reshape(g2_ref.shape)

    @pl.when(is_rope)
    def _():
        scale = jnp.where(jq == 0, q_scale, 1.0).astype(F32)
        c = rope_scr[order, 0] * scale
        s1 = rope_scr[order, 1] * scale
        s2 = rope_scr[order, 2] * scale
        parts = []
        for h in range(tn // HEAD_DIM):
            a = acc[:, h * HEAD_DIM:(h + 1) * HEAD_DIM]
            parts.append(a * c + pltpu.roll(a, LANE - ROPE_HALF, 1) * s1 + pltpu.roll(a, ROPE_HALF, 1) * s2)
        emit(jnp.concatenate(parts, axis=1))

    @pl.when(jnp.logical_not(is_rope))
    def _():
        emit(acc)


def _inproj(x2, sc1, sh1, norm_w, w_packed, w_dt, rope_c, rope_s1, rope_s2, batch, seq):
    t, d = x2.shape
    tm, tn = INPROJ_TM, INPROJ_TN
    n = w_packed.shape[1]
    per_b = seq // tm
    n_main = MAIN_W // tn
    n_qkv = QKV_W // tn
    d1, d2 = ATTN_DILATIONS[1], ATTN_DILATIONS[2]
    row = lambda i, j: (i, 0)
    modrow = lambda i, j: (i // per_b, 0, 0)
    const = lambda i, j: (0, 0)
    main_map = lambda i, j: (i, jnp.minimum(j, n_main - 1))
    g1_map = lambda i, j: (i // per_b, 0, i % per_b, jnp.clip(j - n_main, 0, n_qkv - 1))
    g2_map = lambda i, j: (i // per_b, 0, i % per_b, jnp.clip(j - n_main - n_qkv, 0, n_qkv - 1))
    return pl.pallas_call(
        functools.partial(_inproj_kernel, tn=tn, q_scale=1.0 / math.sqrt(HEAD_DIM)),
        grid=(t // tm, n // tn),
        in_specs=[pl.BlockSpec((tm, d), row),
                  pl.BlockSpec((None, 1, d), modrow),
                  pl.BlockSpec((None, 1, d), modrow),
                  pl.BlockSpec((1, d), const),
                  pl.BlockSpec((d, tn), lambda i, j: (0, j)),
                  pl.BlockSpec((d, LANE), const),
                  pl.BlockSpec((tm, LANE), row),
                  pl.BlockSpec((tm, LANE), row),
                  pl.BlockSpec((tm, LANE), row)],
        out_specs=[pl.BlockSpec((tm, tn), main_map),
                   pl.BlockSpec((None, d1, tm // d1, tn), g1_map),
                   pl.BlockSpec((None, d2, tm // d2, tn), g2_map),
                   pl.BlockSpec((tm, LANE), row)],
        out_shape=[jax.ShapeDtypeStruct((t, MAIN_W), BF16),
                   jax.ShapeDtypeStruct((batch, d1, seq // d1, QKV_W), BF16),
                   jax.ShapeDtypeStruct((batch, d2, seq // d2, QKV_W), BF16),
                   jax.ShapeDtypeStruct((t, LANE), F32)],
        scratch_shapes=[pltpu.VMEM((3, tm, d), BF16), pltpu.VMEM((3, 3, tm, LANE), F32),
                        pltpu.VMEM((d // LANE, tm, LANE), F32)],
        compiler_params=_cp(("arbitrary", "arbitrary")),
        name="inproj",
    )(x2, sc1, sh1, norm_w, w_packed, w_dt, rope_c, rope_s1, rope_s2)


def _attn_kernel(q_ref, k_ref, v_ref, o_ref, st_ref, *, d, nb):
    blk = ATTN_BLK
    qi = lax.broadcasted_iota(I32, (blk, 2 * blk), 0)
    kj = lax.broadcasted_iota(I32, (blk, 2 * blk), 1)
    band = (kj >= qi) & (kj <= qi + blk)
    qi1 = lax.broadcasted_iota(I32, (blk, blk), 0)
    kj1 = lax.broadcasted_iota(I32, (blk, blk), 1)
    causal = kj1 <= qi1
    lane = kj1

    def block(r, q0, k0, nk, mask):
        rows = pl.ds(q0, blk) if d == 1 else pl.ds(q0 * d + r, blk, stride=d)
        st = jnp.zeros((blk, LANE), F32)
        for h in range(HEADS_PER_GROUP):
            hs = slice(h * HEAD_DIM, (h + 1) * HEAD_DIM)
            q = q_ref[r, pl.ds(q0, blk), hs]
            k = k_ref[r, pl.ds(k0, nk), hs]
            v = v_ref[r, pl.ds(k0, nk), hs]
            s = lax.dot_general(q, k, (((1,), (1,)), ((), ())), preferred_element_type=F32)
            s = jnp.where(mask, s, NEG)
            m = jnp.max(s, axis=-1, keepdims=True)
            p = jnp.exp(s - m)
            l = jnp.sum(p, axis=-1, keepdims=True)
            o = jnp.dot(p.astype(BF16), v, preferred_element_type=F32)
            o_ref[h, rows, :] = o / l
            st = jnp.where(lane == h, m, st)
            st = jnp.where(lane == HEADS_PER_GROUP + h, l, st)
        st_ref[rows, :] = st

    def per_residue(r, carry):
        block(r, 0, 0, blk, causal)
        if nb > 1:
            def body(n, c):
                q0 = pl.multiple_of(n * blk, blk)
                block(r, q0, pl.multiple_of(q0 - blk, blk), 2 * blk, band)
                return c
            lax.fori_loop(1, nb, body, 0)
        return carry

    if d == 1:
        per_residue(0, 0)
    else:
        lax.fori_loop(0, d, per_residue, 0)


def _attention_group(src, g, batch, seq, col0):
    d = ATTN_DILATIONS[g]
    n_sub = seq // d
    nb = n_sub // ATTN_BLK
    spec = lambda c: pl.BlockSpec((None, d, n_sub, GROUP_W), lambda b: (b, 0, 0, c))
    o, st = pl.pallas_call(
        functools.partial(_attn_kernel, d=d, nb=nb),
        grid=(batch,),
        in_specs=[spec(col0), spec(col0 + 1), spec(col0 + 2)],
        out_specs=[pl.BlockSpec((HEADS_PER_GROUP, seq, HEAD_DIM), lambda b: (0, b, 0)),
                   pl.BlockSpec((seq, LANE), lambda b: (b, 0))],
        out_shape=[jax.ShapeDtypeStruct((HEADS_PER_GROUP, batch * seq, HEAD_DIM), F32),
                   jax.ShapeDtypeStruct((batch * seq, LANE), F32)],
        compiler_params=_cp(("arbitrary",)),
        name=f"attn_d{d}",
    )(src, src, src)
    return o, st


def _ssd_kernel(xbc_ref, z_ref, dt_ref, cw_ref, cb_ref, dtb_ref, alog_ref, dsk_ref, nw_ref, expand_ref,
                out_ref, xpad, state):
    L = SSM_CHUNK
    inner = SSM_HEADS * SSM_HEADDIM
    gw = SSM_STATE
    c = pl.program_id(1)

    @pl.when(c == 0)
    def _():
        xpad[0:SUBLANE, :] = jnp.zeros((SUBLANE, xpad.shape[1]), F32)
        state[...] = jnp.zeros(state.shape, F32)

    xpad[SUBLANE:SUBLANE + L, :] = xbc_ref[...].astype(F32)
    conv = cb_ref[...] + cw_ref[0:1, :] * xpad[SUBLANE - 3:SUBLANE - 3 + L, :]
    for k in range(1, SSM_CONV):
        conv = conv + cw_ref[k:k + 1, :] * xpad[SUBLANE - 3 + k:SUBLANE - 3 + k + L, :]
    xpad[0:SUBLANE, :] = xpad[L:L + SUBLANE, :]
    act = _silu(conv)
    xs = act[:, :inner]

    lane = lax.broadcasted_iota(I32, (L, LANE), 1)
    row = lax.broadcasted_iota(I32, (L, LANE), 0)
    dtr = dt_ref[...] + dtb_ref[...]
    dt = jnp.maximum(dtr, 0.0) + jnp.log(1.0 + jnp.exp(-jnp.abs(dtr)))
    a_neg = jnp.where(lane < SSM_HEADS, -jnp.exp(alog_ref[...]), 0.0)
    a = dt * a_neg
    cs = a
    shift = 1
    while shift < L:
        cs = cs + jnp.where(row >= shift, pltpu.roll(cs, shift, 0), 0.0)
        shift *= 2
    cs_t = cs.T
    tri = row >= lane
    dt_x = jnp.dot(dt, expand_ref[...], preferred_element_type=F32, precision=lax.Precision.HIGHEST)
    xp = (xs * dt_x).astype(BF16)
    half = lane < SSM_HEADDIM
    zero_b = jnp.zeros((L, LANE), BF16)

    for g in range(SSM_GROUPS):
        bg = act[:, inner + g * gw:inner + (g + 1) * gw]
        cg = act[:, inner + SSM_GROUPS * gw + g * gw:inner + SSM_GROUPS * gw + (g + 1) * gw]
        cg_b = cg.astype(BF16)
        cb = lax.dot_general(cg_b, bg.astype(BF16), (((1,), (1,)), ((), ())), preferred_element_type=F32)
        bg_t = bg.T
        for pair in range(2):
            h0 = g * 4 + pair * 2
            pidx = h0 // 2
            xpp = xp[:, pidx * LANE:(pidx + 1) * LANE]
            rhs = jnp.concatenate([jnp.where(half, xpp, zero_b), jnp.where(half, zero_b, xpp)], axis=0)
            dec, dst, eoff, cdec = [], [], [], []
            for h in (h0, h0 + 1):
                cs_col = cs[:, h:h + 1]
                cs_row = cs_t[h:h + 1, :]
                dec.append(cb * jnp.exp(jnp.where(tri, cs_col - cs_row, NEG)))
                cs_last = cs_row[:, L - 1:L]
                dst.append(bg_t * jnp.exp(cs_last - cs_row))
                eoff.append(jnp.exp(cs_col))
                cdec.append(jnp.exp(cs_last))
            y_diag = jnp.dot(jnp.concatenate(dec, axis=1).astype(BF16), rhs, preferred_element_type=F32)
            st_new = jnp.dot(jnp.concatenate(dst, axis=1).astype(BF16), rhs, preferred_element_type=F32)
            prev = state[pidx]
            y_off = jnp.dot(cg_b, prev.astype(BF16), preferred_element_type=F32)
            y_off = y_off * jnp.where(half, eoff[0], eoff[1])
            state[pidx] = prev * jnp.where(half, cdec[0], cdec[1]) + st_new
            y = y_diag + y_off + dsk_ref[:, pidx * LANE:(pidx + 1) * LANE] * xs[:, pidx * LANE:(pidx + 1) * LANE]
            out_pair = y * _silu(z_ref[:, pidx * LANE:(pidx + 1) * LANE].astype(F32))
            xpad[SUBLANE:SUBLANE + L, pidx * LANE:(pidx + 1) * LANE] = out_pair

    gsz = inner // SSM_GROUPS
    for g in range(SSM_GROUPS):
        yg = xpad[SUBLANE:SUBLANE + L, g * gsz:(g + 1) * gsz]
        ms = jnp.mean(yg * yg, axis=-1, keepdims=True)
        out_ref[:, g * gsz:(g + 1) * gsz] = (yg * lax.rsqrt(ms + NORM_EPS) * nw_ref[:, g * gsz:(g + 1) * gsz]).astype(BF16)


def _ssd(proj, dt_raw, conv_w, conv_b, dt_bias, a_log, d_skip, ssm_norm_w, batch, seq):
    t = batch * seq
    L = SSM_CHUNK
    nc = seq // L
    inner = SSM_HEADS * SSM_HEADDIM
    cdim = conv_w.shape[1]
    pad16 = lambda v: jnp.pad(v.astype(F32), (0, LANE - SSM_HEADS)).reshape(1, LANE)
    expand = (jnp.arange(LANE)[:, None] == (jnp.arange(inner)[None, :] // SSM_HEADDIM)).astype(F32)
    dsk = jnp.repeat(d_skip.astype(F32), SSM_HEADDIM).reshape(1, inner)
    rowc = lambda b, c: (b * nc + c, 0)
    const = lambda b, c: (0, 0)
    return pl.pallas_call(
        _ssd_kernel,
        grid=(batch, nc),
        in_specs=[pl.BlockSpec((L, cdim), lambda b, c: (b * nc + c, COL_XBC // cdim)),
                  pl.BlockSpec((L, inner), lambda b, c: (b * nc + c, COL_Z // inner)),
                  pl.BlockSpec((L, LANE), rowc),
                  pl.BlockSpec((SSM_CONV, cdim), const),
                  pl.BlockSpec((1, cdim), const),
                  pl.BlockSpec((1, LANE), const),
                  pl.BlockSpec((1, LANE), const),
                  pl.BlockSpec((1, inner), const),
                  pl.BlockSpec((1, inner), const),
                  pl.BlockSpec((LANE, inner), const)],
        out_specs=pl.BlockSpec((L, inner), rowc),
        out_shape=jax.ShapeDtypeStruct((t, inner), BF16),
        scratch_shapes=[pltpu.VMEM((L + 2 * SUBLANE, cdim), F32),
                        pltpu.VMEM((SSM_HEADS // 2, SSM_STATE, 2 * SSM_HEADDIM), F32)],
        compiler_params=_cp(("arbitrary", "arbitrary")),
        name="ssd",
    )(proj, proj, dt_raw, conv_w.astype(F32), conv_b.reshape(1, cdim).astype(F32), pad16(dt_bias), pad16(a_log),
      dsk, ssm_norm_w.reshape(1, inner).astype(F32), expand)


def _mix_kernel(o0_ref, o1_ref, o2_ref, s0_ref, s1_ref, s2_ref, ssm_ref, ga_ref, gs_ref, x_ref,
                g1_ref, sc2_ref, sh2_ref, g2_ref, nw_ref, wba_ref, wbs_ref, wo_ref, wrt_ref,
                wgs_ref, wus_ref, wds_ref, base_ref, u2p_ref, sct_ref):
    o_refs = (o0_ref, o1_ref, o2_ref)
    s_refs = (s0_ref, s1_ref, s2_ref)
    heads = []
    for h in range(HEADS_PER_GROUP):
        ms = [s[:, h:h + 1] for s in s_refs]
        ls = [s[:, HEADS_PER_GROUP + h:HEADS_PER_GROUP + h + 1] for s in s_refs]
        mx = jnp.maximum(jnp.maximum(ms[0], ms[1]), ms[2])
        wts = [l * jnp.exp(m - mx) for m, l in zip(ms, ls)]
        num = wts[0] * o_refs[0][h] + wts[1] * o_refs[1][h] + wts[2] * o_refs[2][h]
        heads.append((num / (wts[0] + wts[1] + wts[2])).astype(BF16))
    attn = jnp.concatenate(heads, axis=1)
    ya = jnp.dot(attn, wba_ref[...], preferred_element_type=F32)
    ys = jnp.dot(ssm_ref[...], wbs_ref[...], preferred_element_type=F32)
    merged = _sigmoid(ga_ref[...].astype(F32)) * ya + _sigmoid(gs_ref[...].astype(F32)) * ys
    mix = jnp.dot(merged.astype(BF16), wo_ref[...], preferred_element_type=F32)
    h1 = x_ref[...] + g1_ref[...] * mix
    ms2 = jnp.mean(h1 * h1, axis=-1, keepdims=True)
    u2 = h1 * lax.rsqrt(ms2 + NORM_EPS) * nw_ref[...] * (1.0 + sc2_ref[...]) + sh2_ref[...]
    u2p_ref[...] = _pack_pair(u2[:, :HALF_D], u2[:, HALF_D:])
    logits_t = lax.dot_general(wrt_ref[...], u2, (((1,), (1,)), ((), ())), preferred_element_type=F32,
                               precision=lax.Precision.HIGHEST)
    sct_ref[...] = _sigmoid(logits_t)
    u2b = u2.astype(BF16)
    hs_ = _silu(jnp.dot(u2b, wgs_ref[...], preferred_element_type=F32)) * jnp.dot(u2b, wus_ref[...], preferred_element_type=F32)
    shared = jnp.dot(hs_.astype(BF16), wds_ref[...], preferred_element_type=F32)
    base_ref[...] = h1 + g2_ref[...] * shared


def _mix(o_list, st_list, ssm, proj, x2, g1, sc2, sh2, g2, norm_w, w_ba, w_bs, w_o, w_rt, w_gs, w_us, w_ds, seq):
    t, d = x2.shape
    tm = MIX_TM
    per_b = seq // tm
    row = lambda i: (i, 0)
    modrow = lambda i: (i // per_b, 0, 0)
    const = lambda i: (0, 0)
    full = lambda a: pl.BlockSpec(a.shape, const)
    mod = pl.BlockSpec((None, 1, d), modrow)
    return pl.pallas_call(
        _mix_kernel,
        grid=(t // tm,),
        in_specs=[pl.BlockSpec((HEADS_PER_GROUP, tm, HEAD_DIM), lambda i: (0, i, 0))] * 3
        + [pl.BlockSpec((tm, LANE), row)] * 3 + [
            pl.BlockSpec((tm, d), row),
            pl.BlockSpec((tm, d), lambda i: (i, COL_GA // d)),
            pl.BlockSpec((tm, d), lambda i: (i, COL_GS // d)),
            pl.BlockSpec((tm, d), row),
            mod, mod, mod, mod, full(norm_w), full(w_ba), full(w_bs), full(w_o), full(w_rt),
            full(w_gs), full(w_us), full(w_ds)],
        out_specs=[pl.BlockSpec((tm, d), row), pl.BlockSpec((tm, HALF_D), row),
                   pl.BlockSpec((N_EXPERTS, tm), lambda i: (0, i))],
        out_shape=[jax.ShapeDtypeStruct((t, d), F32), jax.ShapeDtypeStruct((t, HALF_D), U32),
                   jax.ShapeDtypeStruct((N_EXPERTS, t), F32)],
        compiler_params=_cp(("arbitrary",)),
        name="mix",
    )(*o_list, *st_list, ssm, proj, proj, x2, g1, sc2, sh2, g2, norm_w, w_ba, w_bs, w_o, w_rt, w_gs, w_us, w_ds)


def _route_kernel(sct_ref, bias_ref, idx_ref, w_ref, mem_ref, cnt_ref, run):
    i = pl.program_id(0)
    tm = sct_ref.shape[1]
    per_g = N_EXPERTS // N_EXPERT_GROUPS

    @pl.when(i == 0)
    def _():
        run[...] = jnp.zeros(run.shape, F32)

    s = sct_ref[...]
    biased = s + bias_ref[...]
    io_g = lax.broadcasted_iota(I32, (per_g, tm), 0).astype(F32)
    gscore = []
    for g in range(N_EXPERT_GROUPS):
        bgp = biased[g * per_g:(g + 1) * per_g, :]
        m1 = jnp.max(bgp, axis=0, keepdims=True)
        first = jnp.min(jnp.where(bgp == m1, io_g, float(per_g)), axis=0, keepdims=True)
        m2 = jnp.max(jnp.where(io_g == first, NEG, bgp), axis=0, keepdims=True)
        gscore.append(m1 + m2)
    gs = jnp.concatenate(gscore, axis=0)
    io8 = lax.broadcasted_iota(I32, (N_EXPERT_GROUPS, tm), 0).astype(F32)
    gsel = jnp.zeros((N_EXPERT_GROUPS, tm), F32)
    cur = gs
    for _ in range(TOPK_GROUPS):
        mx = jnp.max(cur, axis=0, keepdims=True)
        fi = jnp.min(jnp.where(cur == mx, io8, float(N_EXPERT_GROUPS)), axis=0, keepdims=True)
        hit = io8 == fi
        gsel = jnp.where(hit, 1.0, gsel)
        cur = jnp.where(hit, NEG, cur)
    masked = jnp.concatenate(
        [jnp.where(gsel[g:g + 1, :] > 0.5, biased[g * per_g:(g + 1) * per_g, :], NEG) for g in range(N_EXPERT_GROUPS)],
        axis=0)
    io_e = lax.broadcasted_iota(I32, (N_EXPERTS, tm), 0).astype(F32)
    member = jnp.zeros((N_EXPERTS, tm), F32)
    idxs, ws = [], []
    for _ in range(TOP_K):
        mx = jnp.max(masked, axis=0, keepdims=True)
        fi = jnp.min(jnp.where(masked == mx, io_e, float(N_EXPERTS)), axis=0, keepdims=True)
        hit = io_e == fi
        idxs.append(fi)
        ws.append(jnp.sum(jnp.where(hit, s, 0.0), axis=0, keepdims=True))
        member = jnp.where(hit, 1.0, member)
        masked = jnp.where(hit, NEG, masked)
    wsum = ws[0]
    for k in range(1, TOP_K):
        wsum = wsum + ws[k]
    idx_ref[...] = jnp.concatenate(idxs, axis=0).astype(I32)
    w_ref[...] = jnp.concatenate([w / wsum * ROUTED_SCALE for w in ws], axis=0)
    mem_ref[...] = member.astype(BF16)
    new_run = run[...] + jnp.sum(member, axis=1, keepdims=True)
    run[...] = new_run
    cnt_ref[...] = new_run


def _route(scores_t, router_bias):
    e, t = scores_t.shape
    tm = ROUTE_TM
    tok = pl.BlockSpec((TOP_K, tm), lambda i: (0, i))
    return pl.pallas_call(
        _route_kernel,
        grid=(t // tm,),
        in_specs=[pl.BlockSpec((e, tm), lambda i: (0, i)), pl.BlockSpec((e, 1), lambda i: (0, 0))],
        out_specs=[tok, tok, pl.BlockSpec((e, tm), lambda i: (0, i)), pl.BlockSpec((e, LANE), lambda i: (0, 0))],
        out_shape=[jax.ShapeDtypeStruct((TOP_K, t), I32), jax.ShapeDtypeStruct((TOP_K, t), F32),
                   jax.ShapeDtypeStruct((e, t), BF16), jax.ShapeDtypeStruct((e, LANE), F32)],
        scratch_shapes=[pltpu.VMEM((e, LANE), F32)],
        compiler_params=_cp(("arbitrary",)),
        name="route",
    )(scores_t, router_bias.reshape(e, 1).astype(F32))


def _dest_kernel(mem_ref, idx_ref, start_ref, dest_ref, run):
    i = pl.program_id(0)
    e, tm = mem_ref.shape

    @pl.when(i == 0)
    def _():
        run[...] = jnp.broadcast_to(start_ref[...], run.shape)

    member = mem_ref[...]
    tr = lax.broadcasted_iota(I32, (tm, tm), 0)
    tc = lax.broadcasted_iota(I32, (tm, tm), 1)
    upper = jnp.where(tr < tc, 1.0, 0.0).astype(BF16)
    rank_full = jnp.dot(member, upper, preferred_element_type=F32) + run[:, 0:1]
    io_e = lax.broadcasted_iota(I32, (e, tm), 0)
    idx = idx_ref[...]
    rows = [jnp.sum(jnp.where(io_e == idx[k:k + 1, :], rank_full, 0.0), axis=0, keepdims=True) for k in range(TOP_K)]
    dest_ref[...] = jnp.concatenate(rows, axis=0).astype(I32)
    run[...] = run[...] + jnp.sum(member.astype(F32), axis=1, keepdims=True)


def _dest(member, idx, pad_start):
    e, t = member.shape
    tm = ROUTE_TM
    return pl.pallas_call(
        _dest_kernel,
        grid=(t // tm,),
        in_specs=[pl.BlockSpec((e, tm), lambda i: (0, i)), pl.BlockSpec((TOP_K, tm), lambda i: (0, i)),
                  pl.BlockSpec((e, 1), lambda i: (0, 0))],
        out_specs=pl.BlockSpec((TOP_K, tm), lambda i: (0, i)),
        out_shape=jax.ShapeDtypeStruct((TOP_K, t), I32),
        scratch_shapes=[pltpu.VMEM((e, LANE), F32)],
        compiler_params=_cp(("arbitrary",)),
        name="dest",
    )(member, idx, pad_start.astype(F32).reshape(e, 1))


def _dispatch_kernel(pend_ref, npad_ref, dest_ref, u_ref, xs_ref, zero_scr, sem, zsem):
    i = pl.program_id(0)
    tm = u_ref.shape[0]
    rb = zero_scr.shape[0]

    @pl.when(i == 0)
    def _():
        zero_scr[...] = jnp.zeros(zero_scr.shape, zero_scr.dtype)

        def zcopy(e):
            return pltpu.make_async_copy(zero_scr, xs_ref.at[pl.ds(pl.multiple_of(pend_ref[e] - rb, rb), rb)], zsem)

        def zstart(e, c):
            @pl.when(npad_ref[e] > 0)
            def _():
                zcopy(e).start()
            return c

        def zwait(e, c):
            @pl.when(npad_ref[e] > 0)
            def _():
                zcopy(e).wait()
            return c

        lax.fori_loop(0, N_EXPERTS, zstart, 0)
        lax.fori_loop(0, N_EXPERTS, zwait, 0)

    def row_copy(t, k):
        return pltpu.make_async_copy(u_ref.at[pl.ds(t, 1)], xs_ref.at[pl.ds(dest_ref[k, t], 1)], sem)

    def start(t, c):
        for k in range(TOP_K):
            row_copy(t, k).start(priority=k % 2)
        return c

    def wait(t, c):
        for k in range(TOP_K):
            row_copy(t, k).wait()
        return c

    lax.fori_loop(0, tm, start, 0)
    lax.fori_loop(0, tm, wait, 0)


def _dispatch(u2p, dest, pad_end, padded, n_rows):
    t, w = u2p.shape
    tm = DISPATCH_TM
    gs = pltpu.PrefetchScalarGridSpec(
        num_scalar_prefetch=2,
        grid=(t // tm,),
        in_specs=[pl.BlockSpec((TOP_K, tm), lambda i, pe, npd: (0, i), memory_space=pltpu.SMEM),
                  pl.BlockSpec((tm, w), lambda i, pe, npd: (i, 0))],
        out_specs=pl.BlockSpec(memory_space=pl.ANY),
        scratch_shapes=[pltpu.VMEM((MOE_ROWS, w), U32), pltpu.SemaphoreType.DMA, pltpu.SemaphoreType.DMA],
    )
    return pl.pallas_call(
        _dispatch_kernel,
        grid_spec=gs,
        out_shape=jax.ShapeDtypeStruct((n_rows, w), U32),
        compiler_params=_cp(("arbitrary",)),
        name="dispatch",
    )(pad_end, padded, dest, u2p)


def _expert_kernel(be_ref, nu_ref, xs_ref, wg_ref, wu_ref, wd_ref, ys_ref, wg_b, wu_b, wd_b):
    i = pl.program_id(0)
    changed = (i == 0) | (be_ref[i] != be_ref[jnp.maximum(i - 1, 0)])

    @pl.when(changed)
    def _():
        wg_b[...] = wg_ref[...].astype(BF16)
        wu_b[...] = wu_ref[...].astype(BF16)
        wd_b[...] = wd_ref[...].astype(BF16)

    @pl.when(i < nu_ref[0])
    def _():
        lo, hi = _unpack_pair(xs_ref[...])
        lo = lo.astype(BF16)
        hi = hi.astype(BF16)
        g = (jnp.dot(lo, wg_b[:HALF_D, :], preferred_element_type=F32)
             + jnp.dot(hi, wg_b[HALF_D:, :], preferred_element_type=F32))
        u = (jnp.dot(lo, wu_b[:HALF_D, :], preferred_element_type=F32)
             + jnp.dot(hi, wu_b[HALF_D:, :], preferred_element_type=F32))
        h = (_silu(g) * u).astype(BF16)
        y = jnp.dot(h, wd_b[...], preferred_element_type=F32)
        ys_ref[...] = _pack_pair(y[:, :HALF_D], y[:, HALF_D:])

    @pl.when(i >= nu_ref[0])
    def _():
        ys_ref[...] = jnp.zeros(ys_ref.shape, ys_ref.dtype)


def _experts(xs, blk_e, n_used, w_gate_e, w_up_e, w_down_e):
    n_rows, w = xs.shape
    _, d, ff = w_gate_e.shape
    rb = MOE_ROWS
    gs = pltpu.PrefetchScalarGridSpec(
        num_scalar_prefetch=2,
        grid=(n_rows // rb,),
        in_specs=[pl.BlockSpec((rb, w), lambda i, be, nu: (jnp.minimum(i, nu[0] - 1), 0)),
                  pl.BlockSpec((None, d, ff), lambda i, be, nu: (be[i], 0, 0)),
                  pl.BlockSpec((None, d, ff), lambda i, be, nu: (be[i], 0, 0)),
                  pl.BlockSpec((None, ff, d), lambda i, be, nu: (be[i], 0, 0))],
        out_specs=pl.BlockSpec((rb, w), lambda i, be, nu: (i, 0)),
        scratch_shapes=[pltpu.VMEM((d, ff), BF16), pltpu.VMEM((d, ff), BF16), pltpu.VMEM((ff, d), BF16)],
    )
    return pl.pallas_call(
        _expert_kernel,
        grid_spec=gs,
        out_shape=jax.ShapeDtypeStruct((n_rows, w), U32),
        compiler_params=_cp(("arbitrary",)),
        name="experts",
    )(blk_e, n_used, xs, w_gate_e, w_up_e, w_down_e)


def _combine_kernel(dest_ref, wt_ref, base_ref, g2_ref, nfw_ref, ys_ref, out_ref, buf, sem):
    tm = base_ref.shape[0]

    def row_copy(t, k):
        return pltpu.make_async_copy(ys_ref.at[pl.ds(dest_ref[k, t], 1)], buf.at[k, pl.ds(t, 1)], sem)

    def start(t, c):
        for k in range(TOP_K):
            row_copy(t, k).start(priority=k % 2)
        return c

    def wait(t, c):
        for k in range(TOP_K):
            row_copy(t, k).wait()
        return c

    lax.fori_loop(0, tm, start, 0)
    lax.fori_loop(0, tm, wait, 0)
    r_lo, r_hi = _unpack_pair(buf[0])
    wk = wt_ref[:, 0:1]
    r_lo = r_lo * wk
    r_hi = r_hi * wk
    for k in range(1, TOP_K):
        lo, hi = _unpack_pair(buf[k])
        wk = wt_ref[:, k:k + 1]
        r_lo = r_lo + lo * wk
        r_hi = r_hi + hi * wk
    h_lo = base_ref[:, :HALF_D] + g2_ref[:, :HALF_D] * r_lo
    h_hi = base_ref[:, HALF_D:] + g2_ref[:, HALF_D:] * r_hi
    ssq = jnp.sum(h_lo * h_lo, axis=-1, keepdims=True) + jnp.sum(h_hi * h_hi, axis=-1, keepdims=True)
    inv = lax.rsqrt(ssq / (2 * HALF_D) + NORM_EPS)
    out_ref[:, :HALF_D] = h_lo * inv * nfw_ref[:, :HALF_D]
    out_ref[:, HALF_D:] = h_hi * inv * nfw_ref[:, HALF_D:]


def _combine(ys, dest, w_tok, base, g2, norm_final_w, seq):
    t, d = base.shape
    tm = COMBINE_TM
    per_b = seq // tm
    gs = pltpu.PrefetchScalarGridSpec(
        num_scalar_prefetch=0,
        grid=(t // tm,),
        in_specs=[pl.BlockSpec((TOP_K, tm), lambda i: (0, i), memory_space=pltpu.SMEM),
                  pl.BlockSpec((tm, TOP_K), lambda i: (i, 0)),
                  pl.BlockSpec((tm, d), lambda i: (i, 0)),
                  pl.BlockSpec((None, 1, d), lambda i: (i // per_b, 0, 0)),
                  pl.BlockSpec((1, d), lambda i: (0, 0)),
                  pl.BlockSpec(memory_space=pl.ANY)],
        out_specs=pl.BlockSpec((tm, d), lambda i: (i, 0)),
        scratch_shapes=[pltpu.VMEM((TOP_K, tm, HALF_D), U32), pltpu.SemaphoreType.DMA],
    )
    return pl.pallas_call(
        _combine_kernel,
        grid_spec=gs,
        out_shape=jax.ShapeDtypeStruct((t, d), F32),
        compiler_params=_cp(("arbitrary",)),
        name="combine",
    )(dest, w_tok, base, g2, norm_final_w, ys)


def kernel(x, c, positions, w_mod, b_mod, norm_mix_w, norm_ffn_w, w_in, conv_w, conv_b, dt_bias, a_log, d_skip,
           ssm_norm_w, w_branch_attn, w_branch_ssm, w_out, w_router, router_bias, w_gate_e, w_up_e, w_down_e,
           w_gate_s, w_up_s, w_down_s, norm_final_w):
    batch, seq, d = x.shape
    t = batch * seq
    assert w_mod.shape[0] == 1, "one layer"
    assert d == 2 * HALF_D and seq % INPROJ_TM == 0 and INPROJ_TM % (ATTN_DILATIONS[-1] * 16) == 0

    mod = _modulation(c, w_mod[0], b_mod[0])
    sh1, sc1, g1, sh2, sc2, g2 = [m.reshape(batch, 1, d) for m in jnp.split(mod, 6, axis=-1)]
    rope_c, rope_s1, rope_s2 = _rope_tables(positions)

    wi = w_in[0]
    q_dim = 3 * GROUP_W
    o_z = 3 * q_dim
    o_xbc = o_z + d
    o_dt = o_xbc + conv_w.shape[2]
    o_g = o_dt + SSM_HEADS
    qkv = lambda g: [wi[:, s * q_dim + g * GROUP_W:s * q_dim + (g + 1) * GROUP_W] for s in range(3)]
    w_packed = jnp.concatenate([wi[:, o_xbc:o_dt], wi[:, o_g:], wi[:, o_z:o_xbc]] + qkv(0) + qkv(1) + qkv(2),
                               axis=1).astype(BF16)
    assert w_packed.shape[1] == MAIN_W + 2 * QKV_W
    w_dt = jnp.pad(wi[:, o_dt:o_g], ((0, 0), (0, LANE - SSM_HEADS))).astype(BF16)

    x2 = x.reshape(t, d)
    proj, qkv1, qkv2, dt_raw = _inproj(x2, sc1, sh1, norm_mix_w.reshape(1, d), w_packed, w_dt,
                                       rope_c, rope_s1, rope_s2, batch, seq)

    srcs = [(proj.reshape(batch, 1, seq, MAIN_W), COL_Q0 // GROUP_W), (qkv1, 0), (qkv2, 0)]
    o_list, st_list = [], []
    for g, (src, col0) in enumerate(srcs):
        o, st = _attention_group(src, g, batch, seq, col0)
        o_list.append(o)
        st_list.append(st)
    ssm = _ssd(proj, dt_raw, conv_w[0], conv_b[0], dt_bias[0], a_log[0], d_skip[0], ssm_norm_w[0], batch, seq)

    base, u2p, scores_t = _mix(
        o_list, st_list, ssm, proj, x2, g1, sc2, sh2, g2, norm_ffn_w.reshape(1, d),
        w_branch_attn[0].astype(BF16), w_branch_ssm[0].astype(BF16), w_out[0].astype(BF16),
        w_router[0].T, w_gate_s[0].astype(BF16), w_up_s[0].astype(BF16), w_down_s[0].astype(BF16), seq)

    idx, w_sel, member, counts = _route(scores_t, router_bias[0])

    rb = MOE_ROWS
    cnt = counts[:, 0].astype(I32)
    padded = (cnt + rb - 1) // rb * rb
    pad_end = jnp.cumsum(padded).astype(I32)
    pad_start = pad_end - padded
    n_rows = t * TOP_K + N_EXPERTS * rb
    n_blk = n_rows // rb
    blk_first = jnp.arange(n_blk, dtype=I32) * rb
    blk_e = jnp.minimum(jnp.sum((pad_end[None, :] <= blk_first[:, None]).astype(I32), axis=1), N_EXPERTS - 1)
    n_used = (pad_end[-1:] // rb).astype(I32)

    dest = _dest(member, idx, pad_start)
    xs = _dispatch(u2p, dest, pad_end, padded, n_rows)
    ys = _experts(xs, blk_e, n_used, w_gate_e[0], w_up_e[0], w_down_e[0])
    out = _combine(ys, dest, w_sel.T, base, g2, norm_final_w.reshape(1, d), seq)
    return out.reshape(batch, seq, d)
```

```python
import functools
import math

import jax
import jax.numpy as jnp
from jax import lax
from jax.experimental import pallas as pl
from jax.experimental.pallas import tpu as pltpu

F32 = jnp.float32
BF16 = jnp.bfloat16
I32 = jnp.int32
U32 = jnp.uint32

LANE = 128
SUBLANE = 8
VMEM_LIMIT = 56 * 1024 * 1024

HEAD_DIM = 128
HEADS_PER_GROUP = 4
GROUP_W = HEADS_PER_GROUP * HEAD_DIM
ATTN_DILATIONS = (1, 4, 16)
ATTN_BLK = 128
ROPE_DIM = 32
ROPE_HALF = 16
ROPE_THETA = 500000.0
SSM_HEADS = 16
SSM_HEADDIM = 64
SSM_GROUPS = 4
SSM_STATE = 128
SSM_CONV = 4
SSM_CHUNK = 128
N_EXPERTS = 256
TOP_K = 8
N_EXPERT_GROUPS = 8
TOPK_GROUPS = 4
ROUTED_SCALE = 2.5
NORM_EPS = 1e-6
NEG = -1e30

COL_XBC, COL_GA, COL_GS, COL_Z, COL_Q0 = 0, 2048, 3072, 4096, 5120
MAIN_W = 6656
QKV_W = 3 * GROUP_W

INPROJ_TM, INPROJ_TN = 256, 512
MIX_TM = 256
ROUTE_TM = 512
MOE_ROWS = 256
DISPATCH_TM = 256
COMBINE_TM = 256
HALF_D = 512


def _cp(sem, vmem=VMEM_LIMIT):
    return pltpu.CompilerParams(dimension_semantics=sem, vmem_limit_bytes=vmem)


def _sigmoid(x):
    return 1.0 / (1.0 + jnp.exp(-x))


def _silu(x):
    return x * _sigmoid(x)


def _pack_pair(a, b):
    ua = lax.bitcast_convert_type(a.astype(BF16).astype(F32), U32)
    ub = lax.bitcast_convert_type(b.astype(BF16).astype(F32), U32)
    return (ua >> 16) | ub


def _unpack_pair(w):
    lo = lax.bitcast_convert_type(w << 16, F32)
    hi = lax.bitcast_convert_type(w & jnp.uint32(0xFFFF0000), F32)
    return lo, hi


def _mod_kernel(c_ref, w_ref, b_ref, o_ref):
    cond = _silu(c_ref[...])
    o_ref[...] = jnp.dot(cond, w_ref[...], preferred_element_type=F32) + b_ref[...]


def _modulation(c, w_mod, b_mod):
    b, d = c.shape
    n = w_mod.shape[1]
    return pl.pallas_call(
        _mod_kernel,
        grid=(n // d,),
        in_specs=[pl.BlockSpec((b, d), lambda j: (0, 0)),
                  pl.BlockSpec((d, d), lambda j: (0, j)),
                  pl.BlockSpec((1, d), lambda j: (0, j))],
        out_specs=pl.BlockSpec((b, d), lambda j: (0, j)),
        out_shape=jax.ShapeDtypeStruct((b, n), F32),
        compiler_params=_cp(("arbitrary",)),
        name="modulation",
    )(c, w_mod, b_mod.reshape(1, n))


def _rope_kernel(pos_ref, inv_ref, c_ref, s1_ref, s2_ref):
    ang = pos_ref[...].astype(F32) * inv_ref[...]
    lane = lax.broadcasted_iota(I32, ang.shape, 1)
    cos = jnp.cos(ang)
    sin = jnp.sin(ang)
    c_ref[...] = jnp.where(lane < ROPE_DIM, cos, 1.0)
    s1_ref[...] = jnp.where(lane < ROPE_HALF, -sin, 0.0)
    s2_ref[...] = jnp.where((lane >= ROPE_HALF) & (lane < ROPE_DIM), sin, 0.0)


def _rope_tables(positions):
    t = positions.size
    tm = 2048
    inv_freq = ROPE_THETA ** (-jnp.arange(ROPE_HALF, dtype=F32) / ROPE_HALF)
    inv_row = jnp.concatenate([inv_freq, inv_freq, jnp.zeros((LANE - ROPE_DIM,), F32)]).reshape(1, LANE)
    spec = pl.BlockSpec((tm, LANE), lambda i: (i, 0))
    shp = jax.ShapeDtypeStruct((t, LANE), F32)
    return pl.pallas_call(
        _rope_kernel,
        grid=(t // tm,),
        in_specs=[pl.BlockSpec((tm, 1), lambda i: (i, 0)), pl.BlockSpec((1, LANE), lambda i: (0, 0))],
        out_specs=[spec, spec, spec],
        out_shape=[shp, shp, shp],
        compiler_params=_cp(("arbitrary",)),
        name="rope_tables",
    )(positions.reshape(t, 1), inv_row)


def _inproj_kernel(x_ref, sc_ref, sh_ref, nw_ref, w_ref, wdt_ref, c_ref, s1_ref, s2_ref,
                   main_ref, g1_ref, g2_ref, dt_ref, u_scr, rope_scr, uc_scr, *, tn, q_scale):
    tm = x_ref.shape[0]
    n_main = MAIN_W // tn
    n_qkv = QKV_W // tn

    x = x_ref[...]
    ms = jnp.mean(x * x, axis=-1, keepdims=True)
    y = x * lax.rsqrt(ms + NORM_EPS) * nw_ref[...]
    uf = y * (1.0 + sc_ref[...]) + sh_ref[...]
    u = uf.astype(BF16)
    dt_ref[...] = jnp.dot(u, wdt_ref[...], preferred_element_type=F32)
    n_chunk = uf.shape[1] // LANE
    for cc in range(n_chunk):
        uc_scr[cc] = uf[:, cc * LANE:(cc + 1) * LANE]
    for o, d in enumerate(ATTN_DILATIONS[1:]):
        rows = tm // d
        for r in range(d):
            for cc in range(n_chunk):
                u_scr[o, r * rows:(r + 1) * rows, cc * LANE:(cc + 1) * LANE] = (
                    uc_scr[cc, pl.ds(r, rows, stride=d), :].astype(BF16))
            for ti, tab in enumerate((c_ref, s1_ref, s2_ref)):
                rope_scr[o, ti, r * rows:(r + 1) * rows, :] = tab[pl.ds(r, rows, stride=d), :]

    for c in range(n_main + 2 * n_qkv):
        order = 0 if c < n_main else (1 if c < n_main + n_qkv else 2)
        jq = c - COL_Q0 // tn if order == 0 else (c - n_main - (order - 1) * n_qkv)
        lhs = u if order == 0 else u_scr[order - 1]
        acc = jnp.dot(lhs, w_ref[:, c * tn:(c + 1) * tn], preferred_element_type=F32)
        if jq in (0, 1):
            scale = q_scale if jq == 0 else 1.0
            tabs = (c_ref, s1_ref, s2_ref) if order == 0 else tuple(rope_scr.at[order - 1, ti] for ti in range(3))
            cs = tabs[0][...] * scale
            s1 = tabs[1][...] * scale
            s2 = tabs[2][...] * scale
            parts = []
            for h in range(tn // HEAD_DIM):
                a = acc[:, h * HEAD_DIM:(h + 1) * HEAD_DIM]
                parts.append(a * cs + pltpu.roll(a, LANE - ROPE_HALF, 1) * s1 + pltpu.roll(a, ROPE_HALF, 1) * s2)
            acc = jnp.concatenate(parts, axis=1)
        val = acc.astype(BF16)
        if order == 0:
            main_ref[:, c * tn:(c + 1) * tn] = val
        else:
            dst = g1_ref if order == 1 else g2_ref
            c0 = (c - n_main - (order - 1) * n_qkv) * tn
            dst[:, :, c0:c0 + tn] = val.reshape(dst.shape[0], dst.shape[1], tn)


def _inproj(x2, sc1, sh1, norm_w, w_packed, w_dt, rope_c, rope_s1, rope_s2, batch, seq):
    t, d = x2.shape
    tm, tn = INPROJ_TM, INPROJ_TN
    n = w_packed.shape[1]
    per_b = seq // tm
    d1, d2 = ATTN_DILATIONS[1], ATTN_DILATIONS[2]
    row = lambda i: (i, 0)
    modrow = lambda i: (i // per_b, 0, 0)
    const = lambda i: (0, 0)
    resident = pl.Buffered(1)
    qkv_map = lambda i: (i // per_b, 0, i % per_b, 0)
    return pl.pallas_call(
        functools.partial(_inproj_kernel, tn=tn, q_scale=1.0 / math.sqrt(HEAD_DIM)),
        grid=(t // tm,),
        in_specs=[pl.BlockSpec((tm, d), row),
                  pl.BlockSpec((None, 1, d), modrow),
                  pl.BlockSpec((None, 1, d), modrow),
                  pl.BlockSpec((1, d), const),
                  pl.BlockSpec((d, n), const, pipeline_mode=resident),
                  pl.BlockSpec((d, LANE), const),
                  pl.BlockSpec((tm, LANE), row),
                  pl.BlockSpec((tm, LANE), row),
                  pl.BlockSpec((tm, LANE), row)],
        out_specs=[pl.BlockSpec((tm, MAIN_W), row),
                   pl.BlockSpec((None, d1, tm // d1, QKV_W), qkv_map),
                   pl.BlockSpec((None, d2, tm // d2, QKV_W), qkv_map),
                   pl.BlockSpec((tm, LANE), row)],
        out_shape=[jax.ShapeDtypeStruct((t, MAIN_W), BF16),
                   jax.ShapeDtypeStruct((batch, d1, seq // d1, QKV_W), BF16),
                   jax.ShapeDtypeStruct((batch, d2, seq // d2, QKV_W), BF16),
                   jax.ShapeDtypeStruct((t, LANE), F32)],
        scratch_shapes=[pltpu.VMEM((2, tm, d), BF16), pltpu.VMEM((2, 3, tm, LANE), F32),
                        pltpu.VMEM((d // LANE, tm, LANE), F32)],
        compiler_params=_cp(("arbitrary",)),
        name="inproj",
    )(x2, sc1, sh1, norm_w, w_packed, w_dt, rope_c, rope_s1, rope_s2)


def _attn_kernel(q_ref, k_ref, v_ref, o_ref, st_ref, *, d, nb):
    blk = ATTN_BLK
    qi = lax.broadcasted_iota(I32, (blk, 2 * blk), 0)
    kj = lax.broadcasted_iota(I32, (blk, 2 * blk), 1)
    band = (kj >= qi) & (kj <= qi + blk)
    qi1 = lax.broadcasted_iota(I32, (blk, blk), 0)
    kj1 = lax.broadcasted_iota(I32, (blk, blk), 1)
    causal = kj1 <= qi1
    lane = kj1

    def block(r, q0, k0, nk, mask):
        rows = pl.ds(q0, blk) if d == 1 else pl.ds(q0 * d + r, blk, stride=d)
        st = jnp.zeros((blk, LANE), F32)
        for h in range(HEADS_PER_GROUP):
            hs = slice(h * HEAD_DIM, (h + 1) * HEAD_DIM)
            q = q_ref[r, pl.ds(q0, blk), hs]
            k = k_ref[r, pl.ds(k0, nk), hs]
            v = v_ref[r, pl.ds(k0, nk), hs]
            s = lax.dot_general(q, k, (((1,), (1,)), ((), ())), preferred_element_type=F32)
            s = jnp.where(mask, s, NEG)
            m = jnp.max(s, axis=-1, keepdims=True)
            p = jnp.exp(s - m)
            l = jnp.sum(p, axis=-1, keepdims=True)
            o = jnp.dot(p.astype(BF16), v, preferred_element_type=F32)
            o_ref[h, rows, :] = o / l
            st = jnp.where(lane == h, m, st)
            st = jnp.where(lane == HEADS_PER_GROUP + h, l, st)
        st_ref[rows, :] = st

    def per_residue(r, carry):
        block(r, 0, 0, blk, causal)
        if nb > 1:
            def body(n, c):
                q0 = pl.multiple_of(n * blk, blk)
                block(r, q0, pl.multiple_of(q0 - blk, blk), 2 * blk, band)
                return c
            lax.fori_loop(1, nb, body, 0)
        return carry

    if d == 1:
        per_residue(0, 0)
    else:
        lax.fori_loop(0, d, per_residue, 0)


def _attention_group(src, g, batch, seq, col0):
    d = ATTN_DILATIONS[g]
    n_sub = seq // d
    nb = n_sub // ATTN_BLK
    spec = lambda c: pl.BlockSpec((None, d, n_sub, GROUP_W), lambda b: (b, 0, 0, c))
    o, st = pl.pallas_call(
        functools.partial(_attn_kernel, d=d, nb=nb),
        grid=(batch,),
        in_specs=[spec(col0), spec(col0 + 1), spec(col0 + 2)],
        out_specs=[pl.BlockSpec((HEADS_PER_GROUP, seq, HEAD_DIM), lambda b: (0, b, 0)),
                   pl.BlockSpec((seq, LANE), lambda b: (b, 0))],
        out_shape=[jax.ShapeDtypeStruct((HEADS_PER_GROUP, batch * seq, HEAD_DIM), F32),
                   jax.ShapeDtypeStruct((batch * seq, LANE), F32)],
        compiler_params=_cp(("arbitrary",)),
        name=f"attn_d{d}",
    )(src, src, src)
    return o, st


def _ssd_kernel(xbc_ref, z_ref, dt_ref, cw_ref, cb_ref, dtb_ref, alog_ref, dsk_ref, nw_ref, expand_ref,
                out_ref, xpad, state):
    L = SSM_CHUNK
    inner = SSM_HEADS * SSM_HEADDIM
    gw = SSM_STATE
    c = pl.program_id(1)

    @pl.when(c == 0)
    def _():
        xpad[0:SUBLANE, :] = jnp.zeros((SUBLANE, xpad.shape[1]), F32)
        state[...] = jnp.zeros(state.shape, F32)

    xpad[SUBLANE:SUBLANE + L, :] = xbc_ref[...].astype(F32)
    conv = cb_ref[...] + cw_ref[0:1, :] * xpad[SUBLANE - 3:SUBLANE - 3 + L, :]
    for k in range(1, SSM_CONV):
        conv = conv + cw_ref[k:k + 1, :] * xpad[SUBLANE - 3 + k:SUBLANE - 3 + k + L, :]
    xpad[0:SUBLANE, :] = xpad[L:L + SUBLANE, :]
    act = _silu(conv)
    xs = act[:, :inner]

    lane = lax.broadcasted_iota(I32, (L, LANE), 1)
    row = lax.broadcasted_iota(I32, (L, LANE), 0)
    dtr = dt_ref[...] + dtb_ref[...]
    dt = jnp.maximum(dtr, 0.0) + jnp.log(1.0 + jnp.exp(-jnp.abs(dtr)))
    a_neg = jnp.where(lane < SSM_HEADS, -jnp.exp(alog_ref[...]), 0.0)
    a = dt * a_neg
    cs = a
    shift = 1
    while shift < L:
        cs = cs + jnp.where(row >= shift, pltpu.roll(cs, shift, 0), 0.0)
        shift *= 2
    cs_t = cs.T
    tri = row >= lane
    dt_x = jnp.dot(dt, expand_ref[...], preferred_element_type=F32, precision=lax.Precision.HIGHEST)
    xp = (xs * dt_x).astype(BF16)
    half = lane < SSM_HEADDIM
    zero_b = jnp.zeros((L, LANE), BF16)

    for g in range(SSM_GROUPS):
        bg = act[:, inner + g * gw:inner + (g + 1) * gw]
        cg = act[:, inner + SSM_GROUPS * gw + g * gw:inner + SSM_GROUPS * gw + (g + 1) * gw]
        cg_b = cg.astype(BF16)
        cb = lax.dot_general(cg_b, bg.astype(BF16), (((1,), (1,)), ((), ())), preferred_element_type=F32)
        bg_t = bg.T
        for pair in range(2):
            h0 = g * 4 + pair * 2
            pidx = h0 // 2
            xpp = xp[:, pidx * LANE:(pidx + 1) * LANE]
            rhs = jnp.concatenate([jnp.where(half, xpp, zero_b), jnp.where(half, zero_b, xpp)], axis=0)
            dec, dst, eoff, cdec = [], [], [], []
            for h in (h0, h0 + 1):
                cs_col = cs[:, h:h + 1]
                cs_row = cs_t[h:h + 1, :]
                dec.append(cb * jnp.exp(jnp.where(tri, cs_col - cs_row, NEG)))
                cs_last = cs_row[:, L - 1:L]
                dst.append(bg_t * jnp.exp(cs_last - cs_row))
                eoff.append(jnp.exp(cs_col))
                cdec.append(jnp.exp(cs_last))
            y_diag = jnp.dot(jnp.concatenate(dec, axis=1).astype(BF16), rhs, preferred_element_type=F32)
            st_new = jnp.dot(jnp.concatenate(dst, axis=1).astype(BF16), rhs, preferred_element_type=F32)
            prev = state[pidx]
            y_off = jnp.dot(cg_b, prev.astype(BF16), preferred_element_type=F32)
            y_off = y_off * jnp.where(half, eoff[0], eoff[1])
            state[pidx] = prev * jnp.where(half, cdec[0], cdec[1]) + st_new
            y = y_diag + y_off + dsk_ref[:, pidx * LANE:(pidx + 1) * LANE] * xs[:, pidx * LANE:(pidx + 1) * LANE]
            out_pair = y * _silu(z_ref[:, pidx * LANE:(pidx + 1) * LANE].astype(F32))
            xpad[SUBLANE:SUBLANE + L, pidx * LANE:(pidx + 1) * LANE] = out_pair

    gsz = inner // SSM_GROUPS
    for g in range(SSM_GROUPS):
        yg = xpad[SUBLANE:SUBLANE + L, g * gsz:(g + 1) * gsz]
        ms = jnp.mean(yg * yg, axis=-1, keepdims=True)
        out_ref[:, g * gsz:(g + 1) * gsz] = (yg * lax.rsqrt(ms + NORM_EPS) * nw_ref[:, g * gsz:(g + 1) * gsz]).astype(BF16)


def _ssd(proj, dt_raw, conv_w, conv_b, dt_bias, a_log, d_skip, ssm_norm_w, batch, seq):
    t = batch * seq
    L = SSM_CHUNK
    nc = seq // L
    inner = SSM_HEADS * SSM_HEADDIM
    cdim = conv_w.shape[1]
    pad16 = lambda v: jnp.pad(v.astype(F32), (0, LANE - SSM_HEADS)).reshape(1, LANE)
    expand = (jnp.arange(LANE)[:, None] == (jnp.arange(inner)[None, :] // SSM_HEADDIM)).astype(F32)
    dsk = jnp.repeat(d_skip.astype(F32), SSM_HEADDIM).reshape(1, inner)
    rowc = lambda b, c: (b * nc + c, 0)
    const = lambda b, c: (0, 0)
    return pl.pallas_call(
        _ssd_kernel,
        grid=(batch, nc),
        in_specs=[pl.BlockSpec((L, cdim), lambda b, c: (b * nc + c, COL_XBC // cdim)),
                  pl.BlockSpec((L, inner), lambda b, c: (b * nc + c, COL_Z // inner)),
                  pl.BlockSpec((L, LANE), rowc),
                  pl.BlockSpec((SSM_CONV, cdim), const),
                  pl.BlockSpec((1, cdim), const),
                  pl.BlockSpec((1, LANE), const),
                  pl.BlockSpec((1, LANE), const),
                  pl.BlockSpec((1, inner), const),
                  pl.BlockSpec((1, inner), const),
                  pl.BlockSpec((LANE, inner), const)],
        out_specs=pl.BlockSpec((L, inner), rowc),
        out_shape=jax.ShapeDtypeStruct((t, inner), BF16),
        scratch_shapes=[pltpu.VMEM((L + 2 * SUBLANE, cdim), F32),
                        pltpu.VMEM((SSM_HEADS // 2, SSM_STATE, 2 * SSM_HEADDIM), F32)],
        compiler_params=_cp(("arbitrary", "arbitrary")),
        name="ssd",
    )(proj, proj, dt_raw, conv_w.astype(F32), conv_b.reshape(1, cdim).astype(F32), pad16(dt_bias), pad16(a_log),
      dsk, ssm_norm_w.reshape(1, inner).astype(F32), expand)


def _mix_kernel(o0_ref, o1_ref, o2_ref, s0_ref, s1_ref, s2_ref, ssm_ref, ga_ref, gs_ref, x_ref,
                g1_ref, sc2_ref, sh2_ref, g2_ref, nw_ref, wba_ref, wbs_ref, wo_ref, wrt_ref,
                wgs_ref, wus_ref, wds_ref, base_ref, u2p_ref, sct_ref):
    o_refs = (o0_ref, o1_ref, o2_ref)
    s_refs = (s0_ref, s1_ref, s2_ref)
    heads = []
    for h in range(HEADS_PER_GROUP):
        ms = [s[:, h:h + 1] for s in s_refs]
        ls = [s[:, HEADS_PER_GROUP + h:HEADS_PER_GROUP + h + 1] for s in s_refs]
        mx = jnp.maximum(jnp.maximum(ms[0], ms[1]), ms[2])
        wts = [l * jnp.exp(m - mx) for m, l in zip(ms, ls)]
        num = wts[0] * o_refs[0][h] + wts[1] * o_refs[1][h] + wts[2] * o_refs[2][h]
        heads.append((num / (wts[0] + wts[1] + wts[2])).astype(BF16))
    attn = jnp.concatenate(heads, axis=1)
    ya = jnp.dot(attn, wba_ref[...], preferred_element_type=F32)
    ys = jnp.dot(ssm_ref[...], wbs_ref[...], preferred_element_type=F32)
    merged = _sigmoid(ga_ref[...].astype(F32)) * ya + _sigmoid(gs_ref[...].astype(F32)) * ys
    mix = jnp.dot(merged.astype(BF16), wo_ref[...], preferred_element_type=F32)
    h1 = x_ref[...] + g1_ref[...] * mix
    ms2 = jnp.mean(h1 * h1, axis=-1, keepdims=True)
    u2 = h1 * lax.rsqrt(ms2 + NORM_EPS) * nw_ref[...] * (1.0 + sc2_ref[...]) + sh2_ref[...]
    u2p_ref[...] = _pack_pair(u2[:, :HALF_D], u2[:, HALF_D:])
    logits_t = lax.dot_general(wrt_ref[...], u2, (((1,), (1,)), ((), ())), preferred_element_type=F32,
                               precision=lax.Precision.HIGHEST)
    sct_ref[...] = _sigmoid(logits_t)
    u2b = u2.astype(BF16)
    hs_ = _silu(jnp.dot(u2b, wgs_ref[...], preferred_element_type=F32)) * jnp.dot(u2b, wus_ref[...], preferred_element_type=F32)
    shared = jnp.dot(hs_.astype(BF16), wds_ref[...], preferred_element_type=F32)
    base_ref[...] = h1 + g2_ref[...] * shared


def _mix(o_list, st_list, ssm, proj, x2, g1, sc2, sh2, g2, norm_w, w_ba, w_bs, w_o, w_rt, w_gs, w_us, w_ds, seq):
    t, d = x2.shape
    tm = MIX_TM
    per_b = seq // tm
    row = lambda i: (i, 0)
    modrow = lambda i: (i // per_b, 0, 0)
    const = lambda i: (0, 0)
    full = lambda a: pl.BlockSpec(a.shape, const)
    mod = pl.BlockSpec((None, 1, d), modrow)
    return pl.pallas_call(
        _mix_kernel,
        grid=(t // tm,),
        in_specs=[pl.BlockSpec((HEADS_PER_GROUP, tm, HEAD_DIM), lambda i: (0, i, 0))] * 3
        + [pl.BlockSpec((tm, LANE), row)] * 3 + [
            pl.BlockSpec((tm, d), row),
            pl.BlockSpec((tm, d), lambda i: (i, COL_GA // d)),
            pl.BlockSpec((tm, d), lambda i: (i, COL_GS // d)),
            pl.BlockSpec((tm, d), row),
            mod, mod, mod, mod, full(norm_w), full(w_ba), full(w_bs), full(w_o), full(w_rt),
            full(w_gs), full(w_us), full(w_ds)],
        out_specs=[pl.BlockSpec((tm, d), row), pl.BlockSpec((tm, HALF_D), row),
                   pl.BlockSpec((N_EXPERTS, tm), lambda i: (0, i))],
        out_shape=[jax.ShapeDtypeStruct((t, d), F32), jax.ShapeDtypeStruct((t, HALF_D), U32),
                   jax.ShapeDtypeStruct((N_EXPERTS, t), F32)],
        compiler_params=_cp(("arbitrary",)),
        name="mix",
    )(*o_list, *st_list, ssm, proj, proj, x2, g1, sc2, sh2, g2, norm_w, w_ba, w_bs, w_o, w_rt, w_gs, w_us, w_ds)


def _route_kernel(sct_ref, bias_ref, idx_ref, w_ref, mem_ref, cnt_ref, run):
    i = pl.program_id(0)
    tm = sct_ref.shape[1]
    per_g = N_EXPERTS // N_EXPERT_GROUPS

    @pl.when(i == 0)
    def _():
        run[...] = jnp.zeros(run.shape, F32)

    s = sct_ref[...]
    biased = s + bias_ref[...]
    io_g = lax.broadcasted_iota(I32, (per_g, tm), 0).astype(F32)
    gscore = []
    for g in range(N_EXPERT_GROUPS):
        bgp = biased[g * per_g:(g + 1) * per_g, :]
        m1 = jnp.max(bgp, axis=0, keepdims=True)
        first = jnp.min(jnp.where(bgp == m1, io_g, float(per_g)), axis=0, keepdims=True)
        m2 = jnp.max(jnp.where(io_g == first, NEG, bgp), axis=0, keepdims=True)
        gscore.append(m1 + m2)
    gs = jnp.concatenate(gscore, axis=0)
    io8 = lax.broadcasted_iota(I32, (N_EXPERT_GROUPS, tm), 0).astype(F32)
    gsel = jnp.zeros((N_EXPERT_GROUPS, tm), F32)
    cur = gs
    for _ in range(TOPK_GROUPS):
        mx = jnp.max(cur, axis=0, keepdims=True)
        fi = jnp.min(jnp.where(cur == mx, io8, float(N_EXPERT_GROUPS)), axis=0, keepdims=True)
        hit = io8 == fi
        gsel = jnp.where(hit, 1.0, gsel)
        cur = jnp.where(hit, NEG, cur)
    masked = jnp.concatenate(
        [jnp.where(gsel[g:g + 1, :] > 0.5, biased[g * per_g:(g + 1) * per_g, :], NEG) for g in range(N_EXPERT_GROUPS)],
        axis=0)
    io_e = lax.broadcasted_iota(I32, (N_EXPERTS, tm), 0).astype(F32)
    member = jnp.zeros((N_EXPERTS, tm), F32)
    idxs, ws = [], []
    for _ in range(TOP_K):
        mx = jnp.max(masked, axis=0, keepdims=True)
        fi = jnp.min(jnp.where(masked == mx, io_e, float(N_EXPERTS)), axis=0, keepdims=True)
        hit = io_e == fi
        idxs.append(fi)
        ws.append(jnp.sum(jnp.where(hit, s, 0.0), axis=0, keepdims=True))
        member = jnp.where(hit, 1.0, member)
        masked = jnp.where(hit, NEG, masked)
    wsum = ws[0]
    for k in range(1, TOP_K):
        wsum = wsum + ws[k]
    idx_ref[...] = jnp.concatenate(idxs, axis=0).astype(I32)
    w_ref[...] = jnp.concatenate([w / wsum * ROUTED_SCALE for w in ws], axis=0)
    mem_ref[...] = member.astype(BF16)
    new_run = run[...] + jnp.sum(member, axis=1, keepdims=True)
    run[...] = new_run
    cnt_ref[...] = new_run


def _route(scores_t, router_bias):
    e, t = scores_t.shape
    tm = ROUTE_TM
    tok = pl.BlockSpec((TOP_K, tm), lambda i: (0, i))
    return pl.pallas_call(
        _route_kernel,
        grid=(t // tm,),
        in_specs=[pl.BlockSpec((e, tm), lambda i: (0, i)), pl.BlockSpec((e, 1), lambda i: (0, 0))],
        out_specs=[tok, tok, pl.BlockSpec((e, tm), lambda i: (0, i)), pl.BlockSpec((e, LANE), lambda i: (0, 0))],
        out_shape=[jax.ShapeDtypeStruct((TOP_K, t), I32), jax.ShapeDtypeStruct((TOP_K, t), F32),
                   jax.ShapeDtypeStruct((e, t), BF16), jax.ShapeDtypeStruct((e, LANE), F32)],
        scratch_shapes=[pltpu.VMEM((e, LANE), F32)],
        compiler_params=_cp(("arbitrary",)),
        name="route",
    )(scores_t, router_bias.reshape(e, 1).astype(F32))


def _dest_kernel(mem_ref, idx_ref, start_ref, dest_ref, run):
    i = pl.program_id(0)
    e, tm = mem_ref.shape

    @pl.when(i == 0)
    def _():
        run[...] = jnp.broadcast_to(start_ref[...], run.shape)

    member = mem_ref[...]
    tr = lax.broadcasted_iota(I32, (tm, tm), 0)
    tc = lax.broadcasted_iota(I32, (tm, tm), 1)
    upper = jnp.where(tr < tc, 1.0, 0.0).astype(BF16)
    rank_full = jnp.dot(member, upper, preferred_element_type=F32) + run[:, 0:1]
    io_e = lax.broadcasted_iota(I32, (e, tm), 0)
    idx = idx_ref[...]
    rows = [jnp.sum(jnp.where(io_e == idx[k:k + 1, :], rank_full, 0.0), axis=0, keepdims=True) for k in range(TOP_K)]
    dest_ref[...] = jnp.concatenate(rows, axis=0).astype(I32)
    run[...] = run[...] + jnp.sum(member.astype(F32), axis=1, keepdims=True)


def _dest(member, idx, pad_start):
    e, t = member.shape
    tm = ROUTE_TM
    return pl.pallas_call(
        _dest_kernel,
        grid=(t // tm,),
        in_specs=[pl.BlockSpec((e, tm), lambda i: (0, i)), pl.BlockSpec((TOP_K, tm), lambda i: (0, i)),
                  pl.BlockSpec((e, 1), lambda i: (0, 0))],
        out_specs=pl.BlockSpec((TOP_K, tm), lambda i: (0, i)),
        out_shape=jax.ShapeDtypeStruct((TOP_K, t), I32),
        scratch_shapes=[pltpu.VMEM((e, LANE), F32)],
        compiler_params=_cp(("arbitrary",)),
        name="dest",
    )(member, idx, pad_start.astype(F32).reshape(e, 1))


def _dispatch_kernel(pend_ref, npad_ref, dest_ref, u_ref, xs_ref, zero_scr, sem, zsem):
    i = pl.program_id(0)
    tm = u_ref.shape[0]
    rb = zero_scr.shape[0]

    @pl.when(i == 0)
    def _():
        zero_scr[...] = jnp.zeros(zero_scr.shape, zero_scr.dtype)

        def zcopy(e):
            return pltpu.make_async_copy(zero_scr, xs_ref.at[pl.ds(pl.multiple_of(pend_ref[e] - rb, rb), rb)], zsem)

        def zstart(e, c):
            @pl.when(npad_ref[e] > 0)
            def _():
                zcopy(e).start()
            return c

        def zwait(e, c):
            @pl.when(npad_ref[e] > 0)
            def _():
                zcopy(e).wait()
            return c

        lax.fori_loop(0, N_EXPERTS, zstart, 0)
        lax.fori_loop(0, N_EXPERTS, zwait, 0)

    def row_copy(t, k):
        return pltpu.make_async_copy(u_ref.at[pl.ds(t, 1)], xs_ref.at[pl.ds(dest_ref[k, t], 1)], sem)

    def start(t, c):
        for k in range(TOP_K):
            row_copy(t, k).start(priority=k % 2)
        return c

    def wait(t, c):
        for k in range(TOP_K):
            row_copy(t, k).wait()
        return c

    lax.fori_loop(0, tm, start, 0)
    lax.fori_loop(0, tm, wait, 0)


def _dispatch(u2p, dest, pad_end, padded, n_rows):
    t, w = u2p.shape
    tm = DISPATCH_TM
    gs = pltpu.PrefetchScalarGridSpec(
        num_scalar_prefetch=2,
        grid=(t // tm,),
        in_specs=[pl.BlockSpec((TOP_K, tm), lambda i, pe, npd: (0, i), memory_space=pltpu.SMEM),
                  pl.BlockSpec((tm, w), lambda i, pe, npd: (i, 0))],
        out_specs=pl.BlockSpec(memory_space=pl.ANY),
        scratch_shapes=[pltpu.VMEM((MOE_ROWS, w), U32), pltpu.SemaphoreType.DMA, pltpu.SemaphoreType.DMA],
    )
    return pl.pallas_call(
        _dispatch_kernel,
        grid_spec=gs,
        out_shape=jax.ShapeDtypeStruct((n_rows, w), U32),
        compiler_params=_cp(("arbitrary",)),
        name="dispatch",
    )(pad_end, padded, dest, u2p)


def _expert_kernel(pstart_ref, nblk_ref, wg_ref, wu_ref, wd_ref, xs_ref, ys_ref,
                   wg_b, wu_b, wd_b, xbuf, ybuf, in_sem, out_sem):
    e = pl.program_id(0)
    rb = xbuf.shape[1]
    n = nblk_ref[e]
    base = pstart_ref[e]

    def in_copy(b, slot):
        return pltpu.make_async_copy(xs_ref.at[pl.ds(pl.multiple_of(base + b * rb, rb), rb)], xbuf.at[slot],
                                     in_sem.at[slot])

    def out_copy(b, slot):
        return pltpu.make_async_copy(ybuf.at[slot], ys_ref.at[pl.ds(pl.multiple_of(base + b * rb, rb), rb)],
                                     out_sem.at[slot])

    @pl.when(n > 0)
    def _():
        in_copy(0, 0).start()

    wg_b[...] = wg_ref[...].astype(BF16)
    wu_b[...] = wu_ref[...].astype(BF16)
    wd_b[...] = wd_ref[...].astype(BF16)

    def body(b, carry):
        slot = b % 2
        in_copy(b, slot).wait()

        @pl.when(b + 1 < n)
        def _():
            in_copy(b + 1, 1 - slot).start()

        @pl.when(b >= 2)
        def _():
            out_copy(b - 2, slot).wait()

        lo, hi = _unpack_pair(xbuf[slot])
        xb = jnp.concatenate([lo.astype(BF16), hi.astype(BF16)], axis=1)
        g = jnp.dot(xb, wg_b[...], preferred_element_type=F32)
        u = jnp.dot(xb, wu_b[...], preferred_element_type=F32)
        h = (_silu(g) * u).astype(BF16)
        y = jnp.dot(h, wd_b[...], preferred_element_type=F32)
        ybuf[slot] = _pack_pair(y[:, :HALF_D], y[:, HALF_D:])
        out_copy(b, slot).start()
        return carry

    lax.fori_loop(0, n, body, 0)

    @pl.when(n >= 2)
    def _():
        out_copy(n - 2, n % 2).wait()

    @pl.when(n >= 1)
    def _():
        out_copy(n - 1, (n - 1) % 2).wait()


def _experts(xs, pad_start, nblk, w_gate_e, w_up_e, w_down_e):
    n_rows, w = xs.shape
    ne, d, ff = w_gate_e.shape
    rb = MOE_ROWS
    gs = pltpu.PrefetchScalarGridSpec(
        num_scalar_prefetch=2,
        grid=(ne,),
        in_specs=[pl.BlockSpec((None, d, ff), lambda e, ps, nb: (e, 0, 0)),
                  pl.BlockSpec((None, d, ff), lambda e, ps, nb: (e, 0, 0)),
                  pl.BlockSpec((None, ff, d), lambda e, ps, nb: (e, 0, 0)),
                  pl.BlockSpec(memory_space=pl.ANY)],
        out_specs=pl.BlockSpec(memory_space=pl.ANY),
        scratch_shapes=[pltpu.VMEM((d, ff), BF16), pltpu.VMEM((d, ff), BF16), pltpu.VMEM((ff, d), BF16),
                        pltpu.VMEM((2, rb, w), U32), pltpu.VMEM((2, rb, w), U32),
                        pltpu.SemaphoreType.DMA((2,)), pltpu.SemaphoreType.DMA((2,))],
    )
    return pl.pallas_call(
        _expert_kernel,
        grid_spec=gs,
        out_shape=jax.ShapeDtypeStruct((n_rows, w), U32),
        compiler_params=_cp(("arbitrary",)),
        name="experts",
    )(pad_start, nblk, w_gate_e, w_up_e, w_down_e, xs)


def _combine_kernel(dest_ref, wt_ref, base_ref, g2_ref, nfw_ref, ys_ref, out_ref, buf, sem):
    tm = base_ref.shape[0]

    def row_copy(t, k):
        return pltpu.make_async_copy(ys_ref.at[pl.ds(dest_ref[k, t], 1)], buf.at[k, pl.ds(t, 1)], sem)

    def start(t, c):
        for k in range(TOP_K):
            row_copy(t, k).start(priority=k % 2)
        return c

    def wait(t, c):
        for k in range(TOP_K):
            row_copy(t, k).wait()
        return c

    lax.fori_loop(0, tm, start, 0)
    lax.fori_loop(0, tm, wait, 0)
    r_lo, r_hi = _unpack_pair(buf[0])
    wk = wt_ref[:, 0:1]
    r_lo = r_lo * wk
    r_hi = r_hi * wk
    for k in range(1, TOP_K):
        lo, hi = _unpack_pair(buf[k])
        wk = wt_ref[:, k:k + 1]
        r_lo = r_lo + lo * wk
        r_hi = r_hi + hi * wk
    h_lo = base_ref[:, :HALF_D] + g2_ref[:, :HALF_D] * r_lo
    h_hi = base_ref[:, HALF_D:] + g2_ref[:, HALF_D:] * r_hi
    ssq = jnp.sum(h_lo * h_lo, axis=-1, keepdims=True) + jnp.sum(h_hi * h_hi, axis=-1, keepdims=True)
    inv = lax.rsqrt(ssq / (2 * HALF_D) + NORM_EPS)
    out_ref[:, :HALF_D] = h_lo * inv * nfw_ref[:, :HALF_D]
    out_ref[:, HALF_D:] = h_hi * inv * nfw_ref[:, HALF_D:]


def _combine(ys, dest, w_tok, base, g2, norm_final_w, seq):
    t, d = base.shape
    tm = COMBINE_TM
    per_b = seq // tm
    gs = pltpu.PrefetchScalarGridSpec(
        num_scalar_prefetch=0,
        grid=(t // tm,),
        in_specs=[pl.BlockSpec((TOP_K, tm), lambda i: (0, i), memory_space=pltpu.SMEM),
                  pl.BlockSpec((tm, TOP_K), lambda i: (i, 0)),
                  pl.BlockSpec((tm, d), lambda i: (i, 0)),
                  pl.BlockSpec((None, 1, d), lambda i: (i // per_b, 0, 0)),
                  pl.BlockSpec((1, d), lambda i: (0, 0)),
                  pl.BlockSpec(memory_space=pl.ANY)],
        out_specs=pl.BlockSpec((tm, d), lambda i: (i, 0)),
        scratch_shapes=[pltpu.VMEM((TOP_K, tm, HALF_D), U32), pltpu.SemaphoreType.DMA],
    )
    return pl.pallas_call(
        _combine_kernel,
        grid_spec=gs,
        out_shape=jax.ShapeDtypeStruct((t, d), F32),
        compiler_params=_cp(("arbitrary",)),
        name="combine",
    )(dest, w_tok, base, g2, norm_final_w, ys)


def kernel(x, c, positions, w_mod, b_mod, norm_mix_w, norm_ffn_w, w_in, conv_w, conv_b, dt_bias, a_log, d_skip,
           ssm_norm_w, w_branch_attn, w_branch_ssm, w_out, w_router, router_bias, w_gate_e, w_up_e, w_down_e,
           w_gate_s, w_up_s, w_down_s, norm_final_w):
    batch, seq, d = x.shape
    t = batch * seq
    assert w_mod.shape[0] == 1, "one layer"
    assert d == 2 * HALF_D and seq % INPROJ_TM == 0 and INPROJ_TM % (ATTN_DILATIONS[-1] * 16) == 0

    mod = _modulation(c, w_mod[0], b_mod[0])
    sh1, sc1, g1, sh2, sc2, g2 = [m.reshape(batch, 1, d) for m in jnp.split(mod, 6, axis=-1)]
    rope_c, rope_s1, rope_s2 = _rope_tables(positions)

    wi = w_in[0]
    q_dim = 3 * GROUP_W
    o_z = 3 * q_dim
    o_xbc = o_z + d
    o_dt = o_xbc + conv_w.shape[2]
    o_g = o_dt + SSM_HEADS
    qkv = lambda g: [wi[:, s * q_dim + g * GROUP_W:s * q_dim + (g + 1) * GROUP_W] for s in range(3)]
    w_packed = jnp.concatenate([wi[:, o_xbc:o_dt], wi[:, o_g:], wi[:, o_z:o_xbc]] + qkv(0) + qkv(1) + qkv(2),
                               axis=1).astype(BF16)
    assert w_packed.shape[1] == MAIN_W + 2 * QKV_W
    w_dt = jnp.pad(wi[:, o_dt:o_g], ((0, 0), (0, LANE - SSM_HEADS))).astype(BF16)

    x2 = x.reshape(t, d)
    proj, qkv1, qkv2, dt_raw = _inproj(x2, sc1, sh1, norm_mix_w.reshape(1, d), w_packed, w_dt,
                                       rope_c, rope_s1, rope_s2, batch, seq)

    srcs = [(proj.reshape(batch, 1, seq, MAIN_W), COL_Q0 // GROUP_W), (qkv1, 0), (qkv2, 0)]
    o_list, st_list = [], []
    for g, (src, col0) in enumerate(srcs):
        o, st = _attention_group(src, g, batch, seq, col0)
        o_list.append(o)
        st_list.append(st)
    ssm = _ssd(proj, dt_raw, conv_w[0], conv_b[0], dt_bias[0], a_log[0], d_skip[0], ssm_norm_w[0], batch, seq)

    base, u2p, scores_t = _mix(
        o_list, st_list, ssm, proj, x2, g1, sc2, sh2, g2, norm_ffn_w.reshape(1, d),
        w_branch_attn[0].astype(BF16), w_branch_ssm[0].astype(BF16), w_out[0].astype(BF16),
        w_router[0].T, w_gate_s[0].astype(BF16), w_up_s[0].astype(BF16), w_down_s[0].astype(BF16), seq)

    idx, w_sel, member, counts = _route(scores_t, router_bias[0])

    rb = MOE_ROWS
    cnt = counts[:, 0].astype(I32)
    padded = (cnt + rb - 1) // rb * rb
    pad_end = jnp.cumsum(padded).astype(I32)
    pad_start = pad_end - padded
    n_rows = t * TOP_K + N_EXPERTS * rb

    dest = _dest(member, idx, pad_start)
    xs = _dispatch(u2p, dest, pad_end, padded, n_rows)
    ys = _experts(xs, pad_start, padded // rb, w_gate_e[0], w_up_e[0], w_down_e[0])
    out = _combine(ys, dest, w_sel.T, base, g2, norm_final_w.reshape(1, d), seq)
    return out.reshape(batch, seq, d)
```

```python
import functools
import math

import jax
import jax.numpy as jnp
from jax import lax
from jax.experimental import pallas as pl
from jax.experimental.pallas import tpu as pltpu

F32 = jnp.float32
BF16 = jnp.bfloat16
I32 = jnp.int32

LANE = 128
SUBLANE = 8
VMEM_LIMIT = 56 * 1024 * 1024

HEAD_DIM = 128
HEADS_PER_GROUP = 4
GROUP_W = HEADS_PER_GROUP * HEAD_DIM
ATTN_DILATIONS = (1, 4, 16)
ATTN_BLK = 128
ROPE_DIM = 32
ROPE_HALF = 16
ROPE_THETA = 500000.0
SSM_HEADS = 16
SSM_HEADDIM = 64
SSM_GROUPS = 4
SSM_STATE = 128
SSM_CONV = 4
SSM_CHUNK = 128
N_EXPERTS = 256
TOP_K = 8
N_EXPERT_GROUPS = 8
TOPK_GROUPS = 4
ROUTED_SCALE = 2.5
NORM_EPS = 1e-6
NEG = -1e30

COL_XBC, COL_GA, COL_GS, COL_Z, COL_Q0 = 0, 2048, 3072, 4096, 5120
MAIN_W = 6656
QKV_W = 3 * GROUP_W

INPROJ_TM, INPROJ_TN = 256, 512
MIX_TM = 256
ROUTE_TM = 512
MOE_ROWS = 256
EXPERT_IN_SLOTS, EXPERT_OUT_SLOTS = 4, 3
DISPATCH_TM = 256
COMBINE_TM = 256
ROW_TILE = 8


def _cp(sem, vmem=VMEM_LIMIT):
    return pltpu.CompilerParams(dimension_semantics=sem, vmem_limit_bytes=vmem)


def _sigmoid(x):
    return 1.0 / (1.0 + jnp.exp(-x))


def _silu(x):
    return x * _sigmoid(x)


def _mod_kernel(c_ref, w_ref, b_ref, o_ref):
    cond = _silu(c_ref[...])
    o_ref[...] = jnp.dot(cond, w_ref[...], preferred_element_type=F32) + b_ref[...]


def _modulation(c, w_mod, b_mod):
    b, d = c.shape
    n = w_mod.shape[1]
    return pl.pallas_call(
        _mod_kernel,
        grid=(n // d,),
        in_specs=[pl.BlockSpec((b, d), lambda j: (0, 0)),
                  pl.BlockSpec((d, d), lambda j: (0, j)),
                  pl.BlockSpec((1, d), lambda j: (0, j))],
        out_specs=pl.BlockSpec((b, d), lambda j: (0, j)),
        out_shape=jax.ShapeDtypeStruct((b, n), F32),
        compiler_params=_cp(("arbitrary",)),
        name="modulation",
    )(c, w_mod, b_mod.reshape(1, n))


def _rope_kernel(pos_ref, inv_ref, c_ref, s1_ref, s2_ref):
    ang = pos_ref[...].astype(F32) * inv_ref[...]
    lane = lax.broadcasted_iota(I32, ang.shape, 1)
    cos = jnp.cos(ang)
    sin = jnp.sin(ang)
    c_ref[...] = jnp.where(lane < ROPE_DIM, cos, 1.0)
    s1_ref[...] = jnp.where(lane < ROPE_HALF, -sin, 0.0)
    s2_ref[...] = jnp.where((lane >= ROPE_HALF) & (lane < ROPE_DIM), sin, 0.0)


def _rope_tables(positions):
    t = positions.size
    tm = 2048
    inv_freq = ROPE_THETA ** (-jnp.arange(ROPE_HALF, dtype=F32) / ROPE_HALF)
    inv_row = jnp.concatenate([inv_freq, inv_freq, jnp.zeros((LANE - ROPE_DIM,), F32)]).reshape(1, LANE)
    spec = pl.BlockSpec((tm, LANE), lambda i: (i, 0))
    shp = jax.ShapeDtypeStruct((t, LANE), F32)
    return pl.pallas_call(
        _rope_kernel,
        grid=(t // tm,),
        in_specs=[pl.BlockSpec((tm, 1), lambda i: (i, 0)), pl.BlockSpec((1, LANE), lambda i: (0, 0))],
        out_specs=[spec, spec, spec],
        out_shape=[shp, shp, shp],
        compiler_params=_cp(("arbitrary",)),
        name="rope_tables",
    )(positions.reshape(t, 1), inv_row)


def _inproj_kernel(x_ref, sc_ref, sh_ref, nw_ref, w_ref, wdt_ref, c_ref, s1_ref, s2_ref,
                   main_ref, g1_ref, g2_ref, dt_ref, u_scr, rope_scr, uc_scr, *, tn, q_scale):
    tm = x_ref.shape[0]
    n_main = MAIN_W // tn
    n_qkv = QKV_W // tn

    x = x_ref[...]
    ms = jnp.mean(x * x, axis=-1, keepdims=True)
    y = x * lax.rsqrt(ms + NORM_EPS) * nw_ref[...]
    uf = y * (1.0 + sc_ref[...]) + sh_ref[...]
    u = uf.astype(BF16)
    dt_ref[...] = jnp.dot(u, wdt_ref[...], preferred_element_type=F32)
    n_chunk = uf.shape[1] // LANE
    for cc in range(n_chunk):
        uc_scr[cc] = uf[:, cc * LANE:(cc + 1) * LANE]
    for o, d in enumerate(ATTN_DILATIONS[1:]):
        rows = tm // d
        for r in range(d):
            for cc in range(n_chunk):
                u_scr[o, r * rows:(r + 1) * rows, cc * LANE:(cc + 1) * LANE] = (
                    uc_scr[cc, pl.ds(r, rows, stride=d), :].astype(BF16))
            for ti, tab in enumerate((c_ref, s1_ref, s2_ref)):
                rope_scr[o, ti, r * rows:(r + 1) * rows, :] = tab[pl.ds(r, rows, stride=d), :]

    for c in range(n_main + 2 * n_qkv):
        order = 0 if c < n_main else (1 if c < n_main + n_qkv else 2)
        jq = c - COL_Q0 // tn if order == 0 else (c - n_main - (order - 1) * n_qkv)
        lhs = u if order == 0 else u_scr[order - 1]
        acc = jnp.dot(lhs, w_ref[:, c * tn:(c + 1) * tn], preferred_element_type=F32)
        if jq in (0, 1):
            scale = q_scale if jq == 0 else 1.0
            tabs = (c_ref, s1_ref, s2_ref) if order == 0 else tuple(rope_scr.at[order - 1, ti] for ti in range(3))
            cs = tabs[0][...] * scale
            s1 = tabs[1][...] * scale
            s2 = tabs[2][...] * scale
            parts = []
            for h in range(tn // HEAD_DIM):
                a = acc[:, h * HEAD_DIM:(h + 1) * HEAD_DIM]
                parts.append(a * cs + pltpu.roll(a, LANE - ROPE_HALF, 1) * s1 + pltpu.roll(a, ROPE_HALF, 1) * s2)
            acc = jnp.concatenate(parts, axis=1)
        val = acc.astype(BF16)
        if order == 0:
            main_ref[:, c * tn:(c + 1) * tn] = val
        else:
            dst = g1_ref if order == 1 else g2_ref
            c0 = (c - n_main - (order - 1) * n_qkv) * tn
            dst[:, :, c0:c0 + tn] = val.reshape(dst.shape[0], dst.shape[1], tn)


def _inproj(x2, sc1, sh1, norm_w, w_packed, w_dt, rope_c, rope_s1, rope_s2, batch, seq):
    t, d = x2.shape
    tm, tn = INPROJ_TM, INPROJ_TN
    n = w_packed.shape[1]
    per_b = seq // tm
    d1, d2 = ATTN_DILATIONS[1], ATTN_DILATIONS[2]
    row = lambda i: (i, 0)
    modrow = lambda i: (i // per_b, 0, 0)
    const = lambda i: (0, 0)
    resident = pl.Buffered(1)
    qkv_map = lambda i: (i // per_b, 0, i % per_b, 0)
    return pl.pallas_call(
        functools.partial(_inproj_kernel, tn=tn, q_scale=1.0 / math.sqrt(HEAD_DIM)),
        grid=(t // tm,),
        in_specs=[pl.BlockSpec((tm, d), row),
                  pl.BlockSpec((None, 1, d), modrow),
                  pl.BlockSpec((None, 1, d), modrow),
                  pl.BlockSpec((1, d), const),
                  pl.BlockSpec((d, n), const, pipeline_mode=resident),
                  pl.BlockSpec((d, LANE), const),
                  pl.BlockSpec((tm, LANE), row),
                  pl.BlockSpec((tm, LANE), row),
                  pl.BlockSpec((tm, LANE), row)],
        out_specs=[pl.BlockSpec((tm, MAIN_W), row),
                   pl.BlockSpec((None, d1, tm // d1, QKV_W), qkv_map),
                   pl.BlockSpec((None, d2, tm // d2, QKV_W), qkv_map),
                   pl.BlockSpec((tm, LANE), row)],
        out_shape=[jax.ShapeDtypeStruct((t, MAIN_W), BF16),
                   jax.ShapeDtypeStruct((batch, d1, seq // d1, QKV_W), BF16),
                   jax.ShapeDtypeStruct((batch, d2, seq // d2, QKV_W), BF16),
                   jax.ShapeDtypeStruct((t, LANE), F32)],
        scratch_shapes=[pltpu.VMEM((2, tm, d), BF16), pltpu.VMEM((2, 3, tm, LANE), F32),
                        pltpu.VMEM((d // LANE, tm, LANE), F32)],
        compiler_params=_cp(("arbitrary",)),
        name="inproj",
    )(x2, sc1, sh1, norm_w, w_packed, w_dt, rope_c, rope_s1, rope_s2)


def _attn_kernel(q_ref, k_ref, v_ref, o_ref, st_ref, *, d, nb):
    blk = ATTN_BLK
    qi = lax.broadcasted_iota(I32, (blk, 2 * blk), 0)
    kj = lax.broadcasted_iota(I32, (blk, 2 * blk), 1)
    band = (kj >= qi) & (kj <= qi + blk)
    qi1 = lax.broadcasted_iota(I32, (blk, blk), 0)
    kj1 = lax.broadcasted_iota(I32, (blk, blk), 1)
    causal = kj1 <= qi1
    lane = kj1

    def block(r, q0, k0, nk, mask):
        rows = pl.ds(q0, blk) if d == 1 else pl.ds(q0 * d + r, blk, stride=d)
        hsl = [slice(h * HEAD_DIM, (h + 1) * HEAD_DIM) for h in range(HEADS_PER_GROUP)]
        s = jnp.concatenate(
            [lax.dot_general(q_ref[r, pl.ds(q0, blk), hs], k_ref[r, pl.ds(k0, nk), hs], (((1,), (1,)), ((), ())),
                             preferred_element_type=F32) for hs in hsl], axis=0)
        s = jnp.where(jnp.concatenate([mask] * HEADS_PER_GROUP, axis=0), s, NEG)
        m = jnp.max(s, axis=-1, keepdims=True)
        p = jnp.exp(s - m)
        l = jnp.sum(p, axis=-1, keepdims=True)
        pb = p.astype(BF16)
        st = jnp.zeros((blk, LANE), F32)
        for h, hs in enumerate(hsl):
            hr = slice(h * blk, (h + 1) * blk)
            o = jnp.dot(pb[hr], v_ref[r, pl.ds(k0, nk), hs], preferred_element_type=F32)
            o_ref[h, rows, :] = o / l[hr]
            st = jnp.where(lane == h, m[hr], st)
            st = jnp.where(lane == HEADS_PER_GROUP + h, l[hr], st)
        st_ref[rows, :] = st

    def per_residue(r, carry):
        block(r, 0, 0, blk, causal)
        if nb > 1:
            def body(n, c):
                q0 = pl.multiple_of(n * blk, blk)
                block(r, q0, pl.multiple_of(q0 - blk, blk), 2 * blk, band)
                return c
            lax.fori_loop(1, nb, body, 0)
        return carry

    if d == 1:
        per_residue(0, 0)
    else:
        lax.fori_loop(0, d, per_residue, 0)


def _attention_group(src, g, batch, seq, col0):
    d = ATTN_DILATIONS[g]
    n_sub = seq // d
    nb = n_sub // ATTN_BLK
    spec = lambda c: pl.BlockSpec((None, d, n_sub, GROUP_W), lambda b: (b, 0, 0, c))
    o, st = pl.pallas_call(
        functools.partial(_attn_kernel, d=d, nb=nb),
        grid=(batch,),
        in_specs=[spec(col0), spec(col0 + 1), spec(col0 + 2)],
        out_specs=[pl.BlockSpec((HEADS_PER_GROUP, seq, HEAD_DIM), lambda b: (0, b, 0)),
                   pl.BlockSpec((seq, LANE), lambda b: (b, 0))],
        out_shape=[jax.ShapeDtypeStruct((HEADS_PER_GROUP, batch * seq, HEAD_DIM), F32),
                   jax.ShapeDtypeStruct((batch * seq, LANE), F32)],
        compiler_params=_cp(("arbitrary",)),
        name=f"attn_d{d}",
    )(src, src, src)
    return o, st


def _ssd_kernel(xbc_ref, z_ref, dt_ref, cw_ref, cb_ref, dtb_ref, alog_ref, dsk_ref, nw_ref, expand_ref,
                out_ref, xpad, state):
    L = SSM_CHUNK
    inner = SSM_HEADS * SSM_HEADDIM
    gw = SSM_STATE
    c = pl.program_id(1)

    @pl.when(c == 0)
    def _():
        xpad[0:SUBLANE, :] = jnp.zeros((SUBLANE, xpad.shape[1]), F32)
        state[...] = jnp.zeros(state.shape, F32)

    xpad[SUBLANE:SUBLANE + L, :] = xbc_ref[...].astype(F32)
    conv = cb_ref[...] + cw_ref[0:1, :] * xpad[SUBLANE - 3:SUBLANE - 3 + L, :]
    for k in range(1, SSM_CONV):
        conv = conv + cw_ref[k:k + 1, :] * xpad[SUBLANE - 3 + k:SUBLANE - 3 + k + L, :]
    xpad[0:SUBLANE, :] = xpad[L:L + SUBLANE, :]
    act = _silu(conv)
    xs = act[:, :inner]

    lane = lax.broadcasted_iota(I32, (L, LANE), 1)
    row = lax.broadcasted_iota(I32, (L, LANE), 0)
    dtr = dt_ref[...] + dtb_ref[...]
    dt = jnp.maximum(dtr, 0.0) + jnp.log(1.0 + jnp.exp(-jnp.abs(dtr)))
    a_neg = jnp.where(lane < SSM_HEADS, -jnp.exp(alog_ref[...]), 0.0)
    a = dt * a_neg
    cs = a
    shift = 1
    while shift < L:
        cs = cs + jnp.where(row >= shift, pltpu.roll(cs, shift, 0), 0.0)
        shift *= 2
    cs_t = cs.T
    tri = row >= lane
    dt_x = jnp.dot(dt, expand_ref[...], preferred_element_type=F32, precision=lax.Precision.HIGHEST)
    xp = (xs * dt_x).astype(BF16)
    half = lane < SSM_HEADDIM
    zero_b = jnp.zeros((L, LANE), BF16)

    for g in range(SSM_GROUPS):
        bg = act[:, inner + g * gw:inner + (g + 1) * gw]
        cg = act[:, inner + SSM_GROUPS * gw + g * gw:inner + SSM_GROUPS * gw + (g + 1) * gw]
        cg_b = cg.astype(BF16)
        cb = lax.dot_general(cg_b, bg.astype(BF16), (((1,), (1,)), ((), ())), preferred_element_type=F32)
        bg_t = bg.T
        for pair in range(2):
            h0 = g * 4 + pair * 2
            pidx = h0 // 2
            xpp = xp[:, pidx * LANE:(pidx + 1) * LANE]
            rhs = jnp.concatenate([jnp.where(half, xpp, zero_b), jnp.where(half, zero_b, xpp)], axis=0)
            dec, dst, eoff, cdec = [], [], [], []
            for h in (h0, h0 + 1):
                cs_col = cs[:, h:h + 1]
                cs_row = cs_t[h:h + 1, :]
                dec.append(cb * jnp.exp(jnp.where(tri, cs_col - cs_row, NEG)))
                cs_last = cs_row[:, L - 1:L]
                dst.append(bg_t * jnp.exp(cs_last - cs_row))
                eoff.append(jnp.exp(cs_col))
                cdec.append(jnp.exp(cs_last))
            y_diag = jnp.dot(jnp.concatenate(dec, axis=1).astype(BF16), rhs, preferred_element_type=F32)
            st_new = jnp.dot(jnp.concatenate(dst, axis=1).astype(BF16), rhs, preferred_element_type=F32)
            prev = state[pidx]
            y_off = jnp.dot(cg_b, prev.astype(BF16), preferred_element_type=F32)
            y_off = y_off * jnp.where(half, eoff[0], eoff[1])
            state[pidx] = prev * jnp.where(half, cdec[0], cdec[1]) + st_new
            y = y_diag + y_off + dsk_ref[:, pidx * LANE:(pidx + 1) * LANE] * xs[:, pidx * LANE:(pidx + 1) * LANE]
            out_pair = y * _silu(z_ref[:, pidx * LANE:(pidx + 1) * LANE].astype(F32))
            xpad[SUBLANE:SUBLANE + L, pidx * LANE:(pidx + 1) * LANE] = out_pair

    gsz = inner // SSM_GROUPS
    for g in range(SSM_GROUPS):
        yg = xpad[SUBLANE:SUBLANE + L, g * gsz:(g + 1) * gsz]
        ms = jnp.mean(yg * yg, axis=-1, keepdims=True)
        out_ref[:, g * gsz:(g + 1) * gsz] = (yg * lax.rsqrt(ms + NORM_EPS) * nw_ref[:, g * gsz:(g + 1) * gsz]).astype(BF16)


def _ssd(proj, dt_raw, conv_w, conv_b, dt_bias, a_log, d_skip, ssm_norm_w, batch, seq):
    t = batch * seq
    L = SSM_CHUNK
    nc = seq // L
    inner = SSM_HEADS * SSM_HEADDIM
    cdim = conv_w.shape[1]
    pad16 = lambda v: jnp.pad(v.astype(F32), (0, LANE - SSM_HEADS)).reshape(1, LANE)
    expand = (jnp.arange(LANE)[:, None] == (jnp.arange(inner)[None, :] // SSM_HEADDIM)).astype(F32)
    dsk = jnp.repeat(d_skip.astype(F32), SSM_HEADDIM).reshape(1, inner)
    rowc = lambda b, c: (b * nc + c, 0)
    const = lambda b, c: (0, 0)
    return pl.pallas_call(
        _ssd_kernel,
        grid=(batch, nc),
        in_specs=[pl.BlockSpec((L, cdim), lambda b, c: (b * nc + c, COL_XBC // cdim)),
                  pl.BlockSpec((L, inner), lambda b, c: (b * nc + c, COL_Z // inner)),
                  pl.BlockSpec((L, LANE), rowc),
                  pl.BlockSpec((SSM_CONV, cdim), const),
                  pl.BlockSpec((1, cdim), const),
                  pl.BlockSpec((1, LANE), const),
                  pl.BlockSpec((1, LANE), const),
                  pl.BlockSpec((1, inner), const),
                  pl.BlockSpec((1, inner), const),
                  pl.BlockSpec((LANE, inner), const)],
        out_specs=pl.BlockSpec((L, inner), rowc),
        out_shape=jax.ShapeDtypeStruct((t, inner), BF16),
        scratch_shapes=[pltpu.VMEM((L + 2 * SUBLANE, cdim), F32),
                        pltpu.VMEM((SSM_HEADS // 2, SSM_STATE, 2 * SSM_HEADDIM), F32)],
        compiler_params=_cp(("arbitrary", "arbitrary")),
        name="ssd",
    )(proj, proj, dt_raw, conv_w.astype(F32), conv_b.reshape(1, cdim).astype(F32), pad16(dt_bias), pad16(a_log),
      dsk, ssm_norm_w.reshape(1, inner).astype(F32), expand)


def _mix_kernel(o0_ref, o1_ref, o2_ref, s0_ref, s1_ref, s2_ref, ssm_ref, ga_ref, gs_ref, x_ref,
                g1_ref, sc2_ref, sh2_ref, g2_ref, nw_ref, wba_ref, wbs_ref, wo_ref, wrt_ref,
                wgs_ref, wus_ref, wds_ref, base_ref, u2t_ref, sct_ref):
    o_refs = (o0_ref, o1_ref, o2_ref)
    s_refs = (s0_ref, s1_ref, s2_ref)
    heads = []
    for h in range(HEADS_PER_GROUP):
        ms = [s[:, h:h + 1] for s in s_refs]
        ls = [s[:, HEADS_PER_GROUP + h:HEADS_PER_GROUP + h + 1] for s in s_refs]
        mx = jnp.maximum(jnp.maximum(ms[0], ms[1]), ms[2])
        wts = [l * jnp.exp(m - mx) for m, l in zip(ms, ls)]
        num = wts[0] * o_refs[0][h] + wts[1] * o_refs[1][h] + wts[2] * o_refs[2][h]
        heads.append((num / (wts[0] + wts[1] + wts[2])).astype(BF16))
    attn = jnp.concatenate(heads, axis=1)
    ya = jnp.dot(attn, wba_ref[...], preferred_element_type=F32)
    ys = jnp.dot(ssm_ref[...], wbs_ref[...], preferred_element_type=F32)
    merged = _sigmoid(ga_ref[...].astype(F32)) * ya + _sigmoid(gs_ref[...].astype(F32)) * ys
    mix = jnp.dot(merged.astype(BF16), wo_ref[...], preferred_element_type=F32)
    h1 = x_ref[...] + g1_ref[...] * mix
    ms2 = jnp.mean(h1 * h1, axis=-1, keepdims=True)
    u2 = h1 * lax.rsqrt(ms2 + NORM_EPS) * nw_ref[...] * (1.0 + sc2_ref[...]) + sh2_ref[...]
    tm = u2.shape[0]
    for cc in range(ROW_TILE):
        u2t_ref[pl.ds(cc, tm, stride=ROW_TILE), :] = u2[:, cc * LANE:(cc + 1) * LANE]
    logits_t = lax.dot_general(wrt_ref[...], u2, (((1,), (1,)), ((), ())), preferred_element_type=F32,
                               precision=lax.Precision.HIGHEST)
    sct_ref[...] = _sigmoid(logits_t)
    u2b = u2.astype(BF16)
    hs_ = _silu(jnp.dot(u2b, wgs_ref[...], preferred_element_type=F32)) * jnp.dot(u2b, wus_ref[...], preferred_element_type=F32)
    shared = jnp.dot(hs_.astype(BF16), wds_ref[...], preferred_element_type=F32)
    base_ref[...] = h1 + g2_ref[...] * shared


def _mix(o_list, st_list, ssm, proj, x2, g1, sc2, sh2, g2, norm_w, w_ba, w_bs, w_o, w_rt, w_gs, w_us, w_ds, seq):
    t, d = x2.shape
    tm = MIX_TM
    per_b = seq // tm
    row = lambda i: (i, 0)
    modrow = lambda i: (i // per_b, 0, 0)
    const = lambda i: (0, 0)
    full = lambda a: pl.BlockSpec(a.shape, const)
    mod = pl.BlockSpec((None, 1, d), modrow)
    return pl.pallas_call(
        _mix_kernel,
        grid=(t // tm,),
        in_specs=[pl.BlockSpec((HEADS_PER_GROUP, tm, HEAD_DIM), lambda i: (0, i, 0))] * 3
        + [pl.BlockSpec((tm, LANE), row)] * 3 + [
            pl.BlockSpec((tm, d), row),
            pl.BlockSpec((tm, d), lambda i: (i, COL_GA // d)),
            pl.BlockSpec((tm, d), lambda i: (i, COL_GS // d)),
            pl.BlockSpec((tm, d), row),
            mod, mod, mod, mod, full(norm_w), full(w_ba), full(w_bs), full(w_o), full(w_rt),
            full(w_gs), full(w_us), full(w_ds)],
        out_specs=[pl.BlockSpec((tm, d), row), pl.BlockSpec((tm * ROW_TILE, LANE), row),
                   pl.BlockSpec((N_EXPERTS, tm), lambda i: (0, i))],
        out_shape=[jax.ShapeDtypeStruct((t, d), F32), jax.ShapeDtypeStruct((t * ROW_TILE, LANE), F32),
                   jax.ShapeDtypeStruct((N_EXPERTS, t), F32)],
        compiler_params=_cp(("arbitrary",)),
        name="mix",
    )(*o_list, *st_list, ssm, proj, proj, x2, g1, sc2, sh2, g2, norm_w, w_ba, w_bs, w_o, w_rt, w_gs, w_us, w_ds)


def _route_kernel(sct_ref, bias_ref, idx_ref, w_ref, mem_ref, cnt_ref, run):
    i = pl.program_id(0)
    tm = sct_ref.shape[1]
    per_g = N_EXPERTS // N_EXPERT_GROUPS

    @pl.when(i == 0)
    def _():
        run[...] = jnp.zeros(run.shape, F32)

    s = sct_ref[...]
    biased = s + bias_ref[...]
    io_g = lax.broadcasted_iota(I32, (per_g, tm), 0).astype(F32)
    gscore = []
    for g in range(N_EXPERT_GROUPS):
        bgp = biased[g * per_g:(g + 1) * per_g, :]
        m1 = jnp.max(bgp, axis=0, keepdims=True)
        first = jnp.min(jnp.where(bgp == m1, io_g, float(per_g)), axis=0, keepdims=True)
        m2 = jnp.max(jnp.where(io_g == first, NEG, bgp), axis=0, keepdims=True)
        gscore.append(m1 + m2)
    gs = jnp.concatenate(gscore, axis=0)
    io8 = lax.broadcasted_iota(I32, (N_EXPERT_GROUPS, tm), 0).astype(F32)
    gsel = jnp.zeros((N_EXPERT_GROUPS, tm), F32)
    cur = gs
    for _ in range(TOPK_GROUPS):
        mx = jnp.max(cur, axis=0, keepdims=True)
        fi = jnp.min(jnp.where(cur == mx, io8, float(N_EXPERT_GROUPS)), axis=0, keepdims=True)
        hit = io8 == fi
        gsel = jnp.where(hit, 1.0, gsel)
        cur = jnp.where(hit, NEG, cur)
    masked = jnp.concatenate(
        [jnp.where(gsel[g:g + 1, :] > 0.5, biased[g * per_g:(g + 1) * per_g, :], NEG) for g in range(N_EXPERT_GROUPS)],
        axis=0)
    io_e = lax.broadcasted_iota(I32, (N_EXPERTS, tm), 0).astype(F32)
    member = jnp.zeros((N_EXPERTS, tm), F32)
    idxs, ws = [], []
    for _ in range(TOP_K):
        mx = jnp.max(masked, axis=0, keepdims=True)
        fi = jnp.min(jnp.where(masked == mx, io_e, float(N_EXPERTS)), axis=0, keepdims=True)
        hit = io_e == fi
        idxs.append(fi)
        ws.append(jnp.sum(jnp.where(hit, s, 0.0), axis=0, keepdims=True))
        member = jnp.where(hit, 1.0, member)
        masked = jnp.where(hit, NEG, masked)
    wsum = ws[0]
    for k in range(1, TOP_K):
        wsum = wsum + ws[k]
    idx_ref[...] = jnp.concatenate(idxs, axis=0).astype(I32)
    w_ref[...] = jnp.concatenate([w / wsum * ROUTED_SCALE for w in ws], axis=0)
    mem_ref[...] = member.astype(BF16)
    new_run = run[...] + jnp.sum(member, axis=1, keepdims=True)
    run[...] = new_run
    cnt_ref[...] = new_run


def _route(scores_t, router_bias):
    e, t = scores_t.shape
    tm = ROUTE_TM
    tok = pl.BlockSpec((TOP_K, tm), lambda i: (0, i))
    return pl.pallas_call(
        _route_kernel,
        grid=(t // tm,),
        in_specs=[pl.BlockSpec((e, tm), lambda i: (0, i)), pl.BlockSpec((e, 1), lambda i: (0, 0))],
        out_specs=[tok, tok, pl.BlockSpec((e, tm), lambda i: (0, i)), pl.BlockSpec((e, LANE), lambda i: (0, 0))],
        out_shape=[jax.ShapeDtypeStruct((TOP_K, t), I32), jax.ShapeDtypeStruct((TOP_K, t), F32),
                   jax.ShapeDtypeStruct((e, t), BF16), jax.ShapeDtypeStruct((e, LANE), F32)],
        scratch_shapes=[pltpu.VMEM((e, LANE), F32)],
        compiler_params=_cp(("arbitrary",)),
        name="route",
    )(scores_t, router_bias.reshape(e, 1).astype(F32))


def _dest_kernel(mem_ref, idx_ref, start_ref, dest_ref, run):
    i = pl.program_id(0)
    e, tm = mem_ref.shape

    @pl.when(i == 0)
    def _():
        run[...] = jnp.broadcast_to(start_ref[...], run.shape)

    member = mem_ref[...]
    tr = lax.broadcasted_iota(I32, (tm, tm), 0)
    tc = lax.broadcasted_iota(I32, (tm, tm), 1)
    upper = jnp.where(tr < tc, 1.0, 0.0).astype(BF16)
    rank_full = jnp.dot(member, upper, preferred_element_type=F32) + run[:, 0:1]
    io_e = lax.broadcasted_iota(I32, (e, tm), 0)
    idx = idx_ref[...]
    rows = [jnp.sum(jnp.where(io_e == idx[k:k + 1, :], rank_full, 0.0), axis=0, keepdims=True) for k in range(TOP_K)]
    dest_ref[...] = jnp.concatenate(rows, axis=0).astype(I32)
    run[...] = run[...] + jnp.sum(member.astype(F32), axis=1, keepdims=True)


def _dest(member, idx, pad_start):
    e, t = member.shape
    tm = ROUTE_TM
    return pl.pallas_call(
        _dest_kernel,
        grid=(t // tm,),
        in_specs=[pl.BlockSpec((e, tm), lambda i: (0, i)), pl.BlockSpec((TOP_K, tm), lambda i: (0, i)),
                  pl.BlockSpec((e, 1), lambda i: (0, 0))],
        out_specs=pl.BlockSpec((TOP_K, tm), lambda i: (0, i)),
        out_shape=jax.ShapeDtypeStruct((TOP_K, t), I32),
        scratch_shapes=[pltpu.VMEM((e, LANE), F32)],
        compiler_params=_cp(("arbitrary",)),
        name="dest",
    )(member, idx, pad_start.astype(F32).reshape(e, 1))


def _dispatch_kernel(pend_ref, npad_ref, dest_ref, u_ref, xs_ref, zero_scr, sem, zsem):
    i = pl.program_id(0)
    rt = ROW_TILE
    tm = u_ref.shape[0] // rt
    zrows = zero_scr.shape[0]

    @pl.when(i == 0)
    def _():
        zero_scr[...] = jnp.zeros(zero_scr.shape, zero_scr.dtype)

        def zcopy(e):
            start = pl.multiple_of(pend_ref[e] * rt - zrows, zrows)
            return pltpu.make_async_copy(zero_scr, xs_ref.at[pl.ds(start, zrows)], zsem)

        def zstart(e, c):
            @pl.when(npad_ref[e] > 0)
            def _():
                zcopy(e).start()
            return c

        def zwait(e, c):
            @pl.when(npad_ref[e] > 0)
            def _():
                zcopy(e).wait()
            return c

        lax.fori_loop(0, N_EXPERTS, zstart, 0)
        lax.fori_loop(0, N_EXPERTS, zwait, 0)

    def row_copy(t, k):
        src = u_ref.at[pl.ds(pl.multiple_of(t * rt, rt), rt)]
        dst = xs_ref.at[pl.ds(pl.multiple_of(dest_ref[t * TOP_K + k] * rt, rt), rt)]
        return pltpu.make_async_copy(src, dst, sem)

    def start(t, c):
        for k in range(TOP_K):
            row_copy(t, k).start(priority=k % 2)
        return c

    def wait(t, c):
        for k in range(TOP_K):
            row_copy(t, k).wait()
        return c

    lax.fori_loop(0, tm, start, 0)
    lax.fori_loop(0, tm, wait, 0)


def _dispatch(u2t, dest_flat, pad_end, padded, n_rows):
    rt = ROW_TILE
    t = u2t.shape[0] // rt
    tm = DISPATCH_TM
    gs = pltpu.PrefetchScalarGridSpec(
        num_scalar_prefetch=2,
        grid=(t // tm,),
        in_specs=[pl.BlockSpec((tm * TOP_K,), lambda i, pe, npd: (i,), memory_space=pltpu.SMEM),
                  pl.BlockSpec((tm * rt, LANE), lambda i, pe, npd: (i, 0))],
        out_specs=pl.BlockSpec(memory_space=pl.ANY),
        scratch_shapes=[pltpu.VMEM((MOE_ROWS * rt, LANE), F32), pltpu.SemaphoreType.DMA, pltpu.SemaphoreType.DMA],
    )
    return pl.pallas_call(
        _dispatch_kernel,
        grid_spec=gs,
        out_shape=jax.ShapeDtypeStruct((n_rows * rt, LANE), F32),
        compiler_params=_cp(("arbitrary",)),
        name="dispatch",
    )(pad_end, padded, dest_flat, u2t)


def _expert_kernel(bstart_ref, nblk_ref, wg_ref, wu_ref, wd_ref, xs_ref, ys_ref,
                   wg_b, wu_b, wd_b, xbuf, ybuf, in_sem, out_sem):
    e = pl.program_id(0)
    last = pl.num_programs(0) - 1
    rt = ROW_TILE
    n_in, rb = xbuf.shape[0], xbuf.shape[1] // rt
    n_out = ybuf.shape[0]
    n = nblk_ref[e]
    g0 = bstart_ref[e]
    total = bstart_ref[last] + nblk_ref[last]

    def block_rows(g):
        return pl.ds(pl.multiple_of(g * (rb * rt), rb * rt), rb * rt)

    def in_copy(g):
        return pltpu.make_async_copy(xs_ref.at[block_rows(g)], xbuf.at[g % n_in], in_sem.at[g % n_in])

    def out_copy(g):
        return pltpu.make_async_copy(ybuf.at[g % n_out], ys_ref.at[block_rows(g)], out_sem.at[g % n_out])

    @pl.when(e == 0)
    def _():
        for g in range(n_in - 1):
            @pl.when(g < total)
            def _():
                in_copy(g).start()

    wg_b[...] = wg_ref[...].astype(BF16)
    wu_b[...] = wu_ref[...].astype(BF16)
    wd_b[...] = wd_ref[...].astype(BF16)

    def body(b, carry):
        g = g0 + b
        in_copy(g).wait()

        @pl.when(g + n_in - 1 < total)
        def _():
            in_copy(g + n_in - 1).start()

        @pl.when(g >= n_out)
        def _():
            out_copy(g - n_out).wait()

        xb = jnp.concatenate([xbuf[g % n_in, pl.ds(cc, rb, stride=rt), :].astype(BF16) for cc in range(rt)], axis=1)
        gate = jnp.dot(xb, wg_b[...], preferred_element_type=F32)
        up = jnp.dot(xb, wu_b[...], preferred_element_type=F32)
        h = (_silu(gate) * up).astype(BF16)
        y = jnp.dot(h, wd_b[...], preferred_element_type=F32)
        for cc in range(rt):
            ybuf[g % n_out, pl.ds(cc, rb, stride=rt), :] = y[:, cc * LANE:(cc + 1) * LANE]
        out_copy(g).start()
        return carry

    lax.fori_loop(0, n, body, 0)

    @pl.when(e == last)
    def _():
        for j in range(n_out, 0, -1):
            @pl.when(total >= j)
            def _():
                out_copy(total - j).wait()


def _experts(xs, blk_start, nblk, w_gate_e, w_up_e, w_down_e):
    ne, d, ff = w_gate_e.shape
    rb = MOE_ROWS * ROW_TILE
    gs = pltpu.PrefetchScalarGridSpec(
        num_scalar_prefetch=2,
        grid=(ne,),
        in_specs=[pl.BlockSpec((None, d, ff), lambda e, ps, nb: (e, 0, 0)),
                  pl.BlockSpec((None, d, ff), lambda e, ps, nb: (e, 0, 0)),
                  pl.BlockSpec((None, ff, d), lambda e, ps, nb: (e, 0, 0)),
                  pl.BlockSpec(memory_space=pl.ANY)],
        out_specs=pl.BlockSpec(memory_space=pl.ANY),
        scratch_shapes=[pltpu.VMEM((d, ff), BF16), pltpu.VMEM((d, ff), BF16), pltpu.VMEM((ff, d), BF16),
                        pltpu.VMEM((EXPERT_IN_SLOTS, rb, LANE), F32), pltpu.VMEM((EXPERT_OUT_SLOTS, rb, LANE), F32),
                        pltpu.SemaphoreType.DMA((EXPERT_IN_SLOTS,)), pltpu.SemaphoreType.DMA((EXPERT_OUT_SLOTS,))],
    )
    return pl.pallas_call(
        _expert_kernel,
        grid_spec=gs,
        out_shape=jax.ShapeDtypeStruct(xs.shape, F32),
        compiler_params=_cp(("arbitrary",)),
        name="experts",
    )(blk_start, nblk, w_gate_e, w_up_e, w_down_e, xs)


def _combine_kernel(dest_ref, wt_ref, base_ref, g2_ref, nfw_ref, ys_ref, out_ref, buf, sem):
    tm = base_ref.shape[0]
    rt = ROW_TILE

    def row_copy(t, k):
        src = ys_ref.at[pl.ds(pl.multiple_of(dest_ref[t * TOP_K + k] * rt, rt), rt)]
        return pltpu.make_async_copy(src, buf.at[k, pl.ds(pl.multiple_of(t * rt, rt), rt)], sem)

    def start(t, c):
        for k in range(TOP_K):
            row_copy(t, k).start(priority=k % 2)
        return c

    def wait(t, c):
        for k in range(TOP_K):
            row_copy(t, k).wait()
        return c

    lax.fori_loop(0, tm, start, 0)
    lax.fori_loop(0, tm, wait, 0)
    def weigh(t, c):
        rows = pl.ds(pl.multiple_of(t * rt, rt), rt)
        acc = buf[0, rows, :] * wt_ref[t * TOP_K]
        for k in range(1, TOP_K):
            acc = acc + buf[k, rows, :] * wt_ref[t * TOP_K + k]
        buf[0, rows, :] = acc
        return c

    lax.fori_loop(0, tm, weigh, 0, unroll=4)
    ssq = jnp.zeros((tm, 1), F32)
    for cc in range(rt):
        cs = slice(cc * LANE, (cc + 1) * LANE)
        h2 = base_ref[:, cs] + g2_ref[:, cs] * buf[0, pl.ds(cc, tm, stride=rt), :]
        out_ref[:, cs] = h2
        ssq = ssq + jnp.sum(h2 * h2, axis=-1, keepdims=True)
    inv = lax.rsqrt(ssq / (rt * LANE) + NORM_EPS)
    out_ref[...] = out_ref[...] * inv * nfw_ref[...]


def _combine(ys, dest_flat, w_tok, base, g2, norm_final_w, seq):
    t, d = base.shape
    tm = COMBINE_TM
    per_b = seq // tm
    gs = pltpu.PrefetchScalarGridSpec(
        num_scalar_prefetch=0,
        grid=(t // tm,),
        in_specs=[pl.BlockSpec((tm * TOP_K,), lambda i: (i,), memory_space=pltpu.SMEM),
                  pl.BlockSpec((tm * TOP_K,), lambda i: (i,), memory_space=pltpu.SMEM),
                  pl.BlockSpec((tm, d), lambda i: (i, 0)),
                  pl.BlockSpec((None, 1, d), lambda i: (i // per_b, 0, 0)),
                  pl.BlockSpec((1, d), lambda i: (0, 0)),
                  pl.BlockSpec(memory_space=pl.ANY)],
        out_specs=pl.BlockSpec((tm, d), lambda i: (i, 0)),
        scratch_shapes=[pltpu.VMEM((TOP_K, tm * ROW_TILE, LANE), F32), pltpu.SemaphoreType.DMA],
    )
    return pl.pallas_call(
        _combine_kernel,
        grid_spec=gs,
        out_shape=jax.ShapeDtypeStruct((t, d), F32),
        compiler_params=_cp(("arbitrary",)),
        name="combine",
    )(dest_flat, w_tok, base, g2, norm_final_w, ys)


def kernel(x, c, positions, w_mod, b_mod, norm_mix_w, norm_ffn_w, w_in, conv_w, conv_b, dt_bias, a_log, d_skip,
           ssm_norm_w, w_branch_attn, w_branch_ssm, w_out, w_router, router_bias, w_gate_e, w_up_e, w_down_e,
           w_gate_s, w_up_s, w_down_s, norm_final_w):
    batch, seq, d = x.shape
    t = batch * seq
    assert w_mod.shape[0] == 1, "one layer"
    assert d == ROW_TILE * LANE and seq % INPROJ_TM == 0 and INPROJ_TM % (ATTN_DILATIONS[-1] * 16) == 0

    mod = _modulation(c, w_mod[0], b_mod[0])
    sh1, sc1, g1, sh2, sc2, g2 = [m.reshape(batch, 1, d) for m in jnp.split(mod, 6, axis=-1)]
    rope_c, rope_s1, rope_s2 = _rope_tables(positions)

    wi = w_in[0]
    q_dim = 3 * GROUP_W
    o_z = 3 * q_dim
    o_xbc = o_z + d
    o_dt = o_xbc + conv_w.shape[2]
    o_g = o_dt + SSM_HEADS
    qkv = lambda g: [wi[:, s * q_dim + g * GROUP_W:s * q_dim + (g + 1) * GROUP_W] for s in range(3)]
    w_packed = jnp.concatenate([wi[:, o_xbc:o_dt], wi[:, o_g:], wi[:, o_z:o_xbc]] + qkv(0) + qkv(1) + qkv(2),
                               axis=1).astype(BF16)
    assert w_packed.shape[1] == MAIN_W + 2 * QKV_W
    w_dt = jnp.pad(wi[:, o_dt:o_g], ((0, 0), (0, LANE - SSM_HEADS))).astype(BF16)

    x2 = x.reshape(t, d)
    proj, qkv1, qkv2, dt_raw = _inproj(x2, sc1, sh1, norm_mix_w.reshape(1, d), w_packed, w_dt,
                                       rope_c, rope_s1, rope_s2, batch, seq)

    srcs = [(proj.reshape(batch, 1, seq, MAIN_W), COL_Q0 // GROUP_W), (qkv1, 0), (qkv2, 0)]
    o_list, st_list = [], []
    for g, (src, col0) in enumerate(srcs):
        o, st = _attention_group(src, g, batch, seq, col0)
        o_list.append(o)
        st_list.append(st)
    ssm = _ssd(proj, dt_raw, conv_w[0], conv_b[0], dt_bias[0], a_log[0], d_skip[0], ssm_norm_w[0], batch, seq)

    base, u2t, scores_t = _mix(
        o_list, st_list, ssm, proj, x2, g1, sc2, sh2, g2, norm_ffn_w.reshape(1, d),
        w_branch_attn[0].astype(BF16), w_branch_ssm[0].astype(BF16), w_out[0].astype(BF16),
        w_router[0].T, w_gate_s[0].astype(BF16), w_up_s[0].astype(BF16), w_down_s[0].astype(BF16), seq)

    idx, w_sel, member, counts = _route(scores_t, router_bias[0])

    rb = MOE_ROWS
    cnt = counts[:, 0].astype(I32)
    padded = (cnt + rb - 1) // rb * rb
    pad_end = jnp.cumsum(padded).astype(I32)
    pad_start = pad_end - padded
    n_rows = t * TOP_K + N_EXPERTS * rb

    dest_flat = _dest(member, idx, pad_start).T.reshape(t * TOP_K)
    xs = _dispatch(u2t, dest_flat, pad_end, padded, n_rows)
    ys = _experts(xs, pad_start // rb, padded // rb, w_gate_e[0], w_up_e[0], w_down_e[0])
    out = _combine(ys, dest_flat, w_sel.T.reshape(t * TOP_K), base, g2, norm_final_w.reshape(1, d), seq)
    return out.reshape(batch, seq, d)
```

```python
import functools
import math

import jax
import jax.numpy as jnp
from jax import lax
from jax.experimental import pallas as pl
from jax.experimental.pallas import tpu as pltpu

F32 = jnp.float32
BF16 = jnp.bfloat16
I32 = jnp.int32

LANE = 128
SUBLANE = 8
VMEM_LIMIT = 56 * 1024 * 1024

HEAD_DIM = 128
HEADS_PER_GROUP = 4
GROUP_W = HEADS_PER_GROUP * HEAD_DIM
ATTN_DILATIONS = (1, 4, 16)
ATTN_BLK = 128
ROPE_DIM = 32
ROPE_HALF = 16
ROPE_THETA = 500000.0
SSM_HEADS = 16
SSM_HEADDIM = 64
SSM_GROUPS = 4
SSM_STATE = 128
SSM_CONV = 4
SSM_CHUNK = 128
N_EXPERTS = 256
TOP_K = 8
N_EXPERT_GROUPS = 8
TOPK_GROUPS = 4
ROUTED_SCALE = 2.5
NORM_EPS = 1e-6
NEG = -1e30

COL_XBC, COL_GA, COL_GS, COL_Z, COL_Q0 = 0, 2048, 3072, 4096, 5120
MAIN_W = 6656
QKV_W = 3 * GROUP_W

INPROJ_TM, INPROJ_TN = 256, 512
MIX_TM, MIX_SPLIT = 512, 2
ROUTE_TM = 512
MOE_ROWS = 256
EXPERT_IN_SLOTS, EXPERT_OUT_SLOTS = 4, 3
DISPATCH_TM = 256
COMBINE_TM = 256
ROW_TILE = 8


def _cp(sem, vmem=VMEM_LIMIT):
    return pltpu.CompilerParams(dimension_semantics=sem, vmem_limit_bytes=vmem)


def _sigmoid(x):
    return 1.0 / (1.0 + jnp.exp(-x))


def _silu(x):
    return x * _sigmoid(x)


def _mod_kernel(c_ref, w_ref, b_ref, o_ref):
    cond = _silu(c_ref[...])
    o_ref[...] = jnp.dot(cond, w_ref[...], preferred_element_type=F32) + b_ref[...]


def _modulation(c, w_mod, b_mod):
    b, d = c.shape
    n = w_mod.shape[1]
    return pl.pallas_call(
        _mod_kernel,
        grid=(n // d,),
        in_specs=[pl.BlockSpec((b, d), lambda j: (0, 0)),
                  pl.BlockSpec((d, d), lambda j: (0, j)),
                  pl.BlockSpec((1, d), lambda j: (0, j))],
        out_specs=pl.BlockSpec((b, d), lambda j: (0, j)),
        out_shape=jax.ShapeDtypeStruct((b, n), F32),
        compiler_params=_cp(("arbitrary",)),
        name="modulation",
    )(c, w_mod, b_mod.reshape(1, n))


def _rope_kernel(pos_ref, inv_ref, c_ref, s1_ref, s2_ref):
    ang = pos_ref[...].astype(F32) * inv_ref[...]
    lane = lax.broadcasted_iota(I32, ang.shape, 1)
    cos = jnp.cos(ang)
    sin = jnp.sin(ang)
    c_ref[...] = jnp.where(lane < ROPE_DIM, cos, 1.0)
    s1_ref[...] = jnp.where(lane < ROPE_HALF, -sin, 0.0)
    s2_ref[...] = jnp.where((lane >= ROPE_HALF) & (lane < ROPE_DIM), sin, 0.0)


def _rope_tables(positions):
    t = positions.size
    tm = 2048
    inv_freq = ROPE_THETA ** (-jnp.arange(ROPE_HALF, dtype=F32) / ROPE_HALF)
    inv_row = jnp.concatenate([inv_freq, inv_freq, jnp.zeros((LANE - ROPE_DIM,), F32)]).reshape(1, LANE)
    spec = pl.BlockSpec((tm, LANE), lambda i: (i, 0))
    shp = jax.ShapeDtypeStruct((t, LANE), F32)
    return pl.pallas_call(
        _rope_kernel,
        grid=(t // tm,),
        in_specs=[pl.BlockSpec((tm, 1), lambda i: (i, 0)), pl.BlockSpec((1, LANE), lambda i: (0, 0))],
        out_specs=[spec, spec, spec],
        out_shape=[shp, shp, shp],
        compiler_params=_cp(("arbitrary",)),
        name="rope_tables",
    )(positions.reshape(t, 1), inv_row)


def _inproj_kernel(x_ref, sc_ref, sh_ref, nw_ref, w_ref, wdt_ref, c_ref, s1_ref, s2_ref,
                   main_ref, g1_ref, g2_ref, dt_ref, u_scr, rope_scr, uc_scr, *, tn, q_scale):
    tm = x_ref.shape[0]
    n_main = MAIN_W // tn
    n_qkv = QKV_W // tn

    x = x_ref[...]
    ms = jnp.mean(x * x, axis=-1, keepdims=True)
    y = x * lax.rsqrt(ms + NORM_EPS) * nw_ref[...]
    uf = y * (1.0 + sc_ref[...]) + sh_ref[...]
    u = uf.astype(BF16)
    dt_ref[...] = jnp.dot(u, wdt_ref[...], preferred_element_type=F32)
    n_chunk = uf.shape[1] // LANE
    for cc in range(n_chunk):
        uc_scr[cc] = uf[:, cc * LANE:(cc + 1) * LANE]
    for o, d in enumerate(ATTN_DILATIONS[1:]):
        rows = tm // d
        for r in range(d):
            for cc in range(n_chunk):
                u_scr[o, r * rows:(r + 1) * rows, cc * LANE:(cc + 1) * LANE] = (
                    uc_scr[cc, pl.ds(r, rows, stride=d), :].astype(BF16))
            for ti, tab in enumerate((c_ref, s1_ref, s2_ref)):
                rope_scr[o, ti, r * rows:(r + 1) * rows, :] = tab[pl.ds(r, rows, stride=d), :]

    for c in range(n_main + 2 * n_qkv):
        order = 0 if c < n_main else (1 if c < n_main + n_qkv else 2)
        jq = c - COL_Q0 // tn if order == 0 else (c - n_main - (order - 1) * n_qkv)
        lhs = u if order == 0 else u_scr[order - 1]
        acc = jnp.dot(lhs, w_ref[:, c * tn:(c + 1) * tn], preferred_element_type=F32)
        if jq in (0, 1):
            scale = q_scale if jq == 0 else 1.0
            tabs = (c_ref, s1_ref, s2_ref) if order == 0 else tuple(rope_scr.at[order - 1, ti] for ti in range(3))
            cs = tabs[0][...] * scale
            s1 = tabs[1][...] * scale
            s2 = tabs[2][...] * scale
            parts = []
            for h in range(tn // HEAD_DIM):
                a = acc[:, h * HEAD_DIM:(h + 1) * HEAD_DIM]
                parts.append(a * cs + pltpu.roll(a, LANE - ROPE_HALF, 1) * s1 + pltpu.roll(a, ROPE_HALF, 1) * s2)
            acc = jnp.concatenate(parts, axis=1)
        val = acc.astype(BF16)
        if order == 0:
            main_ref[:, c * tn:(c + 1) * tn] = val
        else:
            dst = g1_ref if order == 1 else g2_ref
            c0 = (c - n_main - (order - 1) * n_qkv) * tn
            dst[:, :, c0:c0 + tn] = val.reshape(dst.shape[0], dst.shape[1], tn)


def _inproj(x2, sc1, sh1, norm_w, w_packed, w_dt, rope_c, rope_s1, rope_s2, batch, seq):
    t, d = x2.shape
    tm, tn = INPROJ_TM, INPROJ_TN
    n = w_packed.shape[1]
    per_b = seq // tm
    d1, d2 = ATTN_DILATIONS[1], ATTN_DILATIONS[2]
    row = lambda i: (i, 0)
    modrow = lambda i: (i // per_b, 0, 0)
    const = lambda i: (0, 0)
    resident = pl.Buffered(1)
    qkv_map = lambda i: (i // per_b, 0, i % per_b, 0)
    return pl.pallas_call(
        functools.partial(_inproj_kernel, tn=tn, q_scale=1.0 / math.sqrt(HEAD_DIM)),
        grid=(t // tm,),
        in_specs=[pl.BlockSpec((tm, d), row),
                  pl.BlockSpec((None, 1, d), modrow),
                  pl.BlockSpec((None, 1, d), modrow),
                  pl.BlockSpec((1, d), const),
                  pl.BlockSpec((d, n), const, pipeline_mode=resident),
                  pl.BlockSpec((d, LANE), const),
                  pl.BlockSpec((tm, LANE), row),
                  pl.BlockSpec((tm, LANE), row),
                  pl.BlockSpec((tm, LANE), row)],
        out_specs=[pl.BlockSpec((tm, MAIN_W), row),
                   pl.BlockSpec((None, d1, tm // d1, QKV_W), qkv_map),
                   pl.BlockSpec((None, d2, tm // d2, QKV_W), qkv_map),
                   pl.BlockSpec((tm, LANE), row)],
        out_shape=[jax.ShapeDtypeStruct((t, MAIN_W), BF16),
                   jax.ShapeDtypeStruct((batch, d1, seq // d1, QKV_W), BF16),
                   jax.ShapeDtypeStruct((batch, d2, seq // d2, QKV_W), BF16),
                   jax.ShapeDtypeStruct((t, LANE), F32)],
        scratch_shapes=[pltpu.VMEM((2, tm, d), BF16), pltpu.VMEM((2, 3, tm, LANE), F32),
                        pltpu.VMEM((d // LANE, tm, LANE), F32)],
        compiler_params=_cp(("arbitrary",)),
        name="inproj",
    )(x2, sc1, sh1, norm_w, w_packed, w_dt, rope_c, rope_s1, rope_s2)


def _attn_kernel(q_ref, k_ref, v_ref, o_ref, st_ref, *, d, nb):
    blk = ATTN_BLK
    qi = lax.broadcasted_iota(I32, (blk, 2 * blk), 0)
    kj = lax.broadcasted_iota(I32, (blk, 2 * blk), 1)
    band = (kj >= qi) & (kj <= qi + blk)
    qi1 = lax.broadcasted_iota(I32, (blk, blk), 0)
    kj1 = lax.broadcasted_iota(I32, (blk, blk), 1)
    causal = kj1 <= qi1
    lane = kj1

    def block(r, q0, k0, nk, mask):
        rows = pl.ds(q0, blk) if d == 1 else pl.ds(q0 * d + r, blk, stride=d)
        hsl = [slice(h * HEAD_DIM, (h + 1) * HEAD_DIM) for h in range(HEADS_PER_GROUP)]
        s = jnp.concatenate(
            [lax.dot_general(q_ref[r, pl.ds(q0, blk), hs], k_ref[r, pl.ds(k0, nk), hs], (((1,), (1,)), ((), ())),
                             preferred_element_type=F32) for hs in hsl], axis=0)
        s = jnp.where(jnp.concatenate([mask] * HEADS_PER_GROUP, axis=0), s, NEG)
        m = jnp.max(s, axis=-1, keepdims=True)
        p = jnp.exp(s - m)
        l = jnp.sum(p, axis=-1, keepdims=True)
        pb = p.astype(BF16)
        st = jnp.zeros((blk, LANE), F32)
        for h, hs in enumerate(hsl):
            hr = slice(h * blk, (h + 1) * blk)
            o = jnp.dot(pb[hr], v_ref[r, pl.ds(k0, nk), hs], preferred_element_type=F32)
            o_ref[h, rows, :] = o / l[hr]
            st = jnp.where(lane == h, m[hr], st)
            st = jnp.where(lane == HEADS_PER_GROUP + h, l[hr], st)
        st_ref[rows, :] = st

    def per_residue(r, carry):
        block(r, 0, 0, blk, causal)
        if nb > 1:
            def body(n, c):
                q0 = pl.multiple_of(n * blk, blk)
                block(r, q0, pl.multiple_of(q0 - blk, blk), 2 * blk, band)
                return c
            lax.fori_loop(1, nb, body, 0)
        return carry

    if d == 1:
        per_residue(0, 0)
    else:
        lax.fori_loop(0, d, per_residue, 0)


def _attention_group(src, g, batch, seq, col0):
    d = ATTN_DILATIONS[g]
    n_sub = seq // d
    nb = n_sub // ATTN_BLK
    spec = lambda c: pl.BlockSpec((None, d, n_sub, GROUP_W), lambda b: (b, 0, 0, c))
    o, st = pl.pallas_call(
        functools.partial(_attn_kernel, d=d, nb=nb),
        grid=(batch,),
        in_specs=[spec(col0), spec(col0 + 1), spec(col0 + 2)],
        out_specs=[pl.BlockSpec((HEADS_PER_GROUP, seq, HEAD_DIM), lambda b: (0, b, 0)),
                   pl.BlockSpec((seq, LANE), lambda b: (b, 0))],
        out_shape=[jax.ShapeDtypeStruct((HEADS_PER_GROUP, batch * seq, HEAD_DIM), F32),
                   jax.ShapeDtypeStruct((batch * seq, LANE), F32)],
        compiler_params=_cp(("arbitrary",)),
        name=f"attn_d{d}",
    )(src, src, src)
    return o, st


def _ssd_kernel(xbc_ref, z_ref, dt_ref, cw_ref, cb_ref, dtb_ref, alog_ref, dsk_ref, nw_ref,
                out_ref, xwin, ystage, state):
    L = SSM_CHUNK
    inner = SSM_HEADS * SSM_HEADDIM
    gw = SSM_STATE
    c = pl.program_id(1)

    @pl.when(c == 0)
    def _():
        xwin[0:L, :] = jnp.zeros((L, xwin.shape[1]), BF16)
        state[...] = jnp.zeros(state.shape, F32)

    xcur = xbc_ref[...]
    xwin[L:2 * L, :] = xcur
    win = xwin[...]
    srow = lax.broadcasted_iota(I32, (L, 2 * L), 0)
    scol = lax.broadcasted_iota(I32, (L, 2 * L), 1)
    conv = cb_ref[...] + cw_ref[SSM_CONV - 1:SSM_CONV, :] * xcur.astype(F32)
    for s in range(1, SSM_CONV):
        shift_m = jnp.where(scol == srow + (L - s), 1.0, 0.0).astype(BF16)
        conv = conv + cw_ref[SSM_CONV - 1 - s:SSM_CONV - s, :] * jnp.dot(shift_m, win, preferred_element_type=F32)
    xwin[0:L, :] = xcur
    act = _silu(conv)
    xs = act[:, :inner]
    xs_b = xs.astype(BF16)

    lane = lax.broadcasted_iota(I32, (L, LANE), 1)
    row = lax.broadcasted_iota(I32, (L, LANE), 0)
    dtr = dt_ref[...] + dtb_ref[...]
    dt = jnp.maximum(dtr, 0.0) + jnp.log(1.0 + jnp.exp(-jnp.abs(dtr)))
    a_neg = jnp.where(lane < SSM_HEADS, -jnp.exp(alog_ref[...]), 0.0)
    a = dt * a_neg
    cs = a
    shift = 1
    while shift < L:
        cs = cs + jnp.where(row >= shift, pltpu.roll(cs, shift, 0), 0.0)
        shift *= 2
    cs_t = cs.T
    dt_t = dt.T
    tri = row >= lane
    half = lane < SSM_HEADDIM
    zero_b = jnp.zeros((L, LANE), BF16)

    for g in range(SSM_GROUPS):
        bg = act[:, inner + g * gw:inner + (g + 1) * gw]
        cg = act[:, inner + SSM_GROUPS * gw + g * gw:inner + SSM_GROUPS * gw + (g + 1) * gw]
        cg_b = cg.astype(BF16)
        cb = lax.dot_general(cg_b, bg.astype(BF16), (((1,), (1,)), ((), ())), preferred_element_type=F32)
        bg_t = bg.T
        for pair in range(2):
            h0 = g * 4 + pair * 2
            pidx = h0 // 2
            xpp = xs_b[:, pidx * LANE:(pidx + 1) * LANE]
            rhs = jnp.concatenate([jnp.where(half, xpp, zero_b), jnp.where(half, zero_b, xpp)], axis=0)
            dec, dst, eoff, cdec = [], [], [], []
            for h in (h0, h0 + 1):
                cs_col = cs[:, h:h + 1]
                cs_row = cs_t[h:h + 1, :]
                dt_row = dt_t[h:h + 1, :]
                dec.append(cb * (jnp.exp(jnp.where(tri, cs_col - cs_row, NEG)) * dt_row))
                cs_last = cs_row[:, L - 1:L]
                dst.append(bg_t * (jnp.exp(cs_last - cs_row) * dt_row))
                eoff.append(jnp.exp(cs_col))
                cdec.append(jnp.exp(cs_last))
            y_diag = jnp.dot(jnp.concatenate(dec, axis=1).astype(BF16), rhs, preferred_element_type=F32)
            st_new = jnp.dot(jnp.concatenate(dst, axis=1).astype(BF16), rhs, preferred_element_type=F32)
            prev = state[pidx]
            y_off = jnp.dot(cg_b, prev.astype(BF16), preferred_element_type=F32)
            y_off = y_off * jnp.where(half, eoff[0], eoff[1])
            state[pidx] = prev * jnp.where(half, cdec[0], cdec[1]) + st_new
            y = y_diag + y_off + dsk_ref[:, pidx * LANE:(pidx + 1) * LANE] * xs[:, pidx * LANE:(pidx + 1) * LANE]
            out_pair = y * _silu(z_ref[:, pidx * LANE:(pidx + 1) * LANE].astype(F32))
            ystage[:, pidx * LANE:(pidx + 1) * LANE] = out_pair

    gsz = inner // SSM_GROUPS
    for g in range(SSM_GROUPS):
        yg = ystage[:, g * gsz:(g + 1) * gsz]
        ms = jnp.mean(yg * yg, axis=-1, keepdims=True)
        out_ref[:, g * gsz:(g + 1) * gsz] = (yg * lax.rsqrt(ms + NORM_EPS) * nw_ref[:, g * gsz:(g + 1) * gsz]).astype(BF16)


def _ssd(proj, dt_raw, conv_w, conv_b, dt_bias, a_log, d_skip, ssm_norm_w, batch, seq):
    t = batch * seq
    L = SSM_CHUNK
    nc = seq // L
    inner = SSM_HEADS * SSM_HEADDIM
    cdim = conv_w.shape[1]
    pad16 = lambda v: jnp.pad(v.astype(F32), (0, LANE - SSM_HEADS)).reshape(1, LANE)
    dsk = jnp.repeat(d_skip.astype(F32), SSM_HEADDIM).reshape(1, inner)
    rowc = lambda b, c: (b * nc + c, 0)
    const = lambda b, c: (0, 0)
    return pl.pallas_call(
        _ssd_kernel,
        grid=(batch, nc),
        in_specs=[pl.BlockSpec((L, cdim), lambda b, c: (b * nc + c, COL_XBC // cdim)),
                  pl.BlockSpec((L, inner), lambda b, c: (b * nc + c, COL_Z // inner)),
                  pl.BlockSpec((L, LANE), rowc),
                  pl.BlockSpec((SSM_CONV, cdim), const),
                  pl.BlockSpec((1, cdim), const),
                  pl.BlockSpec((1, LANE), const),
                  pl.BlockSpec((1, LANE), const),
                  pl.BlockSpec((1, inner), const),
                  pl.BlockSpec((1, inner), const)],
        out_specs=pl.BlockSpec((L, inner), rowc),
        out_shape=jax.ShapeDtypeStruct((t, inner), BF16),
        scratch_shapes=[pltpu.VMEM((2 * L, cdim), BF16), pltpu.VMEM((L, inner), F32),
                        pltpu.VMEM((SSM_HEADS // 2, SSM_STATE, 2 * SSM_HEADDIM), F32)],
        compiler_params=_cp(("arbitrary", "arbitrary")),
        name="ssd",
    )(proj, proj, dt_raw, conv_w.astype(F32), conv_b.reshape(1, cdim).astype(F32), pad16(dt_bias), pad16(a_log),
      dsk, ssm_norm_w.reshape(1, inner).astype(F32))


def _mix_kernel(o0_ref, o1_ref, o2_ref, s0_ref, s1_ref, s2_ref, ssm_ref, ga_ref, gs_ref, x_ref,
                g1_ref, sc2_ref, sh2_ref, g2_ref, nw_ref, wba_ref, wbs_ref, wo_ref, wrt_ref,
                wgs_ref, wus_ref, wds_ref, base_ref, u2t_ref, sct_ref):
    o_refs = (o0_ref, o1_ref, o2_ref)
    s_refs = (s0_ref, s1_ref, s2_ref)
    tm = x_ref.shape[0]
    sub = tm // MIX_SPLIT
    for part in range(MIX_SPLIT):
        rs = slice(part * sub, (part + 1) * sub)
        heads = []
        for h in range(HEADS_PER_GROUP):
            ms = [s[rs, h:h + 1] for s in s_refs]
            ls = [s[rs, HEADS_PER_GROUP + h:HEADS_PER_GROUP + h + 1] for s in s_refs]
            mx = jnp.maximum(jnp.maximum(ms[0], ms[1]), ms[2])
            wts = [l * jnp.exp(m - mx) for m, l in zip(ms, ls)]
            num = wts[0] * o_refs[0][h, rs, :] + wts[1] * o_refs[1][h, rs, :] + wts[2] * o_refs[2][h, rs, :]
            heads.append((num / (wts[0] + wts[1] + wts[2])).astype(BF16))
        attn = jnp.concatenate(heads, axis=1)
        ya = jnp.dot(attn, wba_ref[...], preferred_element_type=F32)
        ys = jnp.dot(ssm_ref[rs, :], wbs_ref[...], preferred_element_type=F32)
        merged = _sigmoid(ga_ref[rs, :].astype(F32)) * ya + _sigmoid(gs_ref[rs, :].astype(F32)) * ys
        mix = jnp.dot(merged.astype(BF16), wo_ref[...], preferred_element_type=F32)
        h1 = x_ref[rs, :] + g1_ref[...] * mix
        ms2 = jnp.mean(h1 * h1, axis=-1, keepdims=True)
        u2 = h1 * lax.rsqrt(ms2 + NORM_EPS) * nw_ref[...] * (1.0 + sc2_ref[...]) + sh2_ref[...]
        for cc in range(ROW_TILE):
            u2t_ref[pl.ds(part * sub * ROW_TILE + cc, sub, stride=ROW_TILE), :] = u2[:, cc * LANE:(cc + 1) * LANE]
        u2b = u2.astype(BF16)
        u2lo = (u2 - u2b.astype(F32)).astype(BF16)
        nt = (((1,), (1,)), ((), ()))
        logits_t = (lax.dot_general(wrt_ref[0], u2b, nt, preferred_element_type=F32)
                    + lax.dot_general(wrt_ref[0], u2lo, nt, preferred_element_type=F32)
                    + lax.dot_general(wrt_ref[1], u2b, nt, preferred_element_type=F32))
        sct_ref[:, rs] = _sigmoid(logits_t)
        hs_ = (_silu(jnp.dot(u2b, wgs_ref[...], preferred_element_type=F32))
               * jnp.dot(u2b, wus_ref[...], preferred_element_type=F32))
        shared = jnp.dot(hs_.astype(BF16), wds_ref[...], preferred_element_type=F32)
        base_ref[rs, :] = h1 + g2_ref[...] * shared


def _mix(o_list, st_list, ssm, proj, x2, g1, sc2, sh2, g2, norm_w, w_ba, w_bs, w_o, w_rt, w_gs, w_us, w_ds, seq):
    t, d = x2.shape
    tm = MIX_TM
    per_b = seq // tm
    row = lambda i: (i, 0)
    modrow = lambda i: (i // per_b, 0, 0)
    const = lambda i: (0, 0)
    full = lambda a: pl.BlockSpec(a.shape, lambda i: (0,) * a.ndim, pipeline_mode=pl.Buffered(1))
    mod = pl.BlockSpec((None, 1, d), modrow)
    return pl.pallas_call(
        _mix_kernel,
        grid=(t // tm,),
        in_specs=[pl.BlockSpec((HEADS_PER_GROUP, tm, HEAD_DIM), lambda i: (0, i, 0))] * 3
        + [pl.BlockSpec((tm, LANE), row)] * 3 + [
            pl.BlockSpec((tm, d), row),
            pl.BlockSpec((tm, d), lambda i: (i, COL_GA // d)),
            pl.BlockSpec((tm, d), lambda i: (i, COL_GS // d)),
            pl.BlockSpec((tm, d), row),
            mod, mod, mod, mod, full(norm_w), full(w_ba), full(w_bs), full(w_o), full(w_rt),
            full(w_gs), full(w_us), full(w_ds)],
        out_specs=[pl.BlockSpec((tm, d), row), pl.BlockSpec((tm * ROW_TILE, LANE), row),
                   pl.BlockSpec((N_EXPERTS, tm), lambda i: (0, i))],
        out_shape=[jax.ShapeDtypeStruct((t, d), F32), jax.ShapeDtypeStruct((t * ROW_TILE, LANE), F32),
                   jax.ShapeDtypeStruct((N_EXPERTS, t), F32)],
        compiler_params=_cp(("arbitrary",)),
        name="mix",
    )(*o_list, *st_list, ssm, proj, proj, x2, g1, sc2, sh2, g2, norm_w, w_ba, w_bs, w_o, w_rt, w_gs, w_us, w_ds)


def _route_kernel(sct_ref, bias_ref, idx_ref, w_ref, mem_ref, cnt_ref, run):
    i = pl.program_id(0)
    tm = sct_ref.shape[1]
    per_g = N_EXPERTS // N_EXPERT_GROUPS

    @pl.when(i == 0)
    def _():
        run[...] = jnp.zeros(run.shape, F32)

    s = sct_ref[...]
    biased = s + bias_ref[...]
    io_g = lax.broadcasted_iota(I32, (per_g, tm), 0).astype(F32)
    gscore = []
    for g in range(N_EXPERT_GROUPS):
        bgp = biased[g * per_g:(g + 1) * per_g, :]
        m1 = jnp.max(bgp, axis=0, keepdims=True)
        first = jnp.min(jnp.where(bgp == m1, io_g, float(per_g)), axis=0, keepdims=True)
        m2 = jnp.max(jnp.where(io_g == first, NEG, bgp), axis=0, keepdims=True)
        gscore.append(m1 + m2)
    gs = jnp.concatenate(gscore, axis=0)
    io8 = lax.broadcasted_iota(I32, (N_EXPERT_GROUPS, tm), 0).astype(F32)
    gsel = jnp.zeros((N_EXPERT_GROUPS, tm), F32)
    cur = gs
    for _ in range(TOPK_GROUPS):
        mx = jnp.max(cur, axis=0, keepdims=True)
        fi = jnp.min(jnp.where(cur == mx, io8, float(N_EXPERT_GROUPS)), axis=0, keepdims=True)
        hit = io8 == fi
        gsel = jnp.where(hit, 1.0, gsel)
        cur = jnp.where(hit, NEG, cur)
    masked = jnp.concatenate(
        [jnp.where(gsel[g:g + 1, :] > 0.5, biased[g * per_g:(g + 1) * per_g, :], NEG) for g in range(N_EXPERT_GROUPS)],
        axis=0)
    io_e = lax.broadcasted_iota(I32, (N_EXPERTS, tm), 0).astype(F32)
    member = jnp.zeros((N_EXPERTS, tm), F32)
    idxs, ws = [], []
    for _ in range(TOP_K):
        mx = jnp.max(masked, axis=0, keepdims=True)
        fi = jnp.min(jnp.where(masked == mx, io_e, float(N_EXPERTS)), axis=0, keepdims=True)
        hit = io_e == fi
        idxs.append(fi)
        ws.append(jnp.sum(jnp.where(hit, s, 0.0), axis=0, keepdims=True))
        member = jnp.where(hit, 1.0, member)
        masked = jnp.where(hit, NEG, masked)
    wsum = ws[0]
    for k in range(1, TOP_K):
        wsum = wsum + ws[k]
    idx_ref[...] = jnp.concatenate(idxs, axis=0).astype(I32)
    w_ref[...] = jnp.concatenate([w / wsum * ROUTED_SCALE for w in ws], axis=0)
    mem_ref[...] = member.astype(BF16)
    new_run = run[...] + jnp.sum(member, axis=1, keepdims=True)
    run[...] = new_run
    cnt_ref[...] = new_run


def _route(scores_t, router_bias):
    e, t = scores_t.shape
    tm = ROUTE_TM
    tok = pl.BlockSpec((TOP_K, tm), lambda i: (0, i))
    return pl.pallas_call(
        _route_kernel,
        grid=(t // tm,),
        in_specs=[pl.BlockSpec((e, tm), lambda i: (0, i)), pl.BlockSpec((e, 1), lambda i: (0, 0))],
        out_specs=[tok, tok, pl.BlockSpec((e, tm), lambda i: (0, i)), pl.BlockSpec((e, LANE), lambda i: (0, 0))],
        out_shape=[jax.ShapeDtypeStruct((TOP_K, t), I32), jax.ShapeDtypeStruct((TOP_K, t), F32),
                   jax.ShapeDtypeStruct((e, t), BF16), jax.ShapeDtypeStruct((e, LANE), F32)],
        scratch_shapes=[pltpu.VMEM((e, LANE), F32)],
        compiler_params=_cp(("arbitrary",)),
        name="route",
    )(scores_t, router_bias.reshape(e, 1).astype(F32))


def _dest_kernel(mem_ref, idx_ref, start_ref, dest_ref, run):
    i = pl.program_id(0)
    e, tm = mem_ref.shape

    @pl.when(i == 0)
    def _():
        run[...] = jnp.broadcast_to(start_ref[...], run.shape)

    member = mem_ref[...]
    tr = lax.broadcasted_iota(I32, (tm, tm), 0)
    tc = lax.broadcasted_iota(I32, (tm, tm), 1)
    upper = jnp.where(tr < tc, 1.0, 0.0).astype(BF16)
    rank_full = jnp.dot(member, upper, preferred_element_type=F32) + run[:, 0:1]
    io_e = lax.broadcasted_iota(I32, (e, tm), 0)
    idx = idx_ref[...]
    rows = [jnp.sum(jnp.where(io_e == idx[k:k + 1, :], rank_full, 0.0), axis=0, keepdims=True) for k in range(TOP_K)]
    dest_ref[...] = jnp.concatenate(rows, axis=0).astype(I32)
    run[...] = run[...] + jnp.sum(member.astype(F32), axis=1, keepdims=True)


def _dest(member, idx, pad_start):
    e, t = member.shape
    tm = ROUTE_TM
    return pl.pallas_call(
        _dest_kernel,
        grid=(t // tm,),
        in_specs=[pl.BlockSpec((e, tm), lambda i: (0, i)), pl.BlockSpec((TOP_K, tm), lambda i: (0, i)),
                  pl.BlockSpec((e, 1), lambda i: (0, 0))],
        out_specs=pl.BlockSpec((TOP_K, tm), lambda i: (0, i)),
        out_shape=jax.ShapeDtypeStruct((TOP_K, t), I32),
        scratch_shapes=[pltpu.VMEM((e, LANE), F32)],
        compiler_params=_cp(("arbitrary",)),
        name="dest",
    )(member, idx, pad_start.astype(F32).reshape(e, 1))


def _dispatch_kernel(pend_ref, npad_ref, dest_ref, u_ref, xs_ref, zero_scr, sem, zsem):
    i = pl.program_id(0)
    rt = ROW_TILE
    tm = u_ref.shape[0] // rt
    zrows = zero_scr.shape[0]

    @pl.when(i == 0)
    def _():
        zero_scr[...] = jnp.zeros(zero_scr.shape, zero_scr.dtype)

        def zcopy(e):
            start = pl.multiple_of(pend_ref[e] * rt - zrows, zrows)
            return pltpu.make_async_copy(zero_scr, xs_ref.at[pl.ds(start, zrows)], zsem)

        def zstart(e, c):
            @pl.when(npad_ref[e] > 0)
            def _():
                zcopy(e).start()
            return c

        def zwait(e, c):
            @pl.when(npad_ref[e] > 0)
            def _():
                zcopy(e).wait()
            return c

        lax.fori_loop(0, N_EXPERTS, zstart, 0)
        lax.fori_loop(0, N_EXPERTS, zwait, 0)

    def row_copy(t, k):
        src = u_ref.at[pl.ds(pl.multiple_of(t * rt, rt), rt)]
        dst = xs_ref.at[pl.ds(pl.multiple_of(dest_ref[t * TOP_K + k] * rt, rt), rt)]
        return pltpu.make_async_copy(src, dst, sem)

    def start(t, c):
        for k in range(TOP_K):
            row_copy(t, k).start(priority=k % 2)
        return c

    def wait(t, c):
        for k in range(TOP_K):
            row_copy(t, k).wait()
        return c

    lax.fori_loop(0, tm, start, 0)
    lax.fori_loop(0, tm, wait, 0)


def _dispatch(u2t, dest_flat, pad_end, padded, n_rows):
    rt = ROW_TILE
    t = u2t.shape[0] // rt
    tm = DISPATCH_TM
    gs = pltpu.PrefetchScalarGridSpec(
        num_scalar_prefetch=2,
        grid=(t // tm,),
        in_specs=[pl.BlockSpec((tm * TOP_K,), lambda i, pe, npd: (i,), memory_space=pltpu.SMEM),
                  pl.BlockSpec((tm * rt, LANE), lambda i, pe, npd: (i, 0))],
        out_specs=pl.BlockSpec(memory_space=pl.ANY),
        scratch_shapes=[pltpu.VMEM((MOE_ROWS * rt, LANE), F32), pltpu.SemaphoreType.DMA, pltpu.SemaphoreType.DMA],
    )
    return pl.pallas_call(
        _dispatch_kernel,
        grid_spec=gs,
        out_shape=jax.ShapeDtypeStruct((n_rows * rt, LANE), F32),
        compiler_params=_cp(("arbitrary",)),
        name="dispatch",
    )(pad_end, padded, dest_flat, u2t)


def _expert_kernel(bstart_ref, nblk_ref, wg_ref, wu_ref, wd_ref, xs_ref, ys_ref,
                   wg_b, wu_b, wd_b, xbuf, ybuf, in_sem, out_sem):
    e = pl.program_id(0)
    last = pl.num_programs(0) - 1
    rt = ROW_TILE
    n_in, rb = xbuf.shape[0], xbuf.shape[1] // rt
    n_out = ybuf.shape[0]
    n = nblk_ref[e]
    g0 = bstart_ref[e]
    total = bstart_ref[last] + nblk_ref[last]

    def block_rows(g):
        return pl.ds(pl.multiple_of(g * (rb * rt), rb * rt), rb * rt)

    def in_copy(g):
        return pltpu.make_async_copy(xs_ref.at[block_rows(g)], xbuf.at[g % n_in], in_sem.at[g % n_in])

    def out_copy(g):
        return pltpu.make_async_copy(ybuf.at[g % n_out], ys_ref.at[block_rows(g)], out_sem.at[g % n_out])

    @pl.when(e == 0)
    def _():
        for g in range(n_in - 1):
            @pl.when(g < total)
            def _():
                in_copy(g).start()

    wg_b[...] = wg_ref[...].astype(BF16)
    wu_b[...] = wu_ref[...].astype(BF16)
    wd_b[...] = wd_ref[...].astype(BF16)

    def body(b, carry):
        g = g0 + b
        in_copy(g).wait()

        @pl.when(g + n_in - 1 < total)
        def _():
            in_copy(g + n_in - 1).start()

        @pl.when(g >= n_out)
        def _():
            out_copy(g - n_out).wait()

        xb = jnp.concatenate([xbuf[g % n_in, pl.ds(cc, rb, stride=rt), :].astype(BF16) for cc in range(rt)], axis=1)
        gate = jnp.dot(xb, wg_b[...], preferred_element_type=F32)
        up = jnp.dot(xb, wu_b[...], preferred_element_type=F32)
        h = (_silu(gate) * up).astype(BF16)
        y = jnp.dot(h, wd_b[...], preferred_element_type=F32)
        for cc in range(rt):
            ybuf[g % n_out, pl.ds(cc, rb, stride=rt), :] = y[:, cc * LANE:(cc + 1) * LANE]
        out_copy(g).start()
        return carry

    lax.fori_loop(0, n, body, 0)

    @pl.when(e == last)
    def _():
        for j in range(n_out, 0, -1):
            @pl.when(total >= j)
            def _():
                out_copy(total - j).wait()


def _experts(xs, blk_start, nblk, w_gate_e, w_up_e, w_down_e):
    ne, d, ff = w_gate_e.shape
    rb = MOE_ROWS * ROW_TILE
    gs = pltpu.PrefetchScalarGridSpec(
        num_scalar_prefetch=2,
        grid=(ne,),
        in_specs=[pl.BlockSpec((None, d, ff), lambda e, ps, nb: (e, 0, 0)),
                  pl.BlockSpec((None, d, ff), lambda e, ps, nb: (e, 0, 0)),
                  pl.BlockSpec((None, ff, d), lambda e, ps, nb: (e, 0, 0)),
                  pl.BlockSpec(memory_space=pl.ANY)],
        out_specs=pl.BlockSpec(memory_space=pl.ANY),
        scratch_shapes=[pltpu.VMEM((d, ff), BF16), pltpu.VMEM((d, ff), BF16), pltpu.VMEM((ff, d), BF16),
                        pltpu.VMEM((EXPERT_IN_SLOTS, rb, LANE), F32), pltpu.VMEM((EXPERT_OUT_SLOTS, rb, LANE), F32),
                        pltpu.SemaphoreType.DMA((EXPERT_IN_SLOTS,)), pltpu.SemaphoreType.DMA((EXPERT_OUT_SLOTS,))],
    )
    return pl.pallas_call(
        _expert_kernel,
        grid_spec=gs,
        out_shape=jax.ShapeDtypeStruct(xs.shape, F32),
        compiler_params=_cp(("arbitrary",)),
        name="experts",
    )(blk_start, nblk, w_gate_e, w_up_e, w_down_e, xs)


def _combine_kernel(dest_ref, wt_ref, base_ref, g2_ref, nfw_ref, ys_ref, out_ref, buf, sem):
    tm = base_ref.shape[0]
    rt = ROW_TILE

    def row_copy(t, k):
        src = ys_ref.at[pl.ds(pl.multiple_of(dest_ref[t * TOP_K + k] * rt, rt), rt)]
        return pltpu.make_async_copy(src, buf.at[k, pl.ds(pl.multiple_of(t * rt, rt), rt)], sem)

    def start(t, c):
        for k in range(TOP_K):
            row_copy(t, k).start(priority=k % 2)
        return c

    def wait(t, c):
        for k in range(TOP_K):
            row_copy(t, k).wait()
        return c

    lax.fori_loop(0, tm, start, 0)
    lax.fori_loop(0, tm, wait, 0)
    def weigh(t, c):
        rows = pl.ds(pl.multiple_of(t * rt, rt), rt)
        acc = buf[0, rows, :] * wt_ref[t * TOP_K]
        for k in range(1, TOP_K):
            acc = acc + buf[k, rows, :] * wt_ref[t * TOP_K + k]
        buf[0, rows, :] = acc
        return c

    lax.fori_loop(0, tm, weigh, 0, unroll=4)
    ssq = jnp.zeros((tm, 1), F32)
    for cc in range(rt):
        cs = slice(cc * LANE, (cc + 1) * LANE)
        h2 = base_ref[:, cs] + g2_ref[:, cs] * buf[0, pl.ds(cc, tm, stride=rt), :]
        out_ref[:, cs] = h2
        ssq = ssq + jnp.sum(h2 * h2, axis=-1, keepdims=True)
    inv = lax.rsqrt(ssq / (rt * LANE) + NORM_EPS)
    out_ref[...] = out_ref[...] * inv * nfw_ref[...]


def _combine(ys, dest_flat, w_tok, base, g2, norm_final_w, seq):
    t, d = base.shape
    tm = COMBINE_TM
    per_b = seq // tm
    gs = pltpu.PrefetchScalarGridSpec(
        num_scalar_prefetch=0,
        grid=(t // tm,),
        in_specs=[pl.BlockSpec((tm * TOP_K,), lambda i: (i,), memory_space=pltpu.SMEM),
                  pl.BlockSpec((tm * TOP_K,), lambda i: (i,), memory_space=pltpu.SMEM),
                  pl.BlockSpec((tm, d), lambda i: (i, 0)),
                  pl.BlockSpec((None, 1, d), lambda i: (i // per_b, 0, 0)),
                  pl.BlockSpec((1, d), lambda i: (0, 0)),
                  pl.BlockSpec(memory_space=pl.ANY)],
        out_specs=pl.BlockSpec((tm, d), lambda i: (i, 0)),
        scratch_shapes=[pltpu.VMEM((TOP_K, tm * ROW_TILE, LANE), F32), pltpu.SemaphoreType.DMA],
    )
    return pl.pallas_call(
        _combine_kernel,
        grid_spec=gs,
        out_shape=jax.ShapeDtypeStruct((t, d), F32),
        compiler_params=_cp(("arbitrary",)),
        name="combine",
    )(dest_flat, w_tok, base, g2, norm_final_w, ys)


def kernel(x, c, positions, w_mod, b_mod, norm_mix_w, norm_ffn_w, w_in, conv_w, conv_b, dt_bias, a_log, d_skip,
           ssm_norm_w, w_branch_attn, w_branch_ssm, w_out, w_router, router_bias, w_gate_e, w_up_e, w_down_e,
           w_gate_s, w_up_s, w_down_s, norm_final_w):
    batch, seq, d = x.shape
    t = batch * seq
    assert w_mod.shape[0] == 1, "one layer"
    assert d == ROW_TILE * LANE and seq % INPROJ_TM == 0 and INPROJ_TM % (ATTN_DILATIONS[-1] * 16) == 0

    mod = _modulation(c, w_mod[0], b_mod[0])
    sh1, sc1, g1, sh2, sc2, g2 = [m.reshape(batch, 1, d) for m in jnp.split(mod, 6, axis=-1)]
    rope_c, rope_s1, rope_s2 = _rope_tables(positions)

    wi = w_in[0]
    q_dim = 3 * GROUP_W
    o_z = 3 * q_dim
    o_xbc = o_z + d
    o_dt = o_xbc + conv_w.shape[2]
    o_g = o_dt + SSM_HEADS
    qkv = lambda g: [wi[:, s * q_dim + g * GROUP_W:s * q_dim + (g + 1) * GROUP_W] for s in range(3)]
    w_packed = jnp.concatenate([wi[:, o_xbc:o_dt], wi[:, o_g:], wi[:, o_z:o_xbc]] + qkv(0) + qkv(1) + qkv(2),
                               axis=1).astype(BF16)
    assert w_packed.shape[1] == MAIN_W + 2 * QKV_W
    w_dt = jnp.pad(wi[:, o_dt:o_g], ((0, 0), (0, LANE - SSM_HEADS))).astype(BF16)

    x2 = x.reshape(t, d)
    proj, qkv1, qkv2, dt_raw = _inproj(x2, sc1, sh1, norm_mix_w.reshape(1, d), w_packed, w_dt,
                                       rope_c, rope_s1, rope_s2, batch, seq)

    srcs = [(proj.reshape(batch, 1, seq, MAIN_W), COL_Q0 // GROUP_W), (qkv1, 0), (qkv2, 0)]
    o_list, st_list = [], []
    for g, (src, col0) in enumerate(srcs):
        o, st = _attention_group(src, g, batch, seq, col0)
        o_list.append(o)
        st_list.append(st)
    ssm = _ssd(proj, dt_raw, conv_w[0], conv_b[0], dt_bias[0], a_log[0], d_skip[0], ssm_norm_w[0], batch, seq)

    wr_t = w_router[0].T
    wr_hi = wr_t.astype(BF16)
    w_rt = jnp.stack([wr_hi, (wr_t - wr_hi.astype(F32)).astype(BF16)])
    base, u2t, scores_t = _mix(
        o_list, st_list, ssm, proj, x2, g1, sc2, sh2, g2, norm_ffn_w.reshape(1, d),
        w_branch_attn[0].astype(BF16), w_branch_ssm[0].astype(BF16), w_out[0].astype(BF16),
        w_rt, w_gate_s[0].astype(BF16), w_up_s[0].astype(BF16), w_down_s[0].astype(BF16), seq)

    idx, w_sel, member, counts = _route(scores_t, router_bias[0])

    rb = MOE_ROWS
    cnt = counts[:, 0].astype(I32)
    padded = (cnt + rb - 1) // rb * rb
    pad_end = jnp.cumsum(padded).astype(I32)
    pad_start = pad_end - padded
    n_rows = t * TOP_K + N_EXPERTS * rb

    dest_flat = _dest(member, idx, pad_start).T.reshape(t * TOP_K)
    xs = _dispatch(u2t, dest_flat, pad_end, padded, n_rows)
    ys = _experts(xs, pad_start // rb, padded // rb, w_gate_e[0], w_up_e[0], w_down_e[0])
    out = _combine(ys, dest_flat, w_sel.T.reshape(t * TOP_K), base, g2, norm_final_w.reshape(1, d), seq)
    return out.reshape(batch, seq, d)
```

```python
import functools
import math

import jax
import jax.numpy as jnp
from jax import lax
from jax.experimental import pallas as pl
from jax.experimental.pallas import tpu as pltpu

F32 = jnp.float32
BF16 = jnp.bfloat16
I32 = jnp.int32

LANE = 128
SUBLANE = 8
VMEM_LIMIT = 56 * 1024 * 1024

HEAD_DIM = 128
HEADS_PER_GROUP = 4
GROUP_W = HEADS_PER_GROUP * HEAD_DIM
ATTN_DILATIONS = (1, 4, 16)
ATTN_BLK = 128
ROPE_DIM = 32
ROPE_HALF = 16
ROPE_THETA = 500000.0
SSM_HEADS = 16
SSM_HEADDIM = 64
SSM_GROUPS = 4
SSM_STATE = 128
SSM_CONV = 4
SSM_CHUNK = 128
N_EXPERTS = 256
TOP_K = 8
N_EXPERT_GROUPS = 8
TOPK_GROUPS = 4
ROUTED_SCALE = 2.5
NORM_EPS = 1e-6
NEG = -1e30

COL_XBC, COL_GA, COL_GS, COL_Z, COL_Q0 = 0, 2048, 3072, 4096, 5120
MAIN_W = 6656
QKV_W = 3 * GROUP_W

INPROJ_TM, INPROJ_TN = 256, 512
MIX_TM, MIX_SPLIT = 512, 2
ROUTE_TM = 512
MOE_ROWS = 256
EXPERT_IN_SLOTS, EXPERT_OUT_SLOTS = 4, 3
DISPATCH_TM = 256
COMBINE_TM = 256
ROW_TILE = 8


def _cp(sem, vmem=VMEM_LIMIT):
    return pltpu.CompilerParams(dimension_semantics=sem, vmem_limit_bytes=vmem)


def _sigmoid(x):
    return 1.0 / (1.0 + jnp.exp(-x))


def _silu(x):
    return x * _sigmoid(x)


def _mod_kernel(c_ref, w_ref, b_ref, o_ref):
    cond = _silu(c_ref[...])
    o_ref[...] = jnp.dot(cond, w_ref[...], preferred_element_type=F32) + b_ref[...]


def _modulation(c, w_mod, b_mod):
    b, d = c.shape
    n = w_mod.shape[1]
    return pl.pallas_call(
        _mod_kernel,
        grid=(n // d,),
        in_specs=[pl.BlockSpec((b, d), lambda j: (0, 0)),
                  pl.BlockSpec((d, d), lambda j: (0, j)),
                  pl.BlockSpec((1, d), lambda j: (0, j))],
        out_specs=pl.BlockSpec((b, d), lambda j: (0, j)),
        out_shape=jax.ShapeDtypeStruct((b, n), F32),
        compiler_params=_cp(("arbitrary",)),
        name="modulation",
    )(c, w_mod, b_mod.reshape(1, n))


def _rope_kernel(pos_ref, inv_ref, c_ref, s1_ref, s2_ref):
    ang = pos_ref[...].astype(F32) * inv_ref[...]
    lane = lax.broadcasted_iota(I32, ang.shape, 1)
    cos = jnp.cos(ang)
    sin = jnp.sin(ang)
    c_ref[...] = jnp.where(lane < ROPE_DIM, cos, 1.0)
    s1_ref[...] = jnp.where(lane < ROPE_HALF, -sin, 0.0)
    s2_ref[...] = jnp.where((lane >= ROPE_HALF) & (lane < ROPE_DIM), sin, 0.0)


def _rope_tables(positions):
    t = positions.size
    tm = 2048
    inv_freq = ROPE_THETA ** (-jnp.arange(ROPE_HALF, dtype=F32) / ROPE_HALF)
    inv_row = jnp.concatenate([inv_freq, inv_freq, jnp.zeros((LANE - ROPE_DIM,), F32)]).reshape(1, LANE)
    spec = pl.BlockSpec((tm, LANE), lambda i: (i, 0))
    shp = jax.ShapeDtypeStruct((t, LANE), F32)
    return pl.pallas_call(
        _rope_kernel,
        grid=(t // tm,),
        in_specs=[pl.BlockSpec((tm, 1), lambda i: (i, 0)), pl.BlockSpec((1, LANE), lambda i: (0, 0))],
        out_specs=[spec, spec, spec],
        out_shape=[shp, shp, shp],
        compiler_params=_cp(("arbitrary",)),
        name="rope_tables",
    )(positions.reshape(t, 1), inv_row)


def _inproj_kernel(x_ref, sc_ref, sh_ref, nw_ref, w_ref, wdt_ref, c_ref, s1_ref, s2_ref,
                   main_ref, g1_ref, g2_ref, dt_ref, u_scr, rope_scr, uc_scr, *, tn, q_scale):
    tm = x_ref.shape[0]
    n_main = MAIN_W // tn
    n_qkv = QKV_W // tn

    x = x_ref[...]
    ms = jnp.mean(x * x, axis=-1, keepdims=True)
    y = x * lax.rsqrt(ms + NORM_EPS) * nw_ref[...]
    uf = y * (1.0 + sc_ref[...]) + sh_ref[...]
    u = uf.astype(BF16)
    dt_ref[...] = jnp.dot(u, wdt_ref[...], preferred_element_type=F32)
    n_chunk = uf.shape[1] // LANE
    for cc in range(n_chunk):
        uc_scr[cc] = uf[:, cc * LANE:(cc + 1) * LANE]
    for o, d in enumerate(ATTN_DILATIONS[1:]):
        rows = tm // d
        for r in range(d):
            for cc in range(n_chunk):
                u_scr[o, r * rows:(r + 1) * rows, cc * LANE:(cc + 1) * LANE] = (
                    uc_scr[cc, pl.ds(r, rows, stride=d), :].astype(BF16))
            for ti, tab in enumerate((c_ref, s1_ref, s2_ref)):
                rope_scr[o, ti, r * rows:(r + 1) * rows, :] = tab[pl.ds(r, rows, stride=d), :]

    for c in range(n_main + 2 * n_qkv):
        order = 0 if c < n_main else (1 if c < n_main + n_qkv else 2)
        jq = c - COL_Q0 // tn if order == 0 else (c - n_main - (order - 1) * n_qkv)
        lhs = u if order == 0 else u_scr[order - 1]
        acc = jnp.dot(lhs, w_ref[:, c * tn:(c + 1) * tn], preferred_element_type=F32)
        if jq in (0, 1):
            scale = q_scale if jq == 0 else 1.0
            tabs = (c_ref, s1_ref, s2_ref) if order == 0 else tuple(rope_scr.at[order - 1, ti] for ti in range(3))
            cs = tabs[0][...] * scale
            s1 = tabs[1][...] * scale
            s2 = tabs[2][...] * scale
            parts = []
            for h in range(tn // HEAD_DIM):
                a = acc[:, h * HEAD_DIM:(h + 1) * HEAD_DIM]
                parts.append(a * cs + pltpu.roll(a, LANE - ROPE_HALF, 1) * s1 + pltpu.roll(a, ROPE_HALF, 1) * s2)
            acc = jnp.concatenate(parts, axis=1)
        val = acc.astype(BF16)
        if order == 0:
            main_ref[:, c * tn:(c + 1) * tn] = val
        else:
            dst = g1_ref if order == 1 else g2_ref
            c0 = (c - n_main - (order - 1) * n_qkv) * tn
            dst[:, :, c0:c0 + tn] = val.reshape(dst.shape[0], dst.shape[1], tn)


def _inproj(x2, sc1, sh1, norm_w, w_packed, w_dt, rope_c, rope_s1, rope_s2, batch, seq):
    t, d = x2.shape
    tm, tn = INPROJ_TM, INPROJ_TN
    n = w_packed.shape[1]
    per_b = seq // tm
    d1, d2 = ATTN_DILATIONS[1], ATTN_DILATIONS[2]
    row = lambda i: (i, 0)
    modrow = lambda i: (i // per_b, 0, 0)
    const = lambda i: (0, 0)
    resident = pl.Buffered(1)
    qkv_map = lambda i: (i // per_b, 0, i % per_b, 0)
    return pl.pallas_call(
        functools.partial(_inproj_kernel, tn=tn, q_scale=1.0 / math.sqrt(HEAD_DIM)),
        grid=(t // tm,),
        in_specs=[pl.BlockSpec((tm, d), row),
                  pl.BlockSpec((None, 1, d), modrow),
                  pl.BlockSpec((None, 1, d), modrow),
                  pl.BlockSpec((1, d), const),
                  pl.BlockSpec((d, n), const, pipeline_mode=resident),
                  pl.BlockSpec((d, LANE), const),
                  pl.BlockSpec((tm, LANE), row),
                  pl.BlockSpec((tm, LANE), row),
                  pl.BlockSpec((tm, LANE), row)],
        out_specs=[pl.BlockSpec((tm, MAIN_W), row),
                   pl.BlockSpec((None, d1, tm // d1, QKV_W), qkv_map),
                   pl.BlockSpec((None, d2, tm // d2, QKV_W), qkv_map),
                   pl.BlockSpec((tm, LANE), row)],
        out_shape=[jax.ShapeDtypeStruct((t, MAIN_W), BF16),
                   jax.ShapeDtypeStruct((batch, d1, seq // d1, QKV_W), BF16),
                   jax.ShapeDtypeStruct((batch, d2, seq // d2, QKV_W), BF16),
                   jax.ShapeDtypeStruct((t, LANE), F32)],
        scratch_shapes=[pltpu.VMEM((2, tm, d), BF16), pltpu.VMEM((2, 3, tm, LANE), F32),
                        pltpu.VMEM((d // LANE, tm, LANE), F32)],
        compiler_params=_cp(("arbitrary",)),
        name="inproj",
    )(x2, sc1, sh1, norm_w, w_packed, w_dt, rope_c, rope_s1, rope_s2)


def _attn_kernel(q_ref, k_ref, v_ref, o_ref, st_ref, *, d, nb):
    blk = ATTN_BLK
    qi = lax.broadcasted_iota(I32, (blk, 2 * blk), 0)
    kj = lax.broadcasted_iota(I32, (blk, 2 * blk), 1)
    band = (kj >= qi) & (kj <= qi + blk)
    qi1 = lax.broadcasted_iota(I32, (blk, blk), 0)
    kj1 = lax.broadcasted_iota(I32, (blk, blk), 1)
    causal = kj1 <= qi1
    lane = kj1

    hsl = [slice(h * HEAD_DIM, (h + 1) * HEAD_DIM) for h in range(HEADS_PER_GROUP)]

    def blocks(items, nk, mask):
        s = jnp.concatenate(
            [lax.dot_general(q_ref[r, pl.ds(q0, blk), hs], k_ref[r, pl.ds(k0, nk), hs], (((1,), (1,)), ((), ())),
                             preferred_element_type=F32) for r, q0, k0 in items for hs in hsl], axis=0)
        s = jnp.where(jnp.concatenate([mask] * (HEADS_PER_GROUP * len(items)), axis=0), s, NEG)
        m = jnp.max(s, axis=-1, keepdims=True)
        p = jnp.exp(s - m)
        l = jnp.sum(p, axis=-1, keepdims=True)
        pb = p.astype(BF16)
        for it, (r, q0, k0) in enumerate(items):
            rows = pl.ds(q0, blk) if d == 1 else pl.ds(q0 * d + r, blk, stride=d)
            st = jnp.zeros((blk, LANE), F32)
            for h, hs in enumerate(hsl):
                hr = slice((it * HEADS_PER_GROUP + h) * blk, (it * HEADS_PER_GROUP + h + 1) * blk)
                o = jnp.dot(pb[hr], v_ref[r, pl.ds(k0, nk), hs], preferred_element_type=F32)
                o_ref[h, rows, :] = o / l[hr]
                st = jnp.where(lane == h, m[hr], st)
                st = jnp.where(lane == HEADS_PER_GROUP + h, l[hr], st)
            st_ref[rows, :] = st

    def band_item(r, n):
        q0 = pl.multiple_of(n * blk, blk)
        return (r, q0, pl.multiple_of(q0 - blk, blk))

    if d == 1:
        blocks([(0, 0, 0)], blk, causal)

        def body(j, c):
            blocks([band_item(0, 1 + 2 * j), band_item(0, 2 + 2 * j)], 2 * blk, band)
            return c
        lax.fori_loop(0, (nb - 1) // 2, body, 0, unroll=2)
        if (nb - 1) % 2:
            blocks([band_item(0, nb - 1)], 2 * blk, band)
    else:
        def body(j, c):
            blocks([(2 * j, 0, 0), (2 * j + 1, 0, 0)], blk, causal)
            for n in range(1, nb):
                blocks([band_item(2 * j, n), band_item(2 * j + 1, n)], 2 * blk, band)
            return c
        lax.fori_loop(0, d // 2, body, 0, unroll=2 if nb == 1 else 1)


def _attention_group(src, g, batch, seq, col0):
    d = ATTN_DILATIONS[g]
    n_sub = seq // d
    nb = n_sub // ATTN_BLK
    spec = lambda c: pl.BlockSpec((None, d, n_sub, GROUP_W), lambda b: (b, 0, 0, c))
    o, st = pl.pallas_call(
        functools.partial(_attn_kernel, d=d, nb=nb),
        grid=(batch,),
        in_specs=[spec(col0), spec(col0 + 1), spec(col0 + 2)],
        out_specs=[pl.BlockSpec((HEADS_PER_GROUP, seq, HEAD_DIM), lambda b: (0, b, 0)),
                   pl.BlockSpec((seq, LANE), lambda b: (b, 0))],
        out_shape=[jax.ShapeDtypeStruct((HEADS_PER_GROUP, batch * seq, HEAD_DIM), F32),
                   jax.ShapeDtypeStruct((batch * seq, LANE), F32)],
        compiler_params=_cp(("arbitrary",)),
        name=f"attn_d{d}",
    )(src, src, src)
    return o, st


def _ssd_kernel(xbc_ref, z_ref, dt_ref, cw_ref, cb_ref, dtb_ref, alog_ref, dsk_ref, nw_ref,
                out_ref, xwin, ystage, state):
    L = SSM_CHUNK
    inner = SSM_HEADS * SSM_HEADDIM
    gw = SSM_STATE
    c = pl.program_id(1)

    @pl.when(c == 0)
    def _():
        xwin[0:L, :] = jnp.zeros((L, xwin.shape[1]), BF16)
        state[...] = jnp.zeros(state.shape, F32)

    xcur = xbc_ref[...]
    xwin[L:2 * L, :] = xcur
    win = xwin[...]
    srow = lax.broadcasted_iota(I32, (L, 2 * L), 0)
    scol = lax.broadcasted_iota(I32, (L, 2 * L), 1)
    conv = cb_ref[...] + cw_ref[SSM_CONV - 1:SSM_CONV, :] * xcur.astype(F32)
    for s in range(1, SSM_CONV):
        shift_m = jnp.where(scol == srow + (L - s), 1.0, 0.0).astype(BF16)
        conv = conv + cw_ref[SSM_CONV - 1 - s:SSM_CONV - s, :] * jnp.dot(shift_m, win, preferred_element_type=F32)
    xwin[0:L, :] = xcur
    act = _silu(conv)
    xs = act[:, :inner]
    xs_b = xs.astype(BF16)

    lane = lax.broadcasted_iota(I32, (L, LANE), 1)
    row = lax.broadcasted_iota(I32, (L, LANE), 0)
    dtr = dt_ref[...] + dtb_ref[...]
    dt = jnp.maximum(dtr, 0.0) + jnp.log(1.0 + jnp.exp(-jnp.abs(dtr)))
    a_neg = jnp.where(lane < SSM_HEADS, -jnp.exp(alog_ref[...]), 0.0)
    a = dt * a_neg
    cs = a
    shift = 1
    while shift < L:
        cs = cs + jnp.where(row >= shift, pltpu.roll(cs, shift, 0), 0.0)
        shift *= 2
    cs_t = cs.T
    dt_t = dt.T
    tri = row >= lane
    half = lane < SSM_HEADDIM
    zero_b = jnp.zeros((L, LANE), BF16)

    for g in range(SSM_GROUPS):
        bg = act[:, inner + g * gw:inner + (g + 1) * gw]
        cg = act[:, inner + SSM_GROUPS * gw + g * gw:inner + SSM_GROUPS * gw + (g + 1) * gw]
        cg_b = cg.astype(BF16)
        cb = lax.dot_general(cg_b, bg.astype(BF16), (((1,), (1,)), ((), ())), preferred_element_type=F32)
        bg_t = bg.T
        for pair in range(2):
            h0 = g * 4 + pair * 2
            pidx = h0 // 2
            xpp = xs_b[:, pidx * LANE:(pidx + 1) * LANE]
            rhs = jnp.concatenate([jnp.where(half, xpp, zero_b), jnp.where(half, zero_b, xpp)], axis=0)
            dec, dst, eoff, cdec = [], [], [], []
            for h in (h0, h0 + 1):
                cs_col = cs[:, h:h + 1]
                cs_row = cs_t[h:h + 1, :]
                dt_row = dt_t[h:h + 1, :]
                dec.append(cb * (jnp.exp(jnp.where(tri, cs_col - cs_row, NEG)) * dt_row))
                cs_last = cs_row[:, L - 1:L]
                dst.append(bg_t * (jnp.exp(cs_last - cs_row) * dt_row))
                eoff.append(jnp.exp(cs_col))
                cdec.append(jnp.exp(cs_last))
            y_diag = jnp.dot(jnp.concatenate(dec, axis=1).astype(BF16), rhs, preferred_element_type=F32)
            st_new = jnp.dot(jnp.concatenate(dst, axis=1).astype(BF16), rhs, preferred_element_type=F32)
            prev = state[pidx]
            y_off = jnp.dot(cg_b, prev.astype(BF16), preferred_element_type=F32)
            y_off = y_off * jnp.where(half, eoff[0], eoff[1])
            state[pidx] = prev * jnp.where(half, cdec[0], cdec[1]) + st_new
            y = y_diag + y_off + dsk_ref[:, pidx * LANE:(pidx + 1) * LANE] * xs[:, pidx * LANE:(pidx + 1) * LANE]
            out_pair = y * _silu(z_ref[:, pidx * LANE:(pidx + 1) * LANE].astype(F32))
            ystage[:, pidx * LANE:(pidx + 1) * LANE] = out_pair

    gsz = inner // SSM_GROUPS
    for g in range(SSM_GROUPS):
        yg = ystage[:, g * gsz:(g + 1) * gsz]
        ms = jnp.mean(yg * yg, axis=-1, keepdims=True)
        out_ref[:, g * gsz:(g + 1) * gsz] = (yg * lax.rsqrt(ms + NORM_EPS) * nw_ref[:, g * gsz:(g + 1) * gsz]).astype(BF16)


def _ssd(proj, dt_raw, conv_w, conv_b, dt_bias, a_log, d_skip, ssm_norm_w, batch, seq):
    t = batch * seq
    L = SSM_CHUNK
    nc = seq // L
    inner = SSM_HEADS * SSM_HEADDIM
    cdim = conv_w.shape[1]
    pad16 = lambda v: jnp.pad(v.astype(F32), (0, LANE - SSM_HEADS)).reshape(1, LANE)
    dsk = jnp.repeat(d_skip.astype(F32), SSM_HEADDIM).reshape(1, inner)
    rowc = lambda b, c: (b * nc + c, 0)
    const = lambda b, c: (0, 0)
    return pl.pallas_call(
        _ssd_kernel,
        grid=(batch, nc),
        in_specs=[pl.BlockSpec((L, cdim), lambda b, c: (b * nc + c, COL_XBC // cdim)),
                  pl.BlockSpec((L, inner), lambda b, c: (b * nc + c, COL_Z // inner)),
                  pl.BlockSpec((L, LANE), rowc),
                  pl.BlockSpec((SSM_CONV, cdim), const),
                  pl.BlockSpec((1, cdim), const),
                  pl.BlockSpec((1, LANE), const),
                  pl.BlockSpec((1, LANE), const),
                  pl.BlockSpec((1, inner), const),
                  pl.BlockSpec((1, inner), const)],
        out_specs=pl.BlockSpec((L, inner), rowc),
        out_shape=jax.ShapeDtypeStruct((t, inner), BF16),
        scratch_shapes=[pltpu.VMEM((2 * L, cdim), BF16), pltpu.VMEM((L, inner), F32),
                        pltpu.VMEM((SSM_HEADS // 2, SSM_STATE, 2 * SSM_HEADDIM), F32)],
        compiler_params=_cp(("arbitrary", "arbitrary")),
        name="ssd",
    )(proj, proj, dt_raw, conv_w.astype(F32), conv_b.reshape(1, cdim).astype(F32), pad16(dt_bias), pad16(a_log),
      dsk, ssm_norm_w.reshape(1, inner).astype(F32))


def _mix_kernel(o0_ref, o1_ref, o2_ref, s0_ref, s1_ref, s2_ref, ssm_ref, ga_ref, gs_ref, x_ref,
                g1_ref, sc2_ref, sh2_ref, g2_ref, nw_ref, wba_ref, wbs_ref, wo_ref, wrt_ref,
                wgs_ref, wus_ref, wds_ref, base_ref, u2t_ref, sct_ref):
    o_refs = (o0_ref, o1_ref, o2_ref)
    s_refs = (s0_ref, s1_ref, s2_ref)
    tm = x_ref.shape[0]
    sub = tm // MIX_SPLIT
    for part in range(MIX_SPLIT):
        rs = slice(part * sub, (part + 1) * sub)
        heads = []
        for h in range(HEADS_PER_GROUP):
            ms = [s[rs, h:h + 1] for s in s_refs]
            ls = [s[rs, HEADS_PER_GROUP + h:HEADS_PER_GROUP + h + 1] for s in s_refs]
            mx = jnp.maximum(jnp.maximum(ms[0], ms[1]), ms[2])
            wts = [l * jnp.exp(m - mx) for m, l in zip(ms, ls)]
            num = wts[0] * o_refs[0][h, rs, :] + wts[1] * o_refs[1][h, rs, :] + wts[2] * o_refs[2][h, rs, :]
            heads.append((num / (wts[0] + wts[1] + wts[2])).astype(BF16))
        attn = jnp.concatenate(heads, axis=1)
        ya = jnp.dot(attn, wba_ref[...], preferred_element_type=F32)
        ys = jnp.dot(ssm_ref[rs, :], wbs_ref[...], preferred_element_type=F32)
        merged = _sigmoid(ga_ref[rs, :].astype(F32)) * ya + _sigmoid(gs_ref[rs, :].astype(F32)) * ys
        mix = jnp.dot(merged.astype(BF16), wo_ref[...], preferred_element_type=F32)
        h1 = x_ref[rs, :] + g1_ref[...] * mix
        ms2 = jnp.mean(h1 * h1, axis=-1, keepdims=True)
        u2 = h1 * lax.rsqrt(ms2 + NORM_EPS) * nw_ref[...] * (1.0 + sc2_ref[...]) + sh2_ref[...]
        for cc in range(ROW_TILE):
            u2t_ref[pl.ds(part * sub * ROW_TILE + cc, sub, stride=ROW_TILE), :] = u2[:, cc * LANE:(cc + 1) * LANE]
        u2b = u2.astype(BF16)
        u2lo = (u2 - u2b.astype(F32)).astype(BF16)
        nt = (((1,), (1,)), ((), ()))
        logits_t = (lax.dot_general(wrt_ref[0], u2b, nt, preferred_element_type=F32)
                    + lax.dot_general(wrt_ref[0], u2lo, nt, preferred_element_type=F32)
                    + lax.dot_general(wrt_ref[1], u2b, nt, preferred_element_type=F32))
        sct_ref[:, rs] = _sigmoid(logits_t)
        hs_ = (_silu(jnp.dot(u2b, wgs_ref[...], preferred_element_type=F32))
               * jnp.dot(u2b, wus_ref[...], preferred_element_type=F32))
        shared = jnp.dot(hs_.astype(BF16), wds_ref[...], preferred_element_type=F32)
        base_ref[rs, :] = h1 + g2_ref[...] * shared


def _mix(o_list, st_list, ssm, proj, x2, g1, sc2, sh2, g2, norm_w, w_ba, w_bs, w_o, w_rt, w_gs, w_us, w_ds, seq):
    t, d = x2.shape
    tm = MIX_TM
    per_b = seq // tm
    row = lambda i: (i, 0)
    modrow = lambda i: (i // per_b, 0, 0)
    const = lambda i: (0, 0)
    full = lambda a: pl.BlockSpec(a.shape, lambda i: (0,) * a.ndim, pipeline_mode=pl.Buffered(1))
    mod = pl.BlockSpec((None, 1, d), modrow)
    return pl.pallas_call(
        _mix_kernel,
        grid=(t // tm,),
        in_specs=[pl.BlockSpec((HEADS_PER_GROUP, tm, HEAD_DIM), lambda i: (0, i, 0))] * 3
        + [pl.BlockSpec((tm, LANE), row)] * 3 + [
            pl.BlockSpec((tm, d), row),
            pl.BlockSpec((tm, d), lambda i: (i, COL_GA // d)),
            pl.BlockSpec((tm, d), lambda i: (i, COL_GS // d)),
            pl.BlockSpec((tm, d), row),
            mod, mod, mod, mod, full(norm_w), full(w_ba), full(w_bs), full(w_o), full(w_rt),
            full(w_gs), full(w_us), full(w_ds)],
        out_specs=[pl.BlockSpec((tm, d), row), pl.BlockSpec((tm * ROW_TILE, LANE), row),
                   pl.BlockSpec((N_EXPERTS, tm), lambda i: (0, i))],
        out_shape=[jax.ShapeDtypeStruct((t, d), F32), jax.ShapeDtypeStruct((t * ROW_TILE, LANE), F32),
                   jax.ShapeDtypeStruct((N_EXPERTS, t), F32)],
        compiler_params=_cp(("arbitrary",)),
        name="mix",
    )(*o_list, *st_list, ssm, proj, proj, x2, g1, sc2, sh2, g2, norm_w, w_ba, w_bs, w_o, w_rt, w_gs, w_us, w_ds)


def _route_kernel(sct_ref, bias_ref, idx_ref, w_ref, mem_ref, cnt_ref, run):
    i = pl.program_id(0)
    tm = sct_ref.shape[1]
    per_g = N_EXPERTS // N_EXPERT_GROUPS

    @pl.when(i == 0)
    def _():
        run[...] = jnp.zeros(run.shape, F32)

    s = sct_ref[...]
    biased = s + bias_ref[...]
    io_g = lax.broadcasted_iota(I32, (per_g, tm), 0).astype(F32)
    gscore = []
    for g in range(N_EXPERT_GROUPS):
        bgp = biased[g * per_g:(g + 1) * per_g, :]
        m1 = jnp.max(bgp, axis=0, keepdims=True)
        first = jnp.min(jnp.where(bgp == m1, io_g, float(per_g)), axis=0, keepdims=True)
        m2 = jnp.max(jnp.where(io_g == first, NEG, bgp), axis=0, keepdims=True)
        gscore.append(m1 + m2)
    gs = jnp.concatenate(gscore, axis=0)
    io8 = lax.broadcasted_iota(I32, (N_EXPERT_GROUPS, tm), 0).astype(F32)
    gsel = jnp.zeros((N_EXPERT_GROUPS, tm), F32)
    cur = gs
    for _ in range(TOPK_GROUPS):
        mx = jnp.max(cur, axis=0, keepdims=True)
        fi = jnp.min(jnp.where(cur == mx, io8, float(N_EXPERT_GROUPS)), axis=0, keepdims=True)
        hit = io8 == fi
        gsel = jnp.where(hit, 1.0, gsel)
        cur = jnp.where(hit, NEG, cur)
    masked = jnp.concatenate(
        [jnp.where(gsel[g:g + 1, :] > 0.5, biased[g * per_g:(g + 1) * per_g, :], NEG) for g in range(N_EXPERT_GROUPS)],
        axis=0)
    io_e = lax.broadcasted_iota(I32, (N_EXPERTS, tm), 0).astype(F32)
    member = jnp.zeros((N_EXPERTS, tm), F32)
    idxs, ws = [], []
    for _ in range(TOP_K):
        mx = jnp.max(masked, axis=0, keepdims=True)
        fi = jnp.min(jnp.where(masked == mx, io_e, float(N_EXPERTS)), axis=0, keepdims=True)
        hit = io_e == fi
        idxs.append(fi)
        ws.append(jnp.sum(jnp.where(hit, s, 0.0), axis=0, keepdims=True))
        member = jnp.where(hit, 1.0, member)
        masked = jnp.where(hit, NEG, masked)
    wsum = ws[0]
    for k in range(1, TOP_K):
        wsum = wsum + ws[k]
    idx_ref[...] = jnp.concatenate(idxs, axis=0).astype(I32)
    w_ref[...] = jnp.concatenate([w / wsum * ROUTED_SCALE for w in ws], axis=0)
    mem_ref[...] = member.astype(BF16)
    new_run = run[...] + jnp.sum(member, axis=1, keepdims=True)
    run[...] = new_run
    cnt_ref[...] = new_run


def _route(scores_t, router_bias):
    e, t = scores_t.shape
    tm = ROUTE_TM
    tok = pl.BlockSpec((TOP_K, tm), lambda i: (0, i))
    return pl.pallas_call(
        _route_kernel,
        grid=(t // tm,),
        in_specs=[pl.BlockSpec((e, tm), lambda i: (0, i)), pl.BlockSpec((e, 1), lambda i: (0, 0))],
        out_specs=[tok, tok, pl.BlockSpec((e, tm), lambda i: (0, i)), pl.BlockSpec((e, LANE), lambda i: (0, 0))],
        out_shape=[jax.ShapeDtypeStruct((TOP_K, t), I32), jax.ShapeDtypeStruct((TOP_K, t), F32),
                   jax.ShapeDtypeStruct((e, t), BF16), jax.ShapeDtypeStruct((e, LANE), F32)],
        scratch_shapes=[pltpu.VMEM((e, LANE), F32)],
        compiler_params=_cp(("arbitrary",)),
        name="route",
    )(scores_t, router_bias.reshape(e, 1).astype(F32))


def _dest_kernel(mem_ref, idx_ref, start_ref, dest_ref, run):
    i = pl.program_id(0)
    e, tm = mem_ref.shape

    @pl.when(i == 0)
    def _():
        run[...] = jnp.broadcast_to(start_ref[...], run.shape)

    member = mem_ref[...]
    tr = lax.broadcasted_iota(I32, (tm, tm), 0)
    tc = lax.broadcasted_iota(I32, (tm, tm), 1)
    upper = jnp.where(tr < tc, 1.0, 0.0).astype(BF16)
    rank_full = jnp.dot(member, upper, preferred_element_type=F32) + run[:, 0:1]
    io_e = lax.broadcasted_iota(I32, (e, tm), 0)
    idx = idx_ref[...]
    rows = [jnp.sum(jnp.where(io_e == idx[k:k + 1, :], rank_full, 0.0), axis=0, keepdims=True) for k in range(TOP_K)]
    dest_ref[...] = jnp.concatenate(rows, axis=0).astype(I32)
    run[...] = run[...] + jnp.sum(member.astype(F32), axis=1, keepdims=True)


def _dest(member, idx, pad_start):
    e, t = member.shape
    tm = ROUTE_TM
    return pl.pallas_call(
        _dest_kernel,
        grid=(t // tm,),
        in_specs=[pl.BlockSpec((e, tm), lambda i: (0, i)), pl.BlockSpec((TOP_K, tm), lambda i: (0, i)),
                  pl.BlockSpec((e, 1), lambda i: (0, 0))],
        out_specs=pl.BlockSpec((TOP_K, tm), lambda i: (0, i)),
        out_shape=jax.ShapeDtypeStruct((TOP_K, t), I32),
        scratch_shapes=[pltpu.VMEM((e, LANE), F32)],
        compiler_params=_cp(("arbitrary",)),
        name="dest",
    )(member, idx, pad_start.astype(F32).reshape(e, 1))


def _dispatch_kernel(pend_ref, npad_ref, dest_ref, u_ref, xs_ref, zero_scr, sem, zsem):
    i = pl.program_id(0)
    rt = ROW_TILE
    tm = u_ref.shape[0] // rt
    zrows = zero_scr.shape[0]

    @pl.when(i == 0)
    def _():
        zero_scr[...] = jnp.zeros(zero_scr.shape, zero_scr.dtype)

        def zcopy(e):
            start = pl.multiple_of(pend_ref[e] * rt - zrows, zrows)
            return pltpu.make_async_copy(zero_scr, xs_ref.at[pl.ds(start, zrows)], zsem)

        def zstart(e, c):
            @pl.when(npad_ref[e] > 0)
            def _():
                zcopy(e).start()
            return c

        def zwait(e, c):
            @pl.when(npad_ref[e] > 0)
            def _():
                zcopy(e).wait()
            return c

        lax.fori_loop(0, N_EXPERTS, zstart, 0)
        lax.fori_loop(0, N_EXPERTS, zwait, 0)

    def row_copy(t, k):
        src = u_ref.at[pl.ds(pl.multiple_of(t * rt, rt), rt)]
        dst = xs_ref.at[pl.ds(pl.multiple_of(dest_ref[t * TOP_K + k] * rt, rt), rt)]
        return pltpu.make_async_copy(src, dst, sem)

    def start(t, c):
        for k in range(TOP_K):
            row_copy(t, k).start(priority=k % 2)
        return c

    def wait(t, c):
        for k in range(TOP_K):
            row_copy(t, k).wait()
        return c

    lax.fori_loop(0, tm, start, 0)
    lax.fori_loop(0, tm, wait, 0)


def _dispatch(u2t, dest_flat, pad_end, padded, n_rows):
    rt = ROW_TILE
    t = u2t.shape[0] // rt
    tm = DISPATCH_TM
    gs = pltpu.PrefetchScalarGridSpec(
        num_scalar_prefetch=2,
        grid=(t // tm,),
        in_specs=[pl.BlockSpec((tm * TOP_K,), lambda i, pe, npd: (i,), memory_space=pltpu.SMEM),
                  pl.BlockSpec((tm * rt, LANE), lambda i, pe, npd: (i, 0))],
        out_specs=pl.BlockSpec(memory_space=pl.ANY),
        scratch_shapes=[pltpu.VMEM((MOE_ROWS * rt, LANE), F32), pltpu.SemaphoreType.DMA, pltpu.SemaphoreType.DMA],
    )
    return pl.pallas_call(
        _dispatch_kernel,
        grid_spec=gs,
        out_shape=jax.ShapeDtypeStruct((n_rows * rt, LANE), F32),
        compiler_params=_cp(("arbitrary",)),
        name="dispatch",
    )(pad_end, padded, dest_flat, u2t)


def _expert_kernel(bstart_ref, nblk_ref, wg_ref, wu_ref, wd_ref, xs_ref, ys_ref,
                   wg_b, wu_b, wd_b, xbuf, ybuf, in_sem, out_sem):
    e = pl.program_id(0)
    last = pl.num_programs(0) - 1
    rt = ROW_TILE
    n_in, rb = xbuf.shape[0], xbuf.shape[1] // rt
    n_out = ybuf.shape[0]
    n = nblk_ref[e]
    g0 = bstart_ref[e]
    total = bstart_ref[last] + nblk_ref[last]

    def block_rows(g):
        return pl.ds(pl.multiple_of(g * (rb * rt), rb * rt), rb * rt)

    def in_copy(g):
        return pltpu.make_async_copy(xs_ref.at[block_rows(g)], xbuf.at[g % n_in], in_sem.at[g % n_in])

    def out_copy(g):
        return pltpu.make_async_copy(ybuf.at[g % n_out], ys_ref.at[block_rows(g)], out_sem.at[g % n_out])

    @pl.when(e == 0)
    def _():
        for g in range(n_in - 1):
            @pl.when(g < total)
            def _():
                in_copy(g).start()

    wg_b[...] = wg_ref[...].astype(BF16)
    wu_b[...] = wu_ref[...].astype(BF16)
    wd_b[...] = wd_ref[...].astype(BF16)

    def body(b, carry):
        g = g0 + b
        in_copy(g).wait()

        @pl.when(g + n_in - 1 < total)
        def _():
            in_copy(g + n_in - 1).start()

        @pl.when(g >= n_out)
        def _():
            out_copy(g - n_out).wait()

        xb = jnp.concatenate([xbuf[g % n_in, pl.ds(cc, rb, stride=rt), :].astype(BF16) for cc in range(rt)], axis=1)
        gate = jnp.dot(xb, wg_b[...], preferred_element_type=F32)
        up = jnp.dot(xb, wu_b[...], preferred_element_type=F32)
        h = (_silu(gate) * up).astype(BF16)
        y = jnp.dot(h, wd_b[...], preferred_element_type=F32)
        for cc in range(rt):
            ybuf[g % n_out, pl.ds(cc, rb, stride=rt), :] = y[:, cc * LANE:(cc + 1) * LANE]
        out_copy(g).start()
        return carry

    lax.fori_loop(0, n, body, 0)

    @pl.when(e == last)
    def _():
        for j in range(n_out, 0, -1):
            @pl.when(total >= j)
            def _():
                out_copy(total - j).wait()


def _experts(xs, blk_start, nblk, w_gate_e, w_up_e, w_down_e):
    ne, d, ff = w_gate_e.shape
    rb = MOE_ROWS * ROW_TILE
    gs = pltpu.PrefetchScalarGridSpec(
        num_scalar_prefetch=2,
        grid=(ne,),
        in_specs=[pl.BlockSpec((None, d, ff), lambda e, ps, nb: (e, 0, 0)),
                  pl.BlockSpec((None, d, ff), lambda e, ps, nb: (e, 0, 0)),
                  pl.BlockSpec((None, ff, d), lambda e, ps, nb: (e, 0, 0)),
                  pl.BlockSpec(memory_space=pl.ANY)],
        out_specs=pl.BlockSpec(memory_space=pl.ANY),
        scratch_shapes=[pltpu.VMEM((d, ff), BF16), pltpu.VMEM((d, ff), BF16), pltpu.VMEM((ff, d), BF16),
                        pltpu.VMEM((EXPERT_IN_SLOTS, rb, LANE), F32), pltpu.VMEM((EXPERT_OUT_SLOTS, rb, LANE), F32),
                        pltpu.SemaphoreType.DMA((EXPERT_IN_SLOTS,)), pltpu.SemaphoreType.DMA((EXPERT_OUT_SLOTS,))],
    )
    return pl.pallas_call(
        _expert_kernel,
        grid_spec=gs,
        out_shape=jax.ShapeDtypeStruct(xs.shape, F32),
        compiler_params=_cp(("arbitrary",)),
        name="experts",
    )(blk_start, nblk, w_gate_e, w_up_e, w_down_e, xs)


def _combine_kernel(dest_ref, dnext_ref, wt_ref, base_ref, g2_ref, nfw_ref, ys_ref, out_ref, buf, sem):
    i = pl.program_id(0)
    n = pl.num_programs(0)
    tm = base_ref.shape[0]
    rt = ROW_TILE
    half = i % 2

    def row_copy(d_ref, h, t, k):
        src = ys_ref.at[pl.ds(pl.multiple_of(d_ref[t * TOP_K + k] * rt, rt), rt)]
        return pltpu.make_async_copy(src, buf.at[h, k, pl.ds(pl.multiple_of(t * rt, rt), rt)], sem.at[h])

    def issue(d_ref, h):
        def start(t, c):
            for k in range(TOP_K):
                row_copy(d_ref, h, t, k).start(priority=k % 2)
            return c
        lax.fori_loop(0, tm, start, 0)

    @pl.when(i == 0)
    def _():
        issue(dest_ref, 0)

    @pl.when(i + 1 < n)
    def _():
        issue(dnext_ref, 1 - half)

    def wait(t, c):
        for k in range(TOP_K):
            row_copy(dest_ref, half, t, k).wait()
        return c

    lax.fori_loop(0, tm, wait, 0)

    def weigh(t, c):
        rows = pl.ds(pl.multiple_of(t * rt, rt), rt)
        acc = buf[half, 0, rows, :] * wt_ref[t * TOP_K]
        for k in range(1, TOP_K):
            acc = acc + buf[half, k, rows, :] * wt_ref[t * TOP_K + k]
        buf[half, 0, rows, :] = acc
        return c

    lax.fori_loop(0, tm, weigh, 0, unroll=4)
    ssq = jnp.zeros((tm, 1), F32)
    for cc in range(rt):
        cs = slice(cc * LANE, (cc + 1) * LANE)
        h2 = base_ref[:, cs] + g2_ref[:, cs] * buf[half, 0, pl.ds(cc, tm, stride=rt), :]
        out_ref[:, cs] = h2
        ssq = ssq + jnp.sum(h2 * h2, axis=-1, keepdims=True)
    inv = lax.rsqrt(ssq / (rt * LANE) + NORM_EPS)
    out_ref[...] = out_ref[...] * inv * nfw_ref[...]


def _combine(ys, dest_flat, w_tok, base, g2, norm_final_w, seq):
    t, d = base.shape
    tm = COMBINE_TM
    per_b = seq // tm
    gs = pltpu.PrefetchScalarGridSpec(
        num_scalar_prefetch=0,
        grid=(t // tm,),
        in_specs=[pl.BlockSpec((tm * TOP_K,), lambda i: (i,), memory_space=pltpu.SMEM),
                  pl.BlockSpec((tm * TOP_K,), lambda i: (jnp.minimum(i + 1, t // tm - 1),), memory_space=pltpu.SMEM),
                  pl.BlockSpec((tm * TOP_K,), lambda i: (i,), memory_space=pltpu.SMEM),
                  pl.BlockSpec((tm, d), lambda i: (i, 0)),
                  pl.BlockSpec((None, 1, d), lambda i: (i // per_b, 0, 0)),
                  pl.BlockSpec((1, d), lambda i: (0, 0)),
                  pl.BlockSpec(memory_space=pl.ANY)],
        out_specs=pl.BlockSpec((tm, d), lambda i: (i, 0)),
        scratch_shapes=[pltpu.VMEM((2, TOP_K, tm * ROW_TILE, LANE), F32), pltpu.SemaphoreType.DMA((2,))],
    )
    return pl.pallas_call(
        _combine_kernel,
        grid_spec=gs,
        out_shape=jax.ShapeDtypeStruct((t, d), F32),
        compiler_params=_cp(("arbitrary",)),
        name="combine",
    )(dest_flat, dest_flat, w_tok, base, g2, norm_final_w, ys)


def kernel(x, c, positions, w_mod, b_mod, norm_mix_w, norm_ffn_w, w_in, conv_w, conv_b, dt_bias, a_log, d_skip,
           ssm_norm_w, w_branch_attn, w_branch_ssm, w_out, w_router, router_bias, w_gate_e, w_up_e, w_down_e,
           w_gate_s, w_up_s, w_down_s, norm_final_w):
    batch, seq, d = x.shape
    t = batch * seq
    assert w_mod.shape[0] == 1, "one layer"
    assert d == ROW_TILE * LANE and seq % INPROJ_TM == 0 and INPROJ_TM % (ATTN_DILATIONS[-1] * 16) == 0

    mod = _modulation(c, w_mod[0], b_mod[0])
    sh1, sc1, g1, sh2, sc2, g2 = [m.reshape(batch, 1, d) for m in jnp.split(mod, 6, axis=-1)]
    rope_c, rope_s1, rope_s2 = _rope_tables(positions)

    wi = w_in[0]
    q_dim = 3 * GROUP_W
    o_z = 3 * q_dim
    o_xbc = o_z + d
    o_dt = o_xbc + conv_w.shape[2]
    o_g = o_dt + SSM_HEADS
    qkv = lambda g: [wi[:, s * q_dim + g * GROUP_W:s * q_dim + (g + 1) * GROUP_W] for s in range(3)]
    w_packed = jnp.concatenate([wi[:, o_xbc:o_dt], wi[:, o_g:], wi[:, o_z:o_xbc]] + qkv(0) + qkv(1) + qkv(2),
                               axis=1).astype(BF16)
    assert w_packed.shape[1] == MAIN_W + 2 * QKV_W
    w_dt = jnp.pad(wi[:, o_dt:o_g], ((0, 0), (0, LANE - SSM_HEADS))).astype(BF16)

    x2 = x.reshape(t, d)
    proj, qkv1, qkv2, dt_raw = _inproj(x2, sc1, sh1, norm_mix_w.reshape(1, d), w_packed, w_dt,
                                       rope_c, rope_s1, rope_s2, batch, seq)

    srcs = [(proj.reshape(batch, 1, seq, MAIN_W), COL_Q0 // GROUP_W), (qkv1, 0), (qkv2, 0)]
    o_list, st_list = [], []
    for g, (src, col0) in enumerate(srcs):
        o, st = _attention_group(src, g, batch, seq, col0)
        o_list.append(o)
        st_list.append(st)
    ssm = _ssd(proj, dt_raw, conv_w[0], conv_b[0], dt_bias[0], a_log[0], d_skip[0], ssm_norm_w[0], batch, seq)

    wr_t = w_router[0].T
    wr_hi = wr_t.astype(BF16)
    w_rt = jnp.stack([wr_hi, (wr_t - wr_hi.astype(F32)).astype(BF16)])
    base, u2t, scores_t = _mix(
        o_list, st_list, ssm, proj, x2, g1, sc2, sh2, g2, norm_ffn_w.reshape(1, d),
        w_branch_attn[0].astype(BF16), w_branch_ssm[0].astype(BF16), w_out[0].astype(BF16),
        w_rt, w_gate_s[0].astype(BF16), w_up_s[0].astype(BF16), w_down_s[0].astype(BF16), seq)

    idx, w_sel, member, counts = _route(scores_t, router_bias[0])

    rb = MOE_ROWS
    cnt = counts[:, 0].astype(I32)
    padded = (cnt + rb - 1) // rb * rb
    pad_end = jnp.cumsum(padded).astype(I32)
    pad_start = pad_end - padded
    n_rows = t * TOP_K + N_EXPERTS * rb

    dest_flat = _dest(member, idx, pad_start).T.reshape(t * TOP_K)
    xs = _dispatch(u2t, dest_flat, pad_end, padded, n_rows)
    ys = _experts(xs, pad_start // rb, padded // rb, w_gate_e[0], w_up_e[0], w_down_e[0])
    out = _combine(ys, dest_flat, w_sel.T.reshape(t * TOP_K), base, g2, norm_final_w.reshape(1, d), seq)
    return out.reshape(batch, seq, d)
```

```python
import functools
import math

import jax
import jax.numpy as jnp
from jax import lax
from jax.experimental import pallas as pl
from jax.experimental.pallas import tpu as pltpu

F32 = jnp.float32
BF16 = jnp.bfloat16
I32 = jnp.int32
U32 = jnp.uint32

LANE = 128
SUBLANE = 8
VMEM_LIMIT = 56 * 1024 * 1024

HEAD_DIM = 128
HEADS_PER_GROUP = 4
GROUP_W = HEADS_PER_GROUP * HEAD_DIM
ATTN_DILATIONS = (1, 4, 16)
ATTN_BLK = 128
ROPE_DIM = 32
ROPE_HALF = 16
ROPE_THETA = 500000.0
SSM_HEADS = 16
SSM_HEADDIM = 64
SSM_GROUPS = 4
SSM_STATE = 128
SSM_CONV = 4
SSM_CHUNK = 128
N_EXPERTS = 256
TOP_K = 8
N_EXPERT_GROUPS = 8
TOPK_GROUPS = 4
ROUTED_SCALE = 2.5
NORM_EPS = 1e-6
NEG = -1e30

COL_XBC, COL_GA, COL_GS, COL_Z, COL_Q0 = 0, 2048, 3072, 4096, 5120
MAIN_W = 6656
QKV_W = 3 * GROUP_W

INPROJ_TM, INPROJ_TN = 512, 512
MIX_TM, MIX_SPLIT = 512, 2
ROUTE_TM = 512
MOE_ROWS = 256
EXPERT_IN_SLOTS, EXPERT_OUT_SLOTS = 4, 3
DISPATCH_TM = 256
COMBINE_TM = 256
ROW_PACK = 4


def _cp(sem, vmem=VMEM_LIMIT):
    return pltpu.CompilerParams(dimension_semantics=sem, vmem_limit_bytes=vmem)


def _sigmoid(x):
    return 1.0 / (1.0 + jnp.exp(-x))


def _silu(x):
    return x * _sigmoid(x)


def _pack_pair(a, b):
    ua = lax.bitcast_convert_type(a.astype(BF16).astype(F32), U32)
    ub = lax.bitcast_convert_type(b.astype(BF16).astype(F32), U32)
    return (ua >> 16) | ub


def _unpack_pair(w):
    lo = lax.bitcast_convert_type(w << 16, F32)
    hi = lax.bitcast_convert_type(w & jnp.uint32(0xFFFF0000), F32)
    return lo, hi


def _pack_row_chunks(v):
    return [_pack_pair(v[:, c * LANE:(c + 1) * LANE], v[:, (c + ROW_PACK) * LANE:(c + ROW_PACK + 1) * LANE])
            for c in range(ROW_PACK)]


def _mod_kernel(c_ref, w_ref, b_ref, o_ref):
    cond = _silu(c_ref[...])
    o_ref[...] = jnp.dot(cond, w_ref[...], preferred_element_type=F32) + b_ref[...]


def _modulation(c, w_mod, b_mod):
    b, d = c.shape
    n = w_mod.shape[1]
    return pl.pallas_call(
        _mod_kernel,
        grid=(n // d,),
        in_specs=[pl.BlockSpec((b, d), lambda j: (0, 0)),
                  pl.BlockSpec((d, d), lambda j: (0, j)),
                  pl.BlockSpec((1, d), lambda j: (0, j))],
        out_specs=pl.BlockSpec((b, d), lambda j: (0, j)),
        out_shape=jax.ShapeDtypeStruct((b, n), F32),
        compiler_params=_cp(("arbitrary",)),
        name="modulation",
    )(c, w_mod, b_mod.reshape(1, n))


def _rope_kernel(pos_ref, inv_ref, c_ref, s1_ref, s2_ref):
    ang = pos_ref[...].astype(F32) * inv_ref[...]
    lane = lax.broadcasted_iota(I32, ang.shape, 1)
    cos = jnp.cos(ang)
    sin = jnp.sin(ang)
    c_ref[...] = jnp.where(lane < ROPE_DIM, cos, 1.0)
    s1_ref[...] = jnp.where(lane < ROPE_HALF, -sin, 0.0)
    s2_ref[...] = jnp.where((lane >= ROPE_HALF) & (lane < ROPE_DIM), sin, 0.0)


def _rope_tables(positions):
    t = positions.size
    tm = 2048
    inv_freq = ROPE_THETA ** (-jnp.arange(ROPE_HALF, dtype=F32) / ROPE_HALF)
    inv_row = jnp.concatenate([inv_freq, inv_freq, jnp.zeros((LANE - ROPE_DIM,), F32)]).reshape(1, LANE)
    spec = pl.BlockSpec((tm, LANE), lambda i: (i, 0))
    shp = jax.ShapeDtypeStruct((t, LANE), F32)
    return pl.pallas_call(
        _rope_kernel,
        grid=(t // tm,),
        in_specs=[pl.BlockSpec((tm, 1), lambda i: (i, 0)), pl.BlockSpec((1, LANE), lambda i: (0, 0))],
        out_specs=[spec, spec, spec],
        out_shape=[shp, shp, shp],
        compiler_params=_cp(("arbitrary",)),
        name="rope_tables",
    )(positions.reshape(t, 1), inv_row)


def _inproj_kernel(x_ref, sc_ref, sh_ref, nw_ref, w_ref, wdt_ref, c_ref, s1_ref, s2_ref,
                   main_ref, g1_ref, g2_ref, dt_ref, u_scr, rope_scr, uc_scr, *, tn, q_scale):
    tm = x_ref.shape[0]
    n_main = MAIN_W // tn
    n_qkv = QKV_W // tn

    x = x_ref[...]
    ms = jnp.mean(x * x, axis=-1, keepdims=True)
    y = x * lax.rsqrt(ms + NORM_EPS) * nw_ref[...]
    uf = y * (1.0 + sc_ref[...]) + sh_ref[...]
    u = uf.astype(BF16)
    dt_ref[...] = jnp.dot(u, wdt_ref[...], preferred_element_type=F32)
    n_chunk = uf.shape[1] // LANE
    for cc in range(n_chunk):
        uc_scr[cc] = uf[:, cc * LANE:(cc + 1) * LANE]
    for o, d in enumerate(ATTN_DILATIONS[1:]):
        rows = tm // d
        for r in range(d):
            for cc in range(n_chunk):
                u_scr[o, r * rows:(r + 1) * rows, cc * LANE:(cc + 1) * LANE] = (
                    uc_scr[cc, pl.ds(r, rows, stride=d), :].astype(BF16))
            for ti, tab in enumerate((c_ref, s1_ref, s2_ref)):
                rope_scr[o, ti, r * rows:(r + 1) * rows, :] = tab[pl.ds(r, rows, stride=d), :]

    for c in range(n_main + 2 * n_qkv):
        order = 0 if c < n_main else (1 if c < n_main + n_qkv else 2)
        jq = c - COL_Q0 // tn if order == 0 else (c - n_main - (order - 1) * n_qkv)
        lhs = u if order == 0 else u_scr[order - 1]
        acc = jnp.dot(lhs, w_ref[:, c * tn:(c + 1) * tn], preferred_element_type=F32)
        if jq in (0, 1):
            scale = q_scale if jq == 0 else 1.0
            tabs = (c_ref, s1_ref, s2_ref) if order == 0 else tuple(rope_scr.at[order - 1, ti] for ti in range(3))
            cs = tabs[0][...] * scale
            s1 = tabs[1][...] * scale
            s2 = tabs[2][...] * scale
            parts = []
            for h in range(tn // HEAD_DIM):
                a = acc[:, h * HEAD_DIM:(h + 1) * HEAD_DIM]
                parts.append(a * cs + pltpu.roll(a, LANE - ROPE_HALF, 1) * s1 + pltpu.roll(a, ROPE_HALF, 1) * s2)
            acc = jnp.concatenate(parts, axis=1)
        val = acc.astype(BF16)
        if order == 0:
            main_ref[:, c * tn:(c + 1) * tn] = val
        else:
            dst = g1_ref if order == 1 else g2_ref
            c0 = (c - n_main - (order - 1) * n_qkv) * tn
            dst[:, :, c0:c0 + tn] = val.reshape(dst.shape[0], dst.shape[1], tn)


def _inproj(x2, sc1, sh1, norm_w, w_packed, w_dt, rope_c, rope_s1, rope_s2, batch, seq):
    t, d = x2.shape
    tm, tn = INPROJ_TM, INPROJ_TN
    n = w_packed.shape[1]
    per_b = seq // tm
    d1, d2 = ATTN_DILATIONS[1], ATTN_DILATIONS[2]
    row = lambda i: (i, 0)
    modrow = lambda i: (i // per_b, 0, 0)
    const = lambda i: (0, 0)
    resident = pl.Buffered(1)
    qkv_map = lambda i: (i // per_b, 0, i % per_b, 0)
    return pl.pallas_call(
        functools.partial(_inproj_kernel, tn=tn, q_scale=1.0 / math.sqrt(HEAD_DIM)),
        grid=(t // tm,),
        in_specs=[pl.BlockSpec((tm, d), row),
                  pl.BlockSpec((None, 1, d), modrow),
                  pl.BlockSpec((None, 1, d), modrow),
                  pl.BlockSpec((1, d), const),
                  pl.BlockSpec((d, n), const, pipeline_mode=resident),
                  pl.BlockSpec((d, LANE), const),
                  pl.BlockSpec((tm, LANE), row),
                  pl.BlockSpec((tm, LANE), row),
                  pl.BlockSpec((tm, LANE), row)],
        out_specs=[pl.BlockSpec((tm, MAIN_W), row),
                   pl.BlockSpec((None, d1, tm // d1, QKV_W), qkv_map),
                   pl.BlockSpec((None, d2, tm // d2, QKV_W), qkv_map),
                   pl.BlockSpec((tm, LANE), row)],
        out_shape=[jax.ShapeDtypeStruct((t, MAIN_W), BF16),
                   jax.ShapeDtypeStruct((batch, d1, seq // d1, QKV_W), BF16),
                   jax.ShapeDtypeStruct((batch, d2, seq // d2, QKV_W), BF16),
                   jax.ShapeDtypeStruct((t, LANE), F32)],
        scratch_shapes=[pltpu.VMEM((2, tm, d), BF16), pltpu.VMEM((2, 3, tm, LANE), F32),
                        pltpu.VMEM((d // LANE, tm, LANE), F32)],
        compiler_params=_cp(("arbitrary",)),
        name="inproj",
    )(x2, sc1, sh1, norm_w, w_packed, w_dt, rope_c, rope_s1, rope_s2)


def _attn_kernel(q_ref, k_ref, v_ref, o_ref, st_ref, *, d, nb):
    blk = ATTN_BLK
    qi = lax.broadcasted_iota(I32, (blk, 2 * blk), 0)
    kj = lax.broadcasted_iota(I32, (blk, 2 * blk), 1)
    band = (kj >= qi) & (kj <= qi + blk)
    qi1 = lax.broadcasted_iota(I32, (blk, blk), 0)
    kj1 = lax.broadcasted_iota(I32, (blk, blk), 1)
    causal = kj1 <= qi1
    lane = kj1

    hsl = [slice(h * HEAD_DIM, (h + 1) * HEAD_DIM) for h in range(HEADS_PER_GROUP)]

    def blocks(items, nk, mask):
        s = jnp.concatenate(
            [lax.dot_general(q_ref[r, pl.ds(q0, blk), hs], k_ref[r, pl.ds(k0, nk), hs], (((1,), (1,)), ((), ())),
                             preferred_element_type=F32) for r, q0, k0 in items for hs in hsl], axis=0)
        s = jnp.where(jnp.concatenate([mask] * (HEADS_PER_GROUP * len(items)), axis=0), s, NEG)
        m = jnp.max(s, axis=-1, keepdims=True)
        p = jnp.exp(s - m)
        l = jnp.sum(p, axis=-1, keepdims=True)
        pb = p.astype(BF16)
        for it, (r, q0, k0) in enumerate(items):
            rows = pl.ds(q0, blk) if d == 1 else pl.ds(q0 * d + r, blk, stride=d)
            st = jnp.zeros((blk, LANE), F32)
            for h, hs in enumerate(hsl):
                hr = slice((it * HEADS_PER_GROUP + h) * blk, (it * HEADS_PER_GROUP + h + 1) * blk)
                o = jnp.dot(pb[hr], v_ref[r, pl.ds(k0, nk), hs], preferred_element_type=F32)
                o_ref[h, rows, :] = o / l[hr]
                st = jnp.where(lane == h, m[hr], st)
                st = jnp.where(lane == HEADS_PER_GROUP + h, l[hr], st)
            st_ref[rows, :] = st

    def band_item(r, n):
        q0 = pl.multiple_of(n * blk, blk)
        return (r, q0, pl.multiple_of(q0 - blk, blk))

    if d == 1:
        blocks([(0, 0, 0)], blk, causal)

        def body(j, c):
            blocks([band_item(0, 1 + 2 * j), band_item(0, 2 + 2 * j)], 2 * blk, band)
            return c
        lax.fori_loop(0, (nb - 1) // 2, body, 0, unroll=2)
        if (nb - 1) % 2:
            blocks([band_item(0, nb - 1)], 2 * blk, band)
    else:
        def body(j, c):
            blocks([(2 * j, 0, 0), (2 * j + 1, 0, 0)], blk, causal)
            for n in range(1, nb):
                blocks([band_item(2 * j, n), band_item(2 * j + 1, n)], 2 * blk, band)
            return c
        lax.fori_loop(0, d // 2, body, 0, unroll=2 if nb == 1 else 1)


def _attention_group(src, g, batch, seq, col0):
    d = ATTN_DILATIONS[g]
    n_sub = seq // d
    nb = n_sub // ATTN_BLK
    spec = lambda c: pl.BlockSpec((None, d, n_sub, GROUP_W), lambda b: (b, 0, 0, c))
    o, st = pl.pallas_call(
        functools.partial(_attn_kernel, d=d, nb=nb),
        grid=(batch,),
        in_specs=[spec(col0), spec(col0 + 1), spec(col0 + 2)],
        out_specs=[pl.BlockSpec((HEADS_PER_GROUP, seq, HEAD_DIM), lambda b: (0, b, 0)),
                   pl.BlockSpec((seq, LANE), lambda b: (b, 0))],
        out_shape=[jax.ShapeDtypeStruct((HEADS_PER_GROUP, batch * seq, HEAD_DIM), F32),
                   jax.ShapeDtypeStruct((batch * seq, LANE), F32)],
        compiler_params=_cp(("arbitrary",)),
        name=f"attn_d{d}",
    )(src, src, src)
    return o, st


def _ssd_kernel(xbc_ref, z_ref, dt_ref, cw_ref, cb_ref, dtb_ref, alog_ref, dsk_ref, nw_ref,
                out_ref, xwin, ystage, state):
    L = SSM_CHUNK
    inner = SSM_HEADS * SSM_HEADDIM
    gw = SSM_STATE
    c = pl.program_id(1)

    @pl.when(c == 0)
    def _():
        xwin[0:L, :] = jnp.zeros((L, xwin.shape[1]), BF16)
        state[...] = jnp.zeros(state.shape, F32)

    xcur = xbc_ref[...]
    xwin[L:2 * L, :] = xcur
    win = xwin[...]
    srow = lax.broadcasted_iota(I32, (L, 2 * L), 0)
    scol = lax.broadcasted_iota(I32, (L, 2 * L), 1)
    conv = cb_ref[...] + cw_ref[SSM_CONV - 1:SSM_CONV, :] * xcur.astype(F32)
    for s in range(1, SSM_CONV):
        shift_m = jnp.where(scol == srow + (L - s), 1.0, 0.0).astype(BF16)
        conv = conv + cw_ref[SSM_CONV - 1 - s:SSM_CONV - s, :] * jnp.dot(shift_m, win, preferred_element_type=F32)
    xwin[0:L, :] = xcur
    act = _silu(conv)
    xs = act[:, :inner]
    xs_b = xs.astype(BF16)

    lane = lax.broadcasted_iota(I32, (L, LANE), 1)
    row = lax.broadcasted_iota(I32, (L, LANE), 0)
    dtr = dt_ref[...] + dtb_ref[...]
    dt = jnp.maximum(dtr, 0.0) + jnp.log(1.0 + jnp.exp(-jnp.abs(dtr)))
    a_neg = jnp.where(lane < SSM_HEADS, -jnp.exp(alog_ref[...]), 0.0)
    a = dt * a_neg
    cs = a
    shift = 1
    while shift < L:
        cs = cs + jnp.where(row >= shift, pltpu.roll(cs, shift, 0), 0.0)
        shift *= 2
    cs_t = cs.T
    dt_t = dt.T
    tri = row >= lane
    half = lane < SSM_HEADDIM
    zero_b = jnp.zeros((L, LANE), BF16)

    for g in range(SSM_GROUPS):
        bg = act[:, inner + g * gw:inner + (g + 1) * gw]
        cg = act[:, inner + SSM_GROUPS * gw + g * gw:inner + SSM_GROUPS * gw + (g + 1) * gw]
        cg_b = cg.astype(BF16)
        cb = lax.dot_general(cg_b, bg.astype(BF16), (((1,), (1,)), ((), ())), preferred_element_type=F32)
        bg_t = bg.T
        for pair in range(2):
            h0 = g * 4 + pair * 2
            pidx = h0 // 2
            xpp = xs_b[:, pidx * LANE:(pidx + 1) * LANE]
            rhs = jnp.concatenate([jnp.where(half, xpp, zero_b), jnp.where(half, zero_b, xpp)], axis=0)
            dec, dst, eoff, cdec = [], [], [], []
            for h in (h0, h0 + 1):
                cs_col = cs[:, h:h + 1]
                cs_row = cs_t[h:h + 1, :]
                dt_row = dt_t[h:h + 1, :]
                dec.append(cb * (jnp.exp(jnp.where(tri, cs_col - cs_row, NEG)) * dt_row))
                cs_last = cs_row[:, L - 1:L]
                dst.append(bg_t * (jnp.exp(cs_last - cs_row) * dt_row))
                eoff.append(jnp.exp(cs_col))
                cdec.append(jnp.exp(cs_last))
            y_diag = jnp.dot(jnp.concatenate(dec, axis=1).astype(BF16), rhs, preferred_element_type=F32)
            st_new = jnp.dot(jnp.concatenate(dst, axis=1).astype(BF16), rhs, preferred_element_type=F32)
            prev = state[pidx]
            y_off = jnp.dot(cg_b, prev.astype(BF16), preferred_element_type=F32)
            y_off = y_off * jnp.where(half, eoff[0], eoff[1])
            state[pidx] = prev * jnp.where(half, cdec[0], cdec[1]) + st_new
            y = y_diag + y_off + dsk_ref[:, pidx * LANE:(pidx + 1) * LANE] * xs[:, pidx * LANE:(pidx + 1) * LANE]
            out_pair = y * _silu(z_ref[:, pidx * LANE:(pidx + 1) * LANE].astype(F32))
            ystage[:, pidx * LANE:(pidx + 1) * LANE] = out_pair

    gsz = inner // SSM_GROUPS
    for g in range(SSM_GROUPS):
        yg = ystage[:, g * gsz:(g + 1) * gsz]
        ms = jnp.mean(yg * yg, axis=-1, keepdims=True)
        out_ref[:, g * gsz:(g + 1) * gsz] = (yg * lax.rsqrt(ms + NORM_EPS) * nw_ref[:, g * gsz:(g + 1) * gsz]).astype(BF16)


def _ssd(proj, dt_raw, conv_w, conv_b, dt_bias, a_log, d_skip, ssm_norm_w, batch, seq):
    t = batch * seq
    L = SSM_CHUNK
    nc = seq // L
    inner = SSM_HEADS * SSM_HEADDIM
    cdim = conv_w.shape[1]
    pad16 = lambda v: jnp.pad(v.astype(F32), (0, LANE - SSM_HEADS)).reshape(1, LANE)
    dsk = jnp.repeat(d_skip.astype(F32), SSM_HEADDIM).reshape(1, inner)
    rowc = lambda b, c: (b * nc + c, 0)
    const = lambda b, c: (0, 0)
    return pl.pallas_call(
        _ssd_kernel,
        grid=(batch, nc),
        in_specs=[pl.BlockSpec((L, cdim), lambda b, c: (b * nc + c, COL_XBC // cdim)),
                  pl.BlockSpec((L, inner), lambda b, c: (b * nc + c, COL_Z // inner)),
                  pl.BlockSpec((L, LANE), rowc),
                  pl.BlockSpec((SSM_CONV, cdim), const),
                  pl.BlockSpec((1, cdim), const),
                  pl.BlockSpec((1, LANE), const),
                  pl.BlockSpec((1, LANE), const),
                  pl.BlockSpec((1, inner), const),
                  pl.BlockSpec((1, inner), const)],
        out_specs=pl.BlockSpec((L, inner), rowc),
        out_shape=jax.ShapeDtypeStruct((t, inner), BF16),
        scratch_shapes=[pltpu.VMEM((2 * L, cdim), BF16), pltpu.VMEM((L, inner), F32),
                        pltpu.VMEM((SSM_HEADS // 2, SSM_STATE, 2 * SSM_HEADDIM), F32)],
        compiler_params=_cp(("arbitrary", "arbitrary")),
        name="ssd",
    )(proj, proj, dt_raw, conv_w.astype(F32), conv_b.reshape(1, cdim).astype(F32), pad16(dt_bias), pad16(a_log),
      dsk, ssm_norm_w.reshape(1, inner).astype(F32))


def _mix_kernel(o0_ref, o1_ref, o2_ref, s0_ref, s1_ref, s2_ref, ssm_ref, ga_ref, gs_ref, x_ref,
                g1_ref, sc2_ref, sh2_ref, g2_ref, nw_ref, wba_ref, wbs_ref, wo_ref, wrt_ref,
                wgs_ref, wus_ref, wds_ref, base_ref, u2p_ref, sct_ref):
    o_refs = (o0_ref, o1_ref, o2_ref)
    s_refs = (s0_ref, s1_ref, s2_ref)
    tm = x_ref.shape[0]
    sub = tm // MIX_SPLIT
    for part in range(MIX_SPLIT):
        rs = slice(part * sub, (part + 1) * sub)
        heads = []
        for h in range(HEADS_PER_GROUP):
            ms = [s[rs, h:h + 1] for s in s_refs]
            ls = [s[rs, HEADS_PER_GROUP + h:HEADS_PER_GROUP + h + 1] for s in s_refs]
            mx = jnp.maximum(jnp.maximum(ms[0], ms[1]), ms[2])
            wts = [l * jnp.exp(m - mx) for m, l in zip(ms, ls)]
            num = wts[0] * o_refs[0][h, rs, :] + wts[1] * o_refs[1][h, rs, :] + wts[2] * o_refs[2][h, rs, :]
            heads.append((num / (wts[0] + wts[1] + wts[2])).astype(BF16))
        attn = jnp.concatenate(heads, axis=1)
        ya = jnp.dot(attn, wba_ref[...], preferred_element_type=F32)
        ys = jnp.dot(ssm_ref[rs, :], wbs_ref[...], preferred_element_type=F32)
        merged = _sigmoid(ga_ref[rs, :].astype(F32)) * ya + _sigmoid(gs_ref[rs, :].astype(F32)) * ys
        mix = jnp.dot(merged.astype(BF16), wo_ref[...], preferred_element_type=F32)
        h1 = x_ref[rs, :] + g1_ref[...] * mix
        ms2 = jnp.mean(h1 * h1, axis=-1, keepdims=True)
        u2 = h1 * lax.rsqrt(ms2 + NORM_EPS) * nw_ref[...] * (1.0 + sc2_ref[...]) + sh2_ref[...]
        for cc, chunk in enumerate(_pack_row_chunks(u2)):
            u2p_ref[pl.ds(part * sub * ROW_PACK + cc, sub, stride=ROW_PACK), :] = chunk
        u2b = u2.astype(BF16)
        u2lo = (u2 - u2b.astype(F32)).astype(BF16)
        nt = (((1,), (1,)), ((), ()))
        logits_t = (lax.dot_general(wrt_ref[0], u2b, nt, preferred_element_type=F32)
                    + lax.dot_general(wrt_ref[0], u2lo, nt, preferred_element_type=F32)
                    + lax.dot_general(wrt_ref[1], u2b, nt, preferred_element_type=F32))
        sct_ref[:, rs] = _sigmoid(logits_t)
        hs_ = (_silu(jnp.dot(u2b, wgs_ref[...], preferred_element_type=F32))
               * jnp.dot(u2b, wus_ref[...], preferred_element_type=F32))
        shared = jnp.dot(hs_.astype(BF16), wds_ref[...], preferred_element_type=F32)
        base_ref[rs, :] = h1 + g2_ref[...] * shared


def _mix(o_list, st_list, ssm, proj, x2, g1, sc2, sh2, g2, norm_w, w_ba, w_bs, w_o, w_rt, w_gs, w_us, w_ds, seq):
    t, d = x2.shape
    tm = MIX_TM
    per_b = seq // tm
    row = lambda i: (i, 0)
    modrow = lambda i: (i // per_b, 0, 0)
    const = lambda i: (0, 0)
    full = lambda a: pl.BlockSpec(a.shape, lambda i: (0,) * a.ndim, pipeline_mode=pl.Buffered(1))
    mod = pl.BlockSpec((None, 1, d), modrow)
    return pl.pallas_call(
        _mix_kernel,
        grid=(t // tm,),
        in_specs=[pl.BlockSpec((HEADS_PER_GROUP, tm, HEAD_DIM), lambda i: (0, i, 0))] * 3
        + [pl.BlockSpec((tm, LANE), row)] * 3 + [
            pl.BlockSpec((tm, d), row),
            pl.BlockSpec((tm, d), lambda i: (i, COL_GA // d)),
            pl.BlockSpec((tm, d), lambda i: (i, COL_GS // d)),
            pl.BlockSpec((tm, d), row),
            mod, mod, mod, mod, full(norm_w), full(w_ba), full(w_bs), full(w_o), full(w_rt),
            full(w_gs), full(w_us), full(w_ds)],
        out_specs=[pl.BlockSpec((tm, d), row), pl.BlockSpec((tm * ROW_PACK, LANE), row),
                   pl.BlockSpec((N_EXPERTS, tm), lambda i: (0, i))],
        out_shape=[jax.ShapeDtypeStruct((t, d), F32), jax.ShapeDtypeStruct((t * ROW_PACK, LANE), U32),
                   jax.ShapeDtypeStruct((N_EXPERTS, t), F32)],
        compiler_params=_cp(("arbitrary",)),
        name="mix",
    )(*o_list, *st_list, ssm, proj, proj, x2, g1, sc2, sh2, g2, norm_w, w_ba, w_bs, w_o, w_rt, w_gs, w_us, w_ds)


def _route_kernel(sct_ref, bias_ref, idx_ref, w_ref, mem_ref, cnt_ref, run):
    i = pl.program_id(0)
    tm = sct_ref.shape[1]
    per_g = N_EXPERTS // N_EXPERT_GROUPS

    @pl.when(i == 0)
    def _():
        run[...] = jnp.zeros(run.shape, F32)

    s = sct_ref[...]
    biased = s + bias_ref[...]
    io_g = lax.broadcasted_iota(I32, (per_g, tm), 0).astype(F32)
    gscore = []
    for g in range(N_EXPERT_GROUPS):
        bgp = biased[g * per_g:(g + 1) * per_g, :]
        m1 = jnp.max(bgp, axis=0, keepdims=True)
        first = jnp.min(jnp.where(bgp == m1, io_g, float(per_g)), axis=0, keepdims=True)
        m2 = jnp.max(jnp.where(io_g == first, NEG, bgp), axis=0, keepdims=True)
        gscore.append(m1 + m2)
    gs = jnp.concatenate(gscore, axis=0)
    io8 = lax.broadcasted_iota(I32, (N_EXPERT_GROUPS, tm), 0).astype(F32)
    gsel = jnp.zeros((N_EXPERT_GROUPS, tm), F32)
    cur = gs
    for _ in range(TOPK_GROUPS):
        mx = jnp.max(cur, axis=0, keepdims=True)
        fi = jnp.min(jnp.where(cur == mx, io8, float(N_EXPERT_GROUPS)), axis=0, keepdims=True)
        hit = io8 == fi
        gsel = jnp.where(hit, 1.0, gsel)
        cur = jnp.where(hit, NEG, cur)
    masked = jnp.concatenate(
        [jnp.where(gsel[g:g + 1, :] > 0.5, biased[g * per_g:(g + 1) * per_g, :], NEG) for g in range(N_EXPERT_GROUPS)],
        axis=0)
    io_e = lax.broadcasted_iota(I32, (N_EXPERTS, tm), 0).astype(F32)
    member = jnp.zeros((N_EXPERTS, tm), F32)
    idxs, ws = [], []
    for _ in range(TOP_K):
        mx = jnp.max(masked, axis=0, keepdims=True)
        fi = jnp.min(jnp.where(masked == mx, io_e, float(N_EXPERTS)), axis=0, keepdims=True)
        hit = io_e == fi
        idxs.append(fi)
        ws.append(jnp.sum(jnp.where(hit, s, 0.0), axis=0, keepdims=True))
        member = jnp.where(hit, 1.0, member)
        masked = jnp.where(hit, NEG, masked)
    wsum = ws[0]
    for k in range(1, TOP_K):
        wsum = wsum + ws[k]
    idx_ref[...] = jnp.concatenate(idxs, axis=0).astype(I32)
    w_ref[...] = jnp.concatenate([w / wsum * ROUTED_SCALE for w in ws], axis=0)
    mem_ref[...] = member.astype(BF16)
    new_run = run[...] + jnp.sum(member, axis=1, keepdims=True)
    run[...] = new_run
    cnt_ref[...] = new_run


def _route(scores_t, router_bias):
    e, t = scores_t.shape
    tm = ROUTE_TM
    tok = pl.BlockSpec((TOP_K, tm), lambda i: (0, i))
    return pl.pallas_call(
        _route_kernel,
        grid=(t // tm,),
        in_specs=[pl.BlockSpec((e, tm), lambda i: (0, i)), pl.BlockSpec((e, 1), lambda i: (0, 0))],
        out_specs=[tok, tok, pl.BlockSpec((e, tm), lambda i: (0, i)), pl.BlockSpec((e, LANE), lambda i: (0, 0))],
        out_shape=[jax.ShapeDtypeStruct((TOP_K, t), I32), jax.ShapeDtypeStruct((TOP_K, t), F32),
                   jax.ShapeDtypeStruct((e, t), BF16), jax.ShapeDtypeStruct((e, LANE), F32)],
        scratch_shapes=[pltpu.VMEM((e, LANE), F32)],
        compiler_params=_cp(("arbitrary",)),
        name="route",
    )(scores_t, router_bias.reshape(e, 1).astype(F32))


def _dest_kernel(mem_ref, idx_ref, start_ref, dest_ref, run):
    i = pl.program_id(0)
    e, tm = mem_ref.shape

    @pl.when(i == 0)
    def _():
        run[...] = jnp.broadcast_to(start_ref[...], run.shape)

    member = mem_ref[...]
    tr = lax.broadcasted_iota(I32, (tm, tm), 0)
    tc = lax.broadcasted_iota(I32, (tm, tm), 1)
    upper = jnp.where(tr < tc, 1.0, 0.0).astype(BF16)
    rank_full = jnp.dot(member, upper, preferred_element_type=F32) + run[:, 0:1]
    io_e = lax.broadcasted_iota(I32, (e, tm), 0)
    idx = idx_ref[...]
    rows = [jnp.sum(jnp.where(io_e == idx[k:k + 1, :], rank_full, 0.0), axis=0, keepdims=True) for k in range(TOP_K)]
    dest_ref[...] = jnp.concatenate(rows, axis=0).astype(I32)
    run[...] = run[...] + jnp.sum(member.astype(F32), axis=1, keepdims=True)


def _dest(member, idx, pad_start):
    e, t = member.shape
    tm = ROUTE_TM
    return pl.pallas_call(
        _dest_kernel,
        grid=(t // tm,),
        in_specs=[pl.BlockSpec((e, tm), lambda i: (0, i)), pl.BlockSpec((TOP_K, tm), lambda i: (0, i)),
                  pl.BlockSpec((e, 1), lambda i: (0, 0))],
        out_specs=pl.BlockSpec((TOP_K, tm), lambda i: (0, i)),
        out_shape=jax.ShapeDtypeStruct((TOP_K, t), I32),
        scratch_shapes=[pltpu.VMEM((e, LANE), F32)],
        compiler_params=_cp(("arbitrary",)),
        name="dest",
    )(member, idx, pad_start.astype(F32).reshape(e, 1))


def _dispatch_kernel(pend_ref, npad_ref, dest_ref, u_ref, xs_ref, zero_scr, sem, zsem):
    i = pl.program_id(0)
    rt = ROW_PACK
    tm = u_ref.shape[0] // rt
    zrows = zero_scr.shape[0]

    @pl.when(i == 0)
    def _():
        zero_scr[...] = jnp.zeros(zero_scr.shape, zero_scr.dtype)

        def zcopy(e):
            start = pl.multiple_of(pend_ref[e] * rt - zrows, zrows)
            return pltpu.make_async_copy(zero_scr, xs_ref.at[pl.ds(start, zrows)], zsem)

        def zstart(e, c):
            @pl.when(npad_ref[e] > 0)
            def _():
                zcopy(e).start()
            return c

        def zwait(e, c):
            @pl.when(npad_ref[e] > 0)
            def _():
                zcopy(e).wait()
            return c

        lax.fori_loop(0, N_EXPERTS, zstart, 0)
        lax.fori_loop(0, N_EXPERTS, zwait, 0)

    def row_copy(t, k):
        src = u_ref.at[pl.ds(pl.multiple_of(t * rt, rt), rt)]
        dst = xs_ref.at[pl.ds(pl.multiple_of(dest_ref[t * TOP_K + k] * rt, rt), rt)]
        return pltpu.make_async_copy(src, dst, sem)

    def start(t, c):
        for k in range(TOP_K):
            row_copy(t, k).start(priority=k % 2)
        return c

    def wait(t, c):
        for k in range(TOP_K):
            row_copy(t, k).wait()
        return c

    lax.fori_loop(0, tm, start, 0)
    lax.fori_loop(0, tm, wait, 0)


def _dispatch(u2p, dest_flat, pad_end, padded, n_rows):
    rt = ROW_PACK
    t = u2p.shape[0] // rt
    tm = DISPATCH_TM
    gs = pltpu.PrefetchScalarGridSpec(
        num_scalar_prefetch=2,
        grid=(t // tm,),
        in_specs=[pl.BlockSpec((tm * TOP_K,), lambda i, pe, npd: (i,), memory_space=pltpu.SMEM),
                  pl.BlockSpec((tm * rt, LANE), lambda i, pe, npd: (i, 0))],
        out_specs=pl.BlockSpec(memory_space=pl.ANY),
        scratch_shapes=[pltpu.VMEM((MOE_ROWS * rt, LANE), U32), pltpu.SemaphoreType.DMA, pltpu.SemaphoreType.DMA],
    )
    return pl.pallas_call(
        _dispatch_kernel,
        grid_spec=gs,
        out_shape=jax.ShapeDtypeStruct((n_rows * rt, LANE), U32),
        compiler_params=_cp(("arbitrary",)),
        name="dispatch",
    )(pad_end, padded, dest_flat, u2p)


def _expert_kernel(bstart_ref, nblk_ref, wg_ref, wu_ref, wd_ref, xs_ref, ys_ref,
                   wg_b, wu_b, wd_b, xbuf, ybuf, in_sem, out_sem):
    e = pl.program_id(0)
    last = pl.num_programs(0) - 1
    rt = ROW_PACK
    n_in, rb = xbuf.shape[0], xbuf.shape[1] // rt
    n_out = ybuf.shape[0]
    n = nblk_ref[e]
    g0 = bstart_ref[e]
    total = bstart_ref[last] + nblk_ref[last]

    def block_rows(g):
        return pl.ds(pl.multiple_of(g * (rb * rt), rb * rt), rb * rt)

    def in_copy(g):
        return pltpu.make_async_copy(xs_ref.at[block_rows(g)], xbuf.at[g % n_in], in_sem.at[g % n_in])

    def out_copy(g):
        return pltpu.make_async_copy(ybuf.at[g % n_out], ys_ref.at[block_rows(g)], out_sem.at[g % n_out])

    @pl.when(e == 0)
    def _():
        for g in range(n_in - 1):
            @pl.when(g < total)
            def _():
                in_copy(g).start()

    wg_b[...] = wg_ref[...].astype(BF16)
    wu_b[...] = wu_ref[...].astype(BF16)
    wd_b[...] = wd_ref[...].astype(BF16)

    def body(b, carry):
        g = g0 + b
        in_copy(g).wait()

        @pl.when(g + n_in - 1 < total)
        def _():
            in_copy(g + n_in - 1).start()

        @pl.when(g >= n_out)
        def _():
            out_copy(g - n_out).wait()

        halves = [_unpack_pair(xbuf[g % n_in, pl.ds(cc, rb, stride=rt), :]) for cc in range(rt)]
        xb = jnp.concatenate([lo.astype(BF16) for lo, _ in halves] + [hi.astype(BF16) for _, hi in halves], axis=1)
        gate = jnp.dot(xb, wg_b[...], preferred_element_type=F32)
        up = jnp.dot(xb, wu_b[...], preferred_element_type=F32)
        h = (_silu(gate) * up).astype(BF16)
        y = jnp.dot(h, wd_b[...], preferred_element_type=F32)
        for cc, chunk in enumerate(_pack_row_chunks(y)):
            ybuf[g % n_out, pl.ds(cc, rb, stride=rt), :] = chunk
        out_copy(g).start()
        return carry

    lax.fori_loop(0, n, body, 0)

    @pl.when(e == last)
    def _():
        for j in range(n_out, 0, -1):
            @pl.when(total >= j)
            def _():
                out_copy(total - j).wait()


def _experts(xs, blk_start, nblk, w_gate_e, w_up_e, w_down_e):
    ne, d, ff = w_gate_e.shape
    rb = MOE_ROWS * ROW_PACK
    gs = pltpu.PrefetchScalarGridSpec(
        num_scalar_prefetch=2,
        grid=(ne,),
        in_specs=[pl.BlockSpec((None, d, ff), lambda e, ps, nb: (e, 0, 0)),
                  pl.BlockSpec((None, d, ff), lambda e, ps, nb: (e, 0, 0)),
                  pl.BlockSpec((None, ff, d), lambda e, ps, nb: (e, 0, 0)),
                  pl.BlockSpec(memory_space=pl.ANY)],
        out_specs=pl.BlockSpec(memory_space=pl.ANY),
        scratch_shapes=[pltpu.VMEM((d, ff), BF16), pltpu.VMEM((d, ff), BF16), pltpu.VMEM((ff, d), BF16),
                        pltpu.VMEM((EXPERT_IN_SLOTS, rb, LANE), U32), pltpu.VMEM((EXPERT_OUT_SLOTS, rb, LANE), U32),
                        pltpu.SemaphoreType.DMA((EXPERT_IN_SLOTS,)), pltpu.SemaphoreType.DMA((EXPERT_OUT_SLOTS,))],
    )
    return pl.pallas_call(
        _expert_kernel,
        grid_spec=gs,
        out_shape=jax.ShapeDtypeStruct(xs.shape, U32),
        compiler_params=_cp(("arbitrary",)),
        name="experts",
    )(blk_start, nblk, w_gate_e, w_up_e, w_down_e, xs)


def _combine_kernel(dest_ref, dnext_ref, wt_ref, base_ref, g2_ref, nfw_ref, ys_ref, out_ref, buf, sem):
    i = pl.program_id(0)
    n = pl.num_programs(0)
    tm = base_ref.shape[0]
    rt = ROW_PACK
    half = i % 2

    def row_copy(d_ref, h, t, k):
        src = ys_ref.at[pl.ds(pl.multiple_of(d_ref[t * TOP_K + k] * rt, rt), rt)]
        return pltpu.make_async_copy(src, buf.at[h, k, pl.ds(pl.multiple_of(t * rt, rt), rt)], sem.at[h])

    def issue(d_ref, h):
        def start(t, c):
            for k in range(TOP_K):
                row_copy(d_ref, h, t, k).start(priority=k % 2)
            return c
        lax.fori_loop(0, tm, start, 0)

    @pl.when(i == 0)
    def _():
        issue(dest_ref, 0)

    @pl.when(i + 1 < n)
    def _():
        issue(dnext_ref, 1 - half)

    def wait(t, c):
        for k in range(TOP_K):
            row_copy(dest_ref, half, t, k).wait()
        return c

    lax.fori_loop(0, tm, wait, 0)

    upper = lax.broadcasted_iota(I32, (2 * rt, LANE), 0) >= rt

    def weigh(t2, c):
        rows = pl.ds(pl.multiple_of(t2 * (2 * rt), 2 * rt), 2 * rt)
        acc_lo = jnp.zeros((2 * rt, LANE), F32)
        acc_hi = jnp.zeros((2 * rt, LANE), F32)
        for k in range(TOP_K):
            lo, hi = _unpack_pair(buf[half, k, rows, :])
            wv = jnp.where(upper, wt_ref[(2 * t2 + 1) * TOP_K + k], wt_ref[2 * t2 * TOP_K + k])
            acc_lo = acc_lo + lo * wv
            acc_hi = acc_hi + hi * wv
        buf[half, 0, rows, :] = lax.bitcast_convert_type(acc_lo, U32)
        buf[half, 1, rows, :] = lax.bitcast_convert_type(acc_hi, U32)
        return c

    lax.fori_loop(0, tm // 2, weigh, 0, unroll=2)
    ssq = jnp.zeros((tm, 1), F32)
    for cc in range(2 * rt):
        cs = slice(cc * LANE, (cc + 1) * LANE)
        routed = lax.bitcast_convert_type(buf[half, cc // rt, pl.ds(cc % rt, tm, stride=rt), :], F32)
        h2 = base_ref[:, cs] + g2_ref[:, cs] * routed
        out_ref[:, cs] = h2
        ssq = ssq + jnp.sum(h2 * h2, axis=-1, keepdims=True)
    inv = lax.rsqrt(ssq / (2 * rt * LANE) + NORM_EPS)
    out_ref[...] = out_ref[...] * inv * nfw_ref[...]


def _combine(ys, dest_flat, w_tok, base, g2, norm_final_w, seq):
    t, d = base.shape
    tm = COMBINE_TM
    per_b = seq // tm
    gs = pltpu.PrefetchScalarGridSpec(
        num_scalar_prefetch=0,
        grid=(t // tm,),
        in_specs=[pl.BlockSpec((tm * TOP_K,), lambda i: (i,), memory_space=pltpu.SMEM),
                  pl.BlockSpec((tm * TOP_K,), lambda i: (jnp.minimum(i + 1, t // tm - 1),), memory_space=pltpu.SMEM),
                  pl.BlockSpec((tm * TOP_K,), lambda i: (i,), memory_space=pltpu.SMEM),
                  pl.BlockSpec((tm, d), lambda i: (i, 0)),
                  pl.BlockSpec((None, 1, d), lambda i: (i // per_b, 0, 0)),
                  pl.BlockSpec((1, d), lambda i: (0, 0)),
                  pl.BlockSpec(memory_space=pl.ANY)],
        out_specs=pl.BlockSpec((tm, d), lambda i: (i, 0)),
        scratch_shapes=[pltpu.VMEM((2, TOP_K, tm * ROW_PACK, LANE), U32), pltpu.SemaphoreType.DMA((2,))],
    )
    return pl.pallas_call(
        _combine_kernel,
        grid_spec=gs,
        out_shape=jax.ShapeDtypeStruct((t, d), F32),
        compiler_params=_cp(("arbitrary",)),
        name="combine",
    )(dest_flat, dest_flat, w_tok, base, g2, norm_final_w, ys)


def kernel(x, c, positions, w_mod, b_mod, norm_mix_w, norm_ffn_w, w_in, conv_w, conv_b, dt_bias, a_log, d_skip,
           ssm_norm_w, w_branch_attn, w_branch_ssm, w_out, w_router, router_bias, w_gate_e, w_up_e, w_down_e,
           w_gate_s, w_up_s, w_down_s, norm_final_w):
    batch, seq, d = x.shape
    t = batch * seq
    assert w_mod.shape[0] == 1, "one layer"
    assert d == 2 * ROW_PACK * LANE and seq % INPROJ_TM == 0 and INPROJ_TM % (ATTN_DILATIONS[-1] * 16) == 0

    mod = _modulation(c, w_mod[0], b_mod[0])
    sh1, sc1, g1, sh2, sc2, g2 = [m.reshape(batch, 1, d) for m in jnp.split(mod, 6, axis=-1)]
    rope_c, rope_s1, rope_s2 = _rope_tables(positions)

    wi = w_in[0]
    q_dim = 3 * GROUP_W
    o_z = 3 * q_dim
    o_xbc = o_z + d
    o_dt = o_xbc + conv_w.shape[2]
    o_g = o_dt + SSM_HEADS
    qkv = lambda g: [wi[:, s * q_dim + g * GROUP_W:s * q_dim + (g + 1) * GROUP_W] for s in range(3)]
    w_packed = jnp.concatenate([wi[:, o_xbc:o_dt], wi[:, o_g:], wi[:, o_z:o_xbc]] + qkv(0) + qkv(1) + qkv(2),
                               axis=1).astype(BF16)
    assert w_packed.shape[1] == MAIN_W + 2 * QKV_W
    w_dt = jnp.pad(wi[:, o_dt:o_g], ((0, 0), (0, LANE - SSM_HEADS))).astype(BF16)

    x2 = x.reshape(t, d)
    proj, qkv1, qkv2, dt_raw = _inproj(x2, sc1, sh1, norm_mix_w.reshape(1, d), w_packed, w_dt,
                                       rope_c, rope_s1, rope_s2, batch, seq)

    srcs = [(proj.reshape(batch, 1, seq, MAIN_W), COL_Q0 // GROUP_W), (qkv1, 0), (qkv2, 0)]
    o_list, st_list = [], []
    for g, (src, col0) in enumerate(srcs):
        o, st = _attention_group(src, g, batch, seq, col0)
        o_list.append(o)
        st_list.append(st)
    ssm = _ssd(proj, dt_raw, conv_w[0], conv_b[0], dt_bias[0], a_log[0], d_skip[0], ssm_norm_w[0], batch, seq)

    wr_t = w_router[0].T
    wr_hi = wr_t.astype(BF16)
    w_rt = jnp.stack([wr_hi, (wr_t - wr_hi.astype(F32)).astype(BF16)])
    base, u2p, scores_t = _mix(
        o_list, st_list, ssm, proj, x2, g1, sc2, sh2, g2, norm_ffn_w.reshape(1, d),
        w_branch_attn[0].astype(BF16), w_branch_ssm[0].astype(BF16), w_out[0].astype(BF16),
        w_rt, w_gate_s[0].astype(BF16), w_up_s[0].astype(BF16), w_down_s[0].astype(BF16), seq)

    idx, w_sel, member, counts = _route(scores_t, router_bias[0])

    rb = MOE_ROWS
    cnt = counts[:, 0].astype(I32)
    padded = (cnt + rb - 1) // rb * rb
    pad_end = jnp.cumsum(padded).astype(I32)
    pad_start = pad_end - padded
    n_rows = t * TOP_K + N_EXPERTS * rb

    dest_flat = _dest(member, idx, pad_start).T.reshape(t * TOP_K)
    xs = _dispatch(u2p, dest_flat, pad_end, padded, n_rows)
    ys = _experts(xs, pad_start // rb, padded // rb, w_gate_e[0], w_up_e[0], w_down_e[0])
    out = _combine(ys, dest_flat, w_sel.T.reshape(t * TOP_K), base, g2, norm_final_w.reshape(1, d), seq)
    return out.reshape(batch, seq, d)
```

```python
import functools
import math

import jax
import jax.numpy as jnp
from jax import lax
from jax.experimental import pallas as pl
from jax.experimental.pallas import tpu as pltpu

F32 = jnp.float32
BF16 = jnp.bfloat16
I32 = jnp.int32
U32 = jnp.uint32

LANE = 128
VMEM_LIMIT = 56 * 1024 * 1024

HEAD_DIM = 128
HEADS_PER_GROUP = 4
GROUP_W = HEADS_PER_GROUP * HEAD_DIM
ATTN_DILATIONS = (1, 4, 16)
ATTN_BLK = 128
ROPE_DIM = 32
ROPE_HALF = 16
ROPE_THETA = 500000.0
SSM_HEADS = 16
SSM_HEADDIM = 64
SSM_GROUPS = 4
SSM_STATE = 128
SSM_CONV = 4
SSM_CHUNK = 128
N_EXPERTS = 256
TOP_K = 8
N_EXPERT_GROUPS = 8
TOPK_GROUPS = 4
ROUTED_SCALE = 2.5
NORM_EPS = 1e-6
NEG = -1e30

COL_XBC, COL_GA, COL_GS, COL_Z, COL_Q0 = 0, 2048, 3072, 4096, 5120
MAIN_W = 6656
QKV_W = 3 * GROUP_W

ROPE_TM = 2048
INPROJ_TM, INPROJ_TN = 512, 512
MIX_TM, MIX_SPLIT = 512, 2
ROUTE_TM = 512
MOE_ROWS = 256
EXPERT_IN_SLOTS, EXPERT_OUT_SLOTS = 6, 4
DISPATCH_TM = 256
COMBINE_TM = 256
ROW_PACK = 4


def _cp(sem, vmem=VMEM_LIMIT):
    return pltpu.CompilerParams(dimension_semantics=sem, vmem_limit_bytes=vmem)


def _sigmoid(x):
    return 1.0 / (1.0 + jnp.exp(-x))


def _silu(x):
    return x * _sigmoid(x)


def _pack_pair(a, b):
    ua = lax.bitcast_convert_type(a.astype(BF16).astype(F32), U32)
    ub = lax.bitcast_convert_type(b.astype(BF16).astype(F32), U32)
    return (ua >> 16) | ub


def _unpack_pair(w):
    lo = lax.bitcast_convert_type(w << 16, F32)
    hi = lax.bitcast_convert_type(w & jnp.uint32(0xFFFF0000), F32)
    return lo, hi


def _pack_row_chunks(v):
    return [_pack_pair(v[:, c * LANE:(c + 1) * LANE], v[:, (c + ROW_PACK) * LANE:(c + ROW_PACK + 1) * LANE])
            for c in range(ROW_PACK)]


def _mod_kernel(c_ref, w_ref, b_ref, o_ref):
    cond = _silu(c_ref[...])
    o_ref[...] = jnp.dot(cond, w_ref[...], preferred_element_type=F32) + b_ref[...]


def _modulation(c, w_mod, b_mod):
    b, d = c.shape
    n = w_mod.shape[1]
    return pl.pallas_call(
        _mod_kernel,
        grid=(n // d,),
        in_specs=[pl.BlockSpec((b, d), lambda j: (0, 0)),
                  pl.BlockSpec((d, d), lambda j: (0, j)),
                  pl.BlockSpec((1, d), lambda j: (0, j))],
        out_specs=pl.BlockSpec((b, d), lambda j: (0, j)),
        out_shape=jax.ShapeDtypeStruct((b, n), F32),
        compiler_params=_cp(("arbitrary",)),
        name="modulation",
    )(c, w_mod, b_mod.reshape(1, n))


def _rope_kernel(pos_ref, inv_ref, c_ref, s1_ref, s2_ref):
    ang = pos_ref[...].astype(F32) * inv_ref[...]
    lane = lax.broadcasted_iota(I32, ang.shape, 1)
    cos = jnp.cos(ang)
    sin = jnp.sin(ang)
    c_ref[...] = jnp.where(lane < ROPE_DIM, cos, 1.0)
    s1_ref[...] = jnp.where(lane < ROPE_HALF, -sin, 0.0)
    s2_ref[...] = jnp.where((lane >= ROPE_HALF) & (lane < ROPE_DIM), sin, 0.0)


def _rope_tables(positions):
    t = positions.size
    tm = ROPE_TM
    inv_freq = ROPE_THETA ** (-jnp.arange(ROPE_HALF, dtype=F32) / ROPE_HALF)
    inv_row = jnp.concatenate([inv_freq, inv_freq, jnp.zeros((LANE - ROPE_DIM,), F32)]).reshape(1, LANE)
    spec = pl.BlockSpec((tm, LANE), lambda i: (i, 0))
    shp = jax.ShapeDtypeStruct((t, LANE), F32)
    return pl.pallas_call(
        _rope_kernel,
        grid=(t // tm,),
        in_specs=[pl.BlockSpec((tm, 1), lambda i: (i, 0)), pl.BlockSpec((1, LANE), lambda i: (0, 0))],
        out_specs=[spec, spec, spec],
        out_shape=[shp, shp, shp],
        compiler_params=_cp(("arbitrary",)),
        name="rope_tables",
    )(positions.reshape(t, 1), inv_row)


def _inproj_kernel(x_ref, sc_ref, sh_ref, nw_ref, w_ref, wdt_ref, c_ref, s1_ref, s2_ref,
                   main_ref, g1_ref, g2_ref, dt_ref, u_scr, rope_scr, uc_scr, *, tn, q_scale):
    tm = x_ref.shape[0]
    n_main = MAIN_W // tn
    n_qkv = QKV_W // tn

    x = x_ref[...]
    ms = jnp.mean(x * x, axis=-1, keepdims=True)
    y = x * lax.rsqrt(ms + NORM_EPS) * nw_ref[...]
    uf = y * (1.0 + sc_ref[...]) + sh_ref[...]
    u = uf.astype(BF16)
    dt_ref[...] = jnp.dot(u, wdt_ref[...], preferred_element_type=F32)
    n_chunk = uf.shape[1] // LANE
    for cc in range(n_chunk):
        uc_scr[cc] = uf[:, cc * LANE:(cc + 1) * LANE]
    for o, d in enumerate(ATTN_DILATIONS[1:]):
        rows = tm // d
        for r in range(d):
            for cc in range(n_chunk):
                u_scr[o, r * rows:(r + 1) * rows, cc * LANE:(cc + 1) * LANE] = (
                    uc_scr[cc, pl.ds(r, rows, stride=d), :].astype(BF16))
            for ti, tab in enumerate((c_ref, s1_ref, s2_ref)):
                rope_scr[o, ti, r * rows:(r + 1) * rows, :] = tab[pl.ds(r, rows, stride=d), :]

    for c in range(n_main + 2 * n_qkv):
        order = 0 if c < n_main else (1 if c < n_main + n_qkv else 2)
        jq = c - COL_Q0 // tn if order == 0 else (c - n_main - (order - 1) * n_qkv)
        lhs = u if order == 0 else u_scr[order - 1]
        acc = jnp.dot(lhs, w_ref[:, c * tn:(c + 1) * tn], preferred_element_type=F32)
        if jq in (0, 1):
            scale = q_scale if jq == 0 else 1.0
            tabs = (c_ref, s1_ref, s2_ref) if order == 0 else tuple(rope_scr.at[order - 1, ti] for ti in range(3))
            cs = tabs[0][...] * scale
            s1 = tabs[1][...] * scale
            s2 = tabs[2][...] * scale
            parts = []
            for h in range(tn // HEAD_DIM):
                a = acc[:, h * HEAD_DIM:(h + 1) * HEAD_DIM]
                parts.append(a * cs + pltpu.roll(a, LANE - ROPE_HALF, 1) * s1 + pltpu.roll(a, ROPE_HALF, 1) * s2)
            acc = jnp.concatenate(parts, axis=1)
        val = acc.astype(BF16)
        if order == 0:
            main_ref[:, c * tn:(c + 1) * tn] = val
        else:
            dst = g1_ref if order == 1 else g2_ref
            c0 = (c - n_main - (order - 1) * n_qkv) * tn
            dst[:, :, c0:c0 + tn] = val.reshape(dst.shape[0], dst.shape[1], tn)


def _inproj(x2, sc1, sh1, norm_w, w_packed, w_dt, rope_c, rope_s1, rope_s2, batch, seq):
    t, d = x2.shape
    tm, tn = INPROJ_TM, INPROJ_TN
    n = w_packed.shape[1]
    per_b = seq // tm
    d1, d2 = ATTN_DILATIONS[1], ATTN_DILATIONS[2]
    row = lambda i: (i, 0)
    modrow = lambda i: (i // per_b, 0, 0)
    const = lambda i: (0, 0)
    resident = pl.Buffered(1)
    qkv_map = lambda i: (i // per_b, 0, i % per_b, 0)
    return pl.pallas_call(
        functools.partial(_inproj_kernel, tn=tn, q_scale=1.0 / math.sqrt(HEAD_DIM)),
        grid=(t // tm,),
        in_specs=[pl.BlockSpec((tm, d), row),
                  pl.BlockSpec((None, 1, d), modrow),
                  pl.BlockSpec((None, 1, d), modrow),
                  pl.BlockSpec((1, d), const),
                  pl.BlockSpec((d, n), const, pipeline_mode=resident),
                  pl.BlockSpec((d, LANE), const),
                  pl.BlockSpec((tm, LANE), row),
                  pl.BlockSpec((tm, LANE), row),
                  pl.BlockSpec((tm, LANE), row)],
        out_specs=[pl.BlockSpec((tm, MAIN_W), row),
                   pl.BlockSpec((None, d1, tm // d1, QKV_W), qkv_map),
                   pl.BlockSpec((None, d2, tm // d2, QKV_W), qkv_map),
                   pl.BlockSpec((tm, LANE), row)],
        out_shape=[jax.ShapeDtypeStruct((t, MAIN_W), BF16),
                   jax.ShapeDtypeStruct((batch, d1, seq // d1, QKV_W), BF16),
                   jax.ShapeDtypeStruct((batch, d2, seq // d2, QKV_W), BF16),
                   jax.ShapeDtypeStruct((t, LANE), F32)],
        scratch_shapes=[pltpu.VMEM((2, tm, d), BF16), pltpu.VMEM((2, 3, tm, LANE), F32),
                        pltpu.VMEM((d // LANE, tm, LANE), F32)],
        compiler_params=_cp(("arbitrary",)),
        name="inproj",
    )(x2, sc1, sh1, norm_w, w_packed, w_dt, rope_c, rope_s1, rope_s2)


def _attn_kernel(q_ref, k_ref, v_ref, o_ref, st_ref, *, d, nb):
    blk = ATTN_BLK
    qi = lax.broadcasted_iota(I32, (blk, 2 * blk), 0)
    kj = lax.broadcasted_iota(I32, (blk, 2 * blk), 1)
    band = (kj >= qi) & (kj <= qi + blk)
    qi1 = lax.broadcasted_iota(I32, (blk, blk), 0)
    kj1 = lax.broadcasted_iota(I32, (blk, blk), 1)
    causal = kj1 <= qi1
    lane = kj1

    hsl = [slice(h * HEAD_DIM, (h + 1) * HEAD_DIM) for h in range(HEADS_PER_GROUP)]

    def blocks(items, nk, mask):
        s = jnp.concatenate(
            [lax.dot_general(q_ref[r, pl.ds(q0, blk), hs], k_ref[r, pl.ds(k0, nk), hs], (((1,), (1,)), ((), ())),
                             preferred_element_type=F32) for r, q0, k0 in items for hs in hsl], axis=0)
        s = jnp.where(jnp.concatenate([mask] * (HEADS_PER_GROUP * len(items)), axis=0), s, NEG)
        m = jnp.max(s, axis=-1, keepdims=True)
        p = jnp.exp(s - m)
        l = jnp.sum(p, axis=-1, keepdims=True)
        pb = p.astype(BF16)
        for it, (r, q0, k0) in enumerate(items):
            rows = pl.ds(q0, blk) if d == 1 else pl.ds(q0 * d + r, blk, stride=d)
            st = jnp.zeros((blk, LANE), F32)
            for h, hs in enumerate(hsl):
                hr = slice((it * HEADS_PER_GROUP + h) * blk, (it * HEADS_PER_GROUP + h + 1) * blk)
                o = jnp.dot(pb[hr], v_ref[r, pl.ds(k0, nk), hs], preferred_element_type=F32)
                o_ref[h, rows, :] = o / l[hr]
                st = jnp.where(lane == h, m[hr], st)
                st = jnp.where(lane == HEADS_PER_GROUP + h, l[hr], st)
            st_ref[rows, :] = st

    def band_item(r, n):
        q0 = pl.multiple_of(n * blk, blk)
        return (r, q0, pl.multiple_of(q0 - blk, blk))

    if d == 1:
        blocks([(0, 0, 0)], blk, causal)

        def body(j, c):
            blocks([band_item(0, 1 + 2 * j), band_item(0, 2 + 2 * j)], 2 * blk, band)
            return c
        lax.fori_loop(0, (nb - 1) // 2, body, 0, unroll=2)
        if (nb - 1) % 2:
            blocks([band_item(0, nb - 1)], 2 * blk, band)
    else:
        def body(j, c):
            blocks([(2 * j, 0, 0), (2 * j + 1, 0, 0)], blk, causal)
            for n in range(1, nb):
                blocks([band_item(2 * j, n), band_item(2 * j + 1, n)], 2 * blk, band)
            return c
        lax.fori_loop(0, d // 2, body, 0, unroll=2 if nb == 1 else 1)


def _attention_group(src, g, batch, seq, col0):
    d = ATTN_DILATIONS[g]
    n_sub = seq // d
    nb = n_sub // ATTN_BLK
    spec = lambda c: pl.BlockSpec((None, d, n_sub, GROUP_W), lambda b: (b, 0, 0, c))
    o, st = pl.pallas_call(
        functools.partial(_attn_kernel, d=d, nb=nb),
        grid=(batch,),
        in_specs=[spec(col0), spec(col0 + 1), spec(col0 + 2)],
        out_specs=[pl.BlockSpec((HEADS_PER_GROUP, seq, HEAD_DIM), lambda b: (0, b, 0)),
                   pl.BlockSpec((seq, LANE), lambda b: (b, 0))],
        out_shape=[jax.ShapeDtypeStruct((HEADS_PER_GROUP, batch * seq, HEAD_DIM), F32),
                   jax.ShapeDtypeStruct((batch * seq, LANE), F32)],
        compiler_params=_cp(("arbitrary",)),
        name=f"attn_d{d}",
    )(src, src, src)
    return o, st


def _ssd_kernel(xbc_ref, z_ref, dt_ref, cw_ref, cb_ref, dtb_ref, alog_ref, dsk_ref, nw_ref,
                out_ref, xwin, ystage, state):
    L = SSM_CHUNK
    inner = SSM_HEADS * SSM_HEADDIM
    gw = SSM_STATE
    c = pl.program_id(1)

    @pl.when(c == 0)
    def _():
        xwin[0:L, :] = jnp.zeros((L, xwin.shape[1]), BF16)
        state[...] = jnp.zeros(state.shape, F32)

    xcur = xbc_ref[...]
    xwin[L:2 * L, :] = xcur
    win = xwin[...]
    srow = lax.broadcasted_iota(I32, (L, 2 * L), 0)
    scol = lax.broadcasted_iota(I32, (L, 2 * L), 1)
    conv = cb_ref[...] + cw_ref[SSM_CONV - 1:SSM_CONV, :] * xcur.astype(F32)
    for s in range(1, SSM_CONV):
        shift_m = jnp.where(scol == srow + (L - s), 1.0, 0.0).astype(BF16)
        conv = conv + cw_ref[SSM_CONV - 1 - s:SSM_CONV - s, :] * jnp.dot(shift_m, win, preferred_element_type=F32)
    xwin[0:L, :] = xcur
    act = _silu(conv)
    xs = act[:, :inner]
    xs_b = xs.astype(BF16)

    lane = lax.broadcasted_iota(I32, (L, LANE), 1)
    row = lax.broadcasted_iota(I32, (L, LANE), 0)
    dtr = dt_ref[...] + dtb_ref[...]
    dt = jnp.maximum(dtr, 0.0) + jnp.log(1.0 + jnp.exp(-jnp.abs(dtr)))
    a_neg = jnp.where(lane < SSM_HEADS, -jnp.exp(alog_ref[...]), 0.0)
    a = dt * a_neg
    cs = a
    shift = 1
    while shift < L:
        cs = cs + jnp.where(row >= shift, pltpu.roll(cs, shift, 0), 0.0)
        shift *= 2
    cs_t = cs.T
    dt_t = dt.T
    tri = row >= lane
    half = lane < SSM_HEADDIM
    zero_b = jnp.zeros((L, LANE), BF16)

    for g in range(SSM_GROUPS):
        bg = act[:, inner + g * gw:inner + (g + 1) * gw]
        cg = act[:, inner + SSM_GROUPS * gw + g * gw:inner + SSM_GROUPS * gw + (g + 1) * gw]
        cg_b = cg.astype(BF16)
        cb = lax.dot_general(cg_b, bg.astype(BF16), (((1,), (1,)), ((), ())), preferred_element_type=F32)
        bg_t = bg.T
        for pair in range(2):
            h0 = g * 4 + pair * 2
            pidx = h0 // 2
            xpp = xs_b[:, pidx * LANE:(pidx + 1) * LANE]
            rhs = jnp.concatenate([jnp.where(half, xpp, zero_b), jnp.where(half, zero_b, xpp)], axis=0)
            dec, dst, eoff, cdec = [], [], [], []
            for h in (h0, h0 + 1):
                cs_col = cs[:, h:h + 1]
                cs_row = cs_t[h:h + 1, :]
                dt_row = dt_t[h:h + 1, :]
                dec.append(cb * (jnp.exp(jnp.where(tri, cs_col - cs_row, NEG)) * dt_row))
                cs_last = cs_row[:, L - 1:L]
                dst.append(bg_t * (jnp.exp(cs_last - cs_row) * dt_row))
                eoff.append(jnp.exp(cs_col))
                cdec.append(jnp.exp(cs_last))
            y_diag = jnp.dot(jnp.concatenate(dec, axis=1).astype(BF16), rhs, preferred_element_type=F32)
            st_new = jnp.dot(jnp.concatenate(dst, axis=1).astype(BF16), rhs, preferred_element_type=F32)
            prev = state[pidx]
            y_off = jnp.dot(cg_b, prev.astype(BF16), preferred_element_type=F32)
            y_off = y_off * jnp.where(half, eoff[0], eoff[1])
            state[pidx] = prev * jnp.where(half, cdec[0], cdec[1]) + st_new
            y = y_diag + y_off + dsk_ref[:, pidx * LANE:(pidx + 1) * LANE] * xs[:, pidx * LANE:(pidx + 1) * LANE]
            out_pair = y * _silu(z_ref[:, pidx * LANE:(pidx + 1) * LANE].astype(F32))
            ystage[:, pidx * LANE:(pidx + 1) * LANE] = out_pair

    gsz = inner // SSM_GROUPS
    for g in range(SSM_GROUPS):
        yg = ystage[:, g * gsz:(g + 1) * gsz]
        ms = jnp.mean(yg * yg, axis=-1, keepdims=True)
        out_ref[:, g * gsz:(g + 1) * gsz] = (yg * lax.rsqrt(ms + NORM_EPS) * nw_ref[:, g * gsz:(g + 1) * gsz]).astype(BF16)


def _ssd(proj, dt_raw, conv_w, conv_b, dt_bias, a_log, d_skip, ssm_norm_w, batch, seq):
    t = batch * seq
    L = SSM_CHUNK
    nc = seq // L
    inner = SSM_HEADS * SSM_HEADDIM
    cdim = conv_w.shape[1]
    pad16 = lambda v: jnp.pad(v.astype(F32), (0, LANE - SSM_HEADS)).reshape(1, LANE)
    dsk = jnp.repeat(d_skip.astype(F32), SSM_HEADDIM).reshape(1, inner)
    rowc = lambda b, c: (b * nc + c, 0)
    const = lambda b, c: (0, 0)
    return pl.pallas_call(
        _ssd_kernel,
        grid=(batch, nc),
        in_specs=[pl.BlockSpec((L, cdim), lambda b, c: (b * nc + c, COL_XBC // cdim)),
                  pl.BlockSpec((L, inner), lambda b, c: (b * nc + c, COL_Z // inner)),
                  pl.BlockSpec((L, LANE), rowc),
                  pl.BlockSpec((SSM_CONV, cdim), const),
                  pl.BlockSpec((1, cdim), const),
                  pl.BlockSpec((1, LANE), const),
                  pl.BlockSpec((1, LANE), const),
                  pl.BlockSpec((1, inner), const),
                  pl.BlockSpec((1, inner), const)],
        out_specs=pl.BlockSpec((L, inner), rowc),
        out_shape=jax.ShapeDtypeStruct((t, inner), BF16),
        scratch_shapes=[pltpu.VMEM((2 * L, cdim), BF16), pltpu.VMEM((L, inner), F32),
                        pltpu.VMEM((SSM_HEADS // 2, SSM_STATE, 2 * SSM_HEADDIM), F32)],
        compiler_params=_cp(("arbitrary", "arbitrary")),
        name="ssd",
    )(proj, proj, dt_raw, conv_w.astype(F32), conv_b.reshape(1, cdim).astype(F32), pad16(dt_bias), pad16(a_log),
      dsk, ssm_norm_w.reshape(1, inner).astype(F32))


def _mix_kernel(o0_ref, o1_ref, o2_ref, s0_ref, s1_ref, s2_ref, ssm_ref, ga_ref, gs_ref, x_ref,
                g1_ref, sc2_ref, sh2_ref, g2_ref, nw_ref, wba_ref, wbs_ref, wo_ref, wrt_ref,
                wgs_ref, wus_ref, wds_ref, base_ref, u2p_ref, sct_ref):
    o_refs = (o0_ref, o1_ref, o2_ref)
    s_refs = (s0_ref, s1_ref, s2_ref)
    tm = x_ref.shape[0]
    sub = tm // MIX_SPLIT
    for part in range(MIX_SPLIT):
        rs = slice(part * sub, (part + 1) * sub)
        heads = []
        for h in range(HEADS_PER_GROUP):
            ms = [s[rs, h:h + 1] for s in s_refs]
            ls = [s[rs, HEADS_PER_GROUP + h:HEADS_PER_GROUP + h + 1] for s in s_refs]
            mx = jnp.maximum(jnp.maximum(ms[0], ms[1]), ms[2])
            wts = [l * jnp.exp(m - mx) for m, l in zip(ms, ls)]
            num = wts[0] * o_refs[0][h, rs, :] + wts[1] * o_refs[1][h, rs, :] + wts[2] * o_refs[2][h, rs, :]
            heads.append((num / (wts[0] + wts[1] + wts[2])).astype(BF16))
        attn = jnp.concatenate(heads, axis=1)
        ya = jnp.dot(attn, wba_ref[...], preferred_element_type=F32)
        ys = jnp.dot(ssm_ref[rs, :], wbs_ref[...], preferred_element_type=F32)
        merged = _sigmoid(ga_ref[rs, :].astype(F32)) * ya + _sigmoid(gs_ref[rs, :].astype(F32)) * ys
        mix = jnp.dot(merged.astype(BF16), wo_ref[...], preferred_element_type=F32)
        h1 = x_ref[rs, :] + g1_ref[...] * mix
        ms2 = jnp.mean(h1 * h1, axis=-1, keepdims=True)
        u2 = h1 * lax.rsqrt(ms2 + NORM_EPS) * nw_ref[...] * (1.0 + sc2_ref[...]) + sh2_ref[...]
        for cc, chunk in enumerate(_pack_row_chunks(u2)):
            u2p_ref[pl.ds(part * sub * ROW_PACK + cc, sub, stride=ROW_PACK), :] = chunk
        u2b = u2.astype(BF16)
        u2lo = (u2 - u2b.astype(F32)).astype(BF16)
        nt = (((1,), (1,)), ((), ()))
        logits_t = (lax.dot_general(wrt_ref[0], u2b, nt, preferred_element_type=F32)
                    + lax.dot_general(wrt_ref[0], u2lo, nt, preferred_element_type=F32)
                    + lax.dot_general(wrt_ref[1], u2b, nt, preferred_element_type=F32))
        sct_ref[:, rs] = _sigmoid(logits_t)
        hs_ = (_silu(jnp.dot(u2b, wgs_ref[...], preferred_element_type=F32))
               * jnp.dot(u2b, wus_ref[...], preferred_element_type=F32))
        shared = jnp.dot(hs_.astype(BF16), wds_ref[...], preferred_element_type=F32)
        base_ref[rs, :] = h1 + g2_ref[...] * shared


def _mix(o_list, st_list, ssm, proj, x2, g1, sc2, sh2, g2, norm_w, w_ba, w_bs, w_o, w_rt, w_gs, w_us, w_ds, seq):
    t, d = x2.shape
    tm = MIX_TM
    per_b = seq // tm
    row = lambda i: (i, 0)
    modrow = lambda i: (i // per_b, 0, 0)
    full = lambda a: pl.BlockSpec(a.shape, lambda i: (0,) * a.ndim, pipeline_mode=pl.Buffered(1))
    mod = pl.BlockSpec((None, 1, d), modrow)
    return pl.pallas_call(
        _mix_kernel,
        grid=(t // tm,),
        in_specs=[pl.BlockSpec((HEADS_PER_GROUP, tm, HEAD_DIM), lambda i: (0, i, 0))] * 3
        + [pl.BlockSpec((tm, LANE), row)] * 3 + [
            pl.BlockSpec((tm, d), row),
            pl.BlockSpec((tm, d), lambda i: (i, COL_GA // d)),
            pl.BlockSpec((tm, d), lambda i: (i, COL_GS // d)),
            pl.BlockSpec((tm, d), row),
            mod, mod, mod, mod, full(norm_w), full(w_ba), full(w_bs), full(w_o), full(w_rt),
            full(w_gs), full(w_us), full(w_ds)],
        out_specs=[pl.BlockSpec((tm, d), row), pl.BlockSpec((tm * ROW_PACK, LANE), row),
                   pl.BlockSpec((N_EXPERTS, tm), lambda i: (0, i))],
        out_shape=[jax.ShapeDtypeStruct((t, d), F32), jax.ShapeDtypeStruct((t * ROW_PACK, LANE), U32),
                   jax.ShapeDtypeStruct((N_EXPERTS, t), F32)],
        compiler_params=_cp(("arbitrary",)),
        name="mix",
    )(*o_list, *st_list, ssm, proj, proj, x2, g1, sc2, sh2, g2, norm_w, w_ba, w_bs, w_o, w_rt, w_gs, w_us, w_ds)


def _route_kernel(sct_ref, bias_ref, idx_ref, w_ref, mem_ref, cnt_ref, run):
    i = pl.program_id(0)
    tm = sct_ref.shape[1]
    per_g = N_EXPERTS // N_EXPERT_GROUPS

    @pl.when(i == 0)
    def _():
        run[...] = jnp.zeros(run.shape, F32)

    s = sct_ref[...]
    biased = s + bias_ref[...]
    io_g = lax.broadcasted_iota(I32, (per_g, tm), 0).astype(F32)
    gscore = []
    for g in range(N_EXPERT_GROUPS):
        bgp = biased[g * per_g:(g + 1) * per_g, :]
        m1 = jnp.max(bgp, axis=0, keepdims=True)
        first = jnp.min(jnp.where(bgp == m1, io_g, float(per_g)), axis=0, keepdims=True)
        m2 = jnp.max(jnp.where(io_g == first, NEG, bgp), axis=0, keepdims=True)
        gscore.append(m1 + m2)
    gs = jnp.concatenate(gscore, axis=0)
    io8 = lax.broadcasted_iota(I32, (N_EXPERT_GROUPS, tm), 0).astype(F32)
    gsel = jnp.zeros((N_EXPERT_GROUPS, tm), F32)
    cur = gs
    for _ in range(TOPK_GROUPS):
        mx = jnp.max(cur, axis=0, keepdims=True)
        fi = jnp.min(jnp.where(cur == mx, io8, float(N_EXPERT_GROUPS)), axis=0, keepdims=True)
        hit = io8 == fi
        gsel = jnp.where(hit, 1.0, gsel)
        cur = jnp.where(hit, NEG, cur)
    masked = jnp.concatenate(
        [jnp.where(gsel[g:g + 1, :] > 0.5, biased[g * per_g:(g + 1) * per_g, :], NEG) for g in range(N_EXPERT_GROUPS)],
        axis=0)
    io_e = lax.broadcasted_iota(I32, (N_EXPERTS, tm), 0).astype(F32)
    member = jnp.zeros((N_EXPERTS, tm), F32)
    idxs, ws = [], []
    for _ in range(TOP_K):
        mx = jnp.max(masked, axis=0, keepdims=True)
        fi = jnp.min(jnp.where(masked == mx, io_e, float(N_EXPERTS)), axis=0, keepdims=True)
        hit = io_e == fi
        idxs.append(fi)
        ws.append(jnp.sum(jnp.where(hit, s, 0.0), axis=0, keepdims=True))
        member = jnp.where(hit, 1.0, member)
        masked = jnp.where(hit, NEG, masked)
    wsum = ws[0]
    for k in range(1, TOP_K):
        wsum = wsum + ws[k]
    idx_ref[...] = jnp.concatenate(idxs, axis=0).astype(I32)
    w_ref[...] = jnp.concatenate([w / wsum * ROUTED_SCALE for w in ws], axis=0)
    mem_ref[...] = member.astype(BF16)
    new_run = run[...] + jnp.sum(member, axis=1, keepdims=True)
    run[...] = new_run
    cnt_ref[...] = new_run


def _route(scores_t, router_bias):
    e, t = scores_t.shape
    tm = ROUTE_TM
    tok = pl.BlockSpec((TOP_K, tm), lambda i: (0, i))
    return pl.pallas_call(
        _route_kernel,
        grid=(t // tm,),
        in_specs=[pl.BlockSpec((e, tm), lambda i: (0, i)), pl.BlockSpec((e, 1), lambda i: (0, 0))],
        out_specs=[tok, tok, pl.BlockSpec((e, tm), lambda i: (0, i)), pl.BlockSpec((e, LANE), lambda i: (0, 0))],
        out_shape=[jax.ShapeDtypeStruct((TOP_K, t), I32), jax.ShapeDtypeStruct((TOP_K, t), F32),
                   jax.ShapeDtypeStruct((e, t), BF16), jax.ShapeDtypeStruct((e, LANE), F32)],
        scratch_shapes=[pltpu.VMEM((e, LANE), F32)],
        compiler_params=_cp(("arbitrary",)),
        name="route",
    )(scores_t, router_bias.reshape(e, 1).astype(F32))


def _dest_kernel(mem_ref, idx_ref, start_ref, dest_ref, run):
    i = pl.program_id(0)
    e, tm = mem_ref.shape

    @pl.when(i == 0)
    def _():
        run[...] = jnp.broadcast_to(start_ref[...], run.shape)

    member = mem_ref[...]
    tr = lax.broadcasted_iota(I32, (tm, tm), 0)
    tc = lax.broadcasted_iota(I32, (tm, tm), 1)
    upper = jnp.where(tr < tc, 1.0, 0.0).astype(BF16)
    rank_full = jnp.dot(member, upper, preferred_element_type=F32) + run[:, 0:1]
    io_e = lax.broadcasted_iota(I32, (e, tm), 0)
    idx = idx_ref[...]
    rows = [jnp.sum(jnp.where(io_e == idx[k:k + 1, :], rank_full, 0.0), axis=0, keepdims=True) for k in range(TOP_K)]
    dest_ref[...] = jnp.concatenate(rows, axis=0).astype(I32)
    run[...] = run[...] + jnp.sum(member.astype(F32), axis=1, keepdims=True)


def _dest(member, idx, pad_start):
    e, t = member.shape
    tm = ROUTE_TM
    return pl.pallas_call(
        _dest_kernel,
        grid=(t // tm,),
        in_specs=[pl.BlockSpec((e, tm), lambda i: (0, i)), pl.BlockSpec((TOP_K, tm), lambda i: (0, i)),
                  pl.BlockSpec((e, 1), lambda i: (0, 0))],
        out_specs=pl.BlockSpec((TOP_K, tm), lambda i: (0, i)),
        out_shape=jax.ShapeDtypeStruct((TOP_K, t), I32),
        scratch_shapes=[pltpu.VMEM((e, LANE), F32)],
        compiler_params=_cp(("arbitrary",)),
        name="dest",
    )(member, idx, pad_start.astype(F32).reshape(e, 1))


def _dispatch_kernel(pend_ref, npad_ref, dest_ref, u_ref, xs_ref, zero_scr, sem, zsem):
    i = pl.program_id(0)
    rt = ROW_PACK
    tm = u_ref.shape[0] // rt
    zrows = zero_scr.shape[0]

    @pl.when(i == 0)
    def _():
        zero_scr[...] = jnp.zeros(zero_scr.shape, zero_scr.dtype)

        def zcopy(e):
            start = pl.multiple_of(pend_ref[e] * rt - zrows, zrows)
            return pltpu.make_async_copy(zero_scr, xs_ref.at[pl.ds(start, zrows)], zsem)

        def zstart(e, c):
            @pl.when(npad_ref[e] > 0)
            def _():
                zcopy(e).start()
            return c

        def zwait(e, c):
            @pl.when(npad_ref[e] > 0)
            def _():
                zcopy(e).wait()
            return c

        lax.fori_loop(0, N_EXPERTS, zstart, 0)
        lax.fori_loop(0, N_EXPERTS, zwait, 0)

    def row_copy(t, k):
        src = u_ref.at[pl.ds(pl.multiple_of(t * rt, rt), rt)]
        dst = xs_ref.at[pl.ds(pl.multiple_of(dest_ref[t * TOP_K + k] * rt, rt), rt)]
        return pltpu.make_async_copy(src, dst, sem)

    def start(t, c):
        for k in range(TOP_K):
            row_copy(t, k).start(priority=k % 2)
        return c

    def wait(t, c):
        for k in range(TOP_K):
            row_copy(t, k).wait()
        return c

    lax.fori_loop(0, tm, start, 0)
    lax.fori_loop(0, tm, wait, 0)


def _dispatch(u2p, dest_flat, pad_end, padded, n_rows):
    rt = ROW_PACK
    t = u2p.shape[0] // rt
    tm = DISPATCH_TM
    gs = pltpu.PrefetchScalarGridSpec(
        num_scalar_prefetch=2,
        grid=(t // tm,),
        in_specs=[pl.BlockSpec((tm * TOP_K,), lambda i, pe, npd: (i,), memory_space=pltpu.SMEM),
                  pl.BlockSpec((tm * rt, LANE), lambda i, pe, npd: (i, 0))],
        out_specs=pl.BlockSpec(memory_space=pl.ANY),
        scratch_shapes=[pltpu.VMEM((MOE_ROWS * rt, LANE), U32), pltpu.SemaphoreType.DMA, pltpu.SemaphoreType.DMA],
    )
    return pl.pallas_call(
        _dispatch_kernel,
        grid_spec=gs,
        out_shape=jax.ShapeDtypeStruct((n_rows * rt, LANE), U32),
        compiler_params=_cp(("arbitrary",)),
        name="dispatch",
    )(pad_end, padded, dest_flat, u2p)


def _expert_kernel(bstart_ref, nblk_ref, wg_ref, wu_ref, wd_ref, xs_ref, ys_ref,
                   wg_b, wu_b, wd_b, xbuf, ybuf, in_sem, out_sem):
    e = pl.program_id(0)
    last = pl.num_programs(0) - 1
    rt = ROW_PACK
    n_in, rb = xbuf.shape[0], xbuf.shape[1] // rt
    n_out = ybuf.shape[0]
    n = nblk_ref[e]
    g0 = bstart_ref[e]
    total = bstart_ref[last] + nblk_ref[last]

    def block_rows(g):
        return pl.ds(pl.multiple_of(g * (rb * rt), rb * rt), rb * rt)

    def in_copy(g):
        return pltpu.make_async_copy(xs_ref.at[block_rows(g)], xbuf.at[g % n_in], in_sem.at[g % n_in])

    def out_copy(g):
        return pltpu.make_async_copy(ybuf.at[g % n_out], ys_ref.at[block_rows(g)], out_sem.at[g % n_out])

    @pl.when(e == 0)
    def _():
        for g in range(n_in - 1):
            @pl.when(g < total)
            def _():
                in_copy(g).start()

    wg_b[...] = wg_ref[...].astype(BF16)
    wu_b[...] = wu_ref[...].astype(BF16)
    wd_b[...] = wd_ref[...].astype(BF16)

    def body(b, carry):
        g = g0 + b
        in_copy(g).wait()

        @pl.when(g + n_in - 1 < total)
        def _():
            in_copy(g + n_in - 1).start()

        @pl.when(g >= n_out)
        def _():
            out_copy(g - n_out).wait()

        halves = [_unpack_pair(xbuf[g % n_in, pl.ds(cc, rb, stride=rt), :]) for cc in range(rt)]
        xb = jnp.concatenate([lo.astype(BF16) for lo, _ in halves] + [hi.astype(BF16) for _, hi in halves], axis=1)
        gate = jnp.dot(xb, wg_b[...], preferred_element_type=F32)
        up = jnp.dot(xb, wu_b[...], preferred_element_type=F32)
        h = (_silu(gate) * up).astype(BF16)
        y = jnp.dot(h, wd_b[...], preferred_element_type=F32)
        for cc, chunk in enumerate(_pack_row_chunks(y)):
            ybuf[g % n_out, pl.ds(cc, rb, stride=rt), :] = chunk
        out_copy(g).start()
        return carry

    lax.fori_loop(0, n, body, 0)

    @pl.when(e == last)
    def _():
        for j in range(n_out, 0, -1):
            @pl.when(total >= j)
            def _():
                out_copy(total - j).wait()


def _experts(xs, blk_start, nblk, w_gate_e, w_up_e, w_down_e):
    ne, d, ff = w_gate_e.shape
    rb = MOE_ROWS * ROW_PACK
    gs = pltpu.PrefetchScalarGridSpec(
        num_scalar_prefetch=2,
        grid=(ne,),
        in_specs=[pl.BlockSpec((None, d, ff), lambda e, ps, nb: (e, 0, 0)),
                  pl.BlockSpec((None, d, ff), lambda e, ps, nb: (e, 0, 0)),
                  pl.BlockSpec((None, ff, d), lambda e, ps, nb: (e, 0, 0)),
                  pl.BlockSpec(memory_space=pl.ANY)],
        out_specs=pl.BlockSpec(memory_space=pl.ANY),
        scratch_shapes=[pltpu.VMEM((d, ff), BF16), pltpu.VMEM((d, ff), BF16), pltpu.VMEM((ff, d), BF16),
                        pltpu.VMEM((EXPERT_IN_SLOTS, rb, LANE), U32), pltpu.VMEM((EXPERT_OUT_SLOTS, rb, LANE), U32),
                        pltpu.SemaphoreType.DMA((EXPERT_IN_SLOTS,)), pltpu.SemaphoreType.DMA((EXPERT_OUT_SLOTS,))],
    )
    return pl.pallas_call(
        _expert_kernel,
        grid_spec=gs,
        out_shape=jax.ShapeDtypeStruct(xs.shape, U32),
        compiler_params=_cp(("arbitrary",)),
        name="experts",
    )(blk_start, nblk, w_gate_e, w_up_e, w_down_e, xs)


def _combine_kernel(dest_ref, dnext_ref, wt_ref, base_ref, g2_ref, nfw_ref, ys_ref, out_ref, buf, sem):
    i = pl.program_id(0)
    n = pl.num_programs(0)
    tm = base_ref.shape[0]
    rt = ROW_PACK
    half = i % 2

    def row_copy(d_ref, h, t, k):
        src = ys_ref.at[pl.ds(pl.multiple_of(d_ref[t * TOP_K + k] * rt, rt), rt)]
        return pltpu.make_async_copy(src, buf.at[h, k, pl.ds(pl.multiple_of(t * rt, rt), rt)], sem.at[h])

    def issue(d_ref, h):
        def start(t, c):
            for k in range(TOP_K):
                row_copy(d_ref, h, t, k).start(priority=k % 2)
            return c
        lax.fori_loop(0, tm, start, 0)

    @pl.when(i == 0)
    def _():
        issue(dest_ref, 0)

    @pl.when(i + 1 < n)
    def _():
        issue(dnext_ref, 1 - half)

    def wait(t, c):
        for k in range(TOP_K):
            row_copy(dest_ref, half, t, k).wait()
        return c

    lax.fori_loop(0, tm, wait, 0)

    upper = lax.broadcasted_iota(I32, (2 * rt, LANE), 0) >= rt

    def weigh(t2, c):
        rows = pl.ds(pl.multiple_of(t2 * (2 * rt), 2 * rt), 2 * rt)
        acc_lo = jnp.zeros((2 * rt, LANE), F32)
        acc_hi = jnp.zeros((2 * rt, LANE), F32)
        for k in range(TOP_K):
            lo, hi = _unpack_pair(buf[half, k, rows, :])
            wv = jnp.where(upper, wt_ref[(2 * t2 + 1) * TOP_K + k], wt_ref[2 * t2 * TOP_K + k])
            acc_lo = acc_lo + lo * wv
            acc_hi = acc_hi + hi * wv
        buf[half, 0, rows, :] = lax.bitcast_convert_type(acc_lo, U32)
        buf[half, 1, rows, :] = lax.bitcast_convert_type(acc_hi, U32)
        return c

    lax.fori_loop(0, tm // 2, weigh, 0, unroll=2)
    ssq = jnp.zeros((tm, 1), F32)
    for cc in range(2 * rt):
        cs = slice(cc * LANE, (cc + 1) * LANE)
        routed = lax.bitcast_convert_type(buf[half, cc // rt, pl.ds(cc % rt, tm, stride=rt), :], F32)
        h2 = base_ref[:, cs] + g2_ref[:, cs] * routed
        out_ref[:, cs] = h2
        ssq = ssq + jnp.sum(h2 * h2, axis=-1, keepdims=True)
    inv = lax.rsqrt(ssq / (2 * rt * LANE) + NORM_EPS)
    out_ref[...] = out_ref[...] * inv * nfw_ref[...]


def _combine(ys, dest_flat, w_tok, base, g2, norm_final_w, seq):
    t, d = base.shape
    tm = COMBINE_TM
    per_b = seq // tm
    gs = pltpu.PrefetchScalarGridSpec(
        num_scalar_prefetch=0,
        grid=(t // tm,),
        in_specs=[pl.BlockSpec((tm * TOP_K,), lambda i: (i,), memory_space=pltpu.SMEM),
                  pl.BlockSpec((tm * TOP_K,), lambda i: (jnp.minimum(i + 1, t // tm - 1),), memory_space=pltpu.SMEM),
                  pl.BlockSpec((tm * TOP_K,), lambda i: (i,), memory_space=pltpu.SMEM),
                  pl.BlockSpec((tm, d), lambda i: (i, 0)),
                  pl.BlockSpec((None, 1, d), lambda i: (i // per_b, 0, 0)),
                  pl.BlockSpec((1, d), lambda i: (0, 0)),
                  pl.BlockSpec(memory_space=pl.ANY)],
        out_specs=pl.BlockSpec((tm, d), lambda i: (i, 0)),
        scratch_shapes=[pltpu.VMEM((2, TOP_K, tm * ROW_PACK, LANE), U32), pltpu.SemaphoreType.DMA((2,))],
    )
    return pl.pallas_call(
        _combine_kernel,
        grid_spec=gs,
        out_shape=jax.ShapeDtypeStruct((t, d), F32),
        compiler_params=_cp(("arbitrary",)),
        name="combine",
    )(dest_flat, dest_flat, w_tok, base, g2, norm_final_w, ys)


def kernel(x, c, positions, w_mod, b_mod, norm_mix_w, norm_ffn_w, w_in, conv_w, conv_b, dt_bias, a_log, d_skip,
           ssm_norm_w, w_branch_attn, w_branch_ssm, w_out, w_router, router_bias, w_gate_e, w_up_e, w_down_e,
           w_gate_s, w_up_s, w_down_s, norm_final_w):
    batch, seq, d = x.shape
    t = batch * seq
    assert w_mod.shape[0] == 1, "one layer"
    assert d == 2 * ROW_PACK * LANE and seq % INPROJ_TM == 0 and INPROJ_TM % (ATTN_DILATIONS[-1] * 16) == 0

    mod = _modulation(c, w_mod[0], b_mod[0])
    sh1, sc1, g1, sh2, sc2, g2 = [m.reshape(batch, 1, d) for m in jnp.split(mod, 6, axis=-1)]
    rope_c, rope_s1, rope_s2 = _rope_tables(positions)

    wi = w_in[0]
    q_dim = 3 * GROUP_W
    o_z = 3 * q_dim
    o_xbc = o_z + d
    o_dt = o_xbc + conv_w.shape[2]
    o_g = o_dt + SSM_HEADS
    qkv = lambda g: [wi[:, s * q_dim + g * GROUP_W:s * q_dim + (g + 1) * GROUP_W] for s in range(3)]
    w_packed = jnp.concatenate([wi[:, o_xbc:o_dt], wi[:, o_g:], wi[:, o_z:o_xbc]] + qkv(0) + qkv(1) + qkv(2),
                               axis=1).astype(BF16)
    assert w_packed.shape[1] == MAIN_W + 2 * QKV_W
    w_dt = jnp.pad(wi[:, o_dt:o_g], ((0, 0), (0, LANE - SSM_HEADS))).astype(BF16)

    x2 = x.reshape(t, d)
    proj, qkv1, qkv2, dt_raw = _inproj(x2, sc1, sh1, norm_mix_w.reshape(1, d), w_packed, w_dt,
                                       rope_c, rope_s1, rope_s2, batch, seq)

    srcs = [(proj.reshape(batch, 1, seq, MAIN_W), COL_Q0 // GROUP_W), (qkv1, 0), (qkv2, 0)]
    o_list, st_list = [], []
    for g, (src, col0) in enumerate(srcs):
        o, st = _attention_group(src, g, batch, seq, col0)
        o_list.append(o)
        st_list.append(st)
    ssm = _ssd(proj, dt_raw, conv_w[0], conv_b[0], dt_bias[0], a_log[0], d_skip[0], ssm_norm_w[0], batch, seq)

    wr_t = w_router[0].T
    wr_hi = wr_t.astype(BF16)
    w_rt = jnp.stack([wr_hi, (wr_t - wr_hi.astype(F32)).astype(BF16)])
    base, u2p, scores_t = _mix(
        o_list, st_list, ssm, proj, x2, g1, sc2, sh2, g2, norm_ffn_w.reshape(1, d),
        w_branch_attn[0].astype(BF16), w_branch_ssm[0].astype(BF16), w_out[0].astype(BF16),
        w_rt, w_gate_s[0].astype(BF16), w_up_s[0].astype(BF16), w_down_s[0].astype(BF16), seq)

    idx, w_sel, member, counts = _route(scores_t, router_bias[0])

    rb = MOE_ROWS
    cnt = counts[:, 0].astype(I32)
    padded = (cnt + rb - 1) // rb * rb
    pad_end = jnp.cumsum(padded).astype(I32)
    pad_start = pad_end - padded
    n_rows = t * TOP_K + N_EXPERTS * rb

    dest_flat = _dest(member, idx, pad_start).T.reshape(t * TOP_K)
    xs = _dispatch(u2p, dest_flat, pad_end, padded, n_rows)
    ys = _experts(xs, pad_start // rb, padded // rb, w_gate_e[0], w_up_e[0], w_down_e[0])
    out = _combine(ys, dest_flat, w_sel.T.reshape(t * TOP_K), base, g2, norm_final_w.reshape(1, d), seq)
    return out.reshape(batch, seq, d)
```

```python
import functools
import math

import jax
import jax.numpy as jnp
from jax import lax
from jax.experimental import pallas as pl
from jax.experimental.pallas import tpu as pltpu

F32 = jnp.float32
BF16 = jnp.bfloat16
I32 = jnp.int32
U32 = jnp.uint32

LANE = 128
VMEM_LIMIT = 56 * 1024 * 1024

HEAD_DIM = 128
HEADS_PER_GROUP = 4
GROUP_W = HEADS_PER_GROUP * HEAD_DIM
ATTN_DILATIONS = (1, 4, 16)
ATTN_BLK = 128
ROPE_DIM = 32
ROPE_HALF = 16
ROPE_THETA = 500000.0
SSM_HEADS = 16
SSM_HEADDIM = 64
SSM_GROUPS = 4
SSM_STATE = 128
SSM_CONV = 4
SSM_CHUNK = 128
N_EXPERTS = 256
TOP_K = 8
N_EXPERT_GROUPS = 8
TOPK_GROUPS = 4
ROUTED_SCALE = 2.5
NORM_EPS = 1e-6
NEG = -1e30

COL_XBC, COL_GA, COL_GS, COL_Z, COL_Q0 = 0, 2048, 3072, 4096, 5120
MAIN_W = 6656
QKV_W = 3 * GROUP_W

ROPE_TM = 2048
SSM_CHUNKS_PER_STEP = 2
INPROJ_TM, INPROJ_TN = 512, 512
MIX_TM, MIX_SPLIT = 512, 2
ROUTE_TM = 512
MOE_ROWS = 256
EXPERT_IN_SLOTS, EXPERT_OUT_SLOTS = 6, 4
DISPATCH_TM = 256
COMBINE_TM = 256
ROW_PACK = 4


def _cp(sem, vmem=VMEM_LIMIT):
    return pltpu.CompilerParams(dimension_semantics=sem, vmem_limit_bytes=vmem)


def _sigmoid(x):
    return 1.0 / (1.0 + jnp.exp(-x))


def _silu(x):
    return x * _sigmoid(x)


def _pack_pair(a, b):
    ua = lax.bitcast_convert_type(a.astype(BF16).astype(F32), U32)
    ub = lax.bitcast_convert_type(b.astype(BF16).astype(F32), U32)
    return (ua >> 16) | ub


def _unpack_pair(w):
    lo = lax.bitcast_convert_type(w << 16, F32)
    hi = lax.bitcast_convert_type(w & jnp.uint32(0xFFFF0000), F32)
    return lo, hi


def _pack_row_chunks(v):
    return [_pack_pair(v[:, c * LANE:(c + 1) * LANE], v[:, (c + ROW_PACK) * LANE:(c + ROW_PACK + 1) * LANE])
            for c in range(ROW_PACK)]


def _mod_kernel(c_ref, w_ref, b_ref, o_ref):
    cond = _silu(c_ref[...])
    o_ref[...] = jnp.dot(cond, w_ref[...], preferred_element_type=F32) + b_ref[...]


def _modulation(c, w_mod, b_mod):
    b, d = c.shape
    n = w_mod.shape[1]
    return pl.pallas_call(
        _mod_kernel,
        grid=(n // d,),
        in_specs=[pl.BlockSpec((b, d), lambda j: (0, 0)),
                  pl.BlockSpec((d, d), lambda j: (0, j)),
                  pl.BlockSpec((1, d), lambda j: (0, j))],
        out_specs=pl.BlockSpec((b, d), lambda j: (0, j)),
        out_shape=jax.ShapeDtypeStruct((b, n), F32),
        compiler_params=_cp(("arbitrary",)),
        name="modulation",
    )(c, w_mod, b_mod.reshape(1, n))


def _rope_kernel(pos_ref, inv_ref, c_ref, s1_ref, s2_ref):
    ang = pos_ref[...].astype(F32) * inv_ref[...]
    lane = lax.broadcasted_iota(I32, ang.shape, 1)
    cos = jnp.cos(ang)
    sin = jnp.sin(ang)
    c_ref[...] = jnp.where(lane < ROPE_DIM, cos, 1.0)
    s1_ref[...] = jnp.where(lane < ROPE_HALF, -sin, 0.0)
    s2_ref[...] = jnp.where((lane >= ROPE_HALF) & (lane < ROPE_DIM), sin, 0.0)


def _rope_tables(positions):
    t = positions.size
    tm = ROPE_TM
    inv_freq = ROPE_THETA ** (-jnp.arange(ROPE_HALF, dtype=F32) / ROPE_HALF)
    inv_row = jnp.concatenate([inv_freq, inv_freq, jnp.zeros((LANE - ROPE_DIM,), F32)]).reshape(1, LANE)
    spec = pl.BlockSpec((tm, LANE), lambda i: (i, 0))
    shp = jax.ShapeDtypeStruct((t, LANE), F32)
    return pl.pallas_call(
        _rope_kernel,
        grid=(t // tm,),
        in_specs=[pl.BlockSpec((tm, 1), lambda i: (i, 0)), pl.BlockSpec((1, LANE), lambda i: (0, 0))],
        out_specs=[spec, spec, spec],
        out_shape=[shp, shp, shp],
        compiler_params=_cp(("arbitrary",)),
        name="rope_tables",
    )(positions.reshape(t, 1), inv_row)


def _inproj_kernel(x_ref, sc_ref, sh_ref, nw_ref, w_ref, wdt_ref, c_ref, s1_ref, s2_ref,
                   main_ref, g1_ref, g2_ref, dt_ref, u_scr, rope_scr, uc_scr, *, tn, q_scale):
    tm = x_ref.shape[0]
    n_main = MAIN_W // tn
    n_qkv = QKV_W // tn

    x = x_ref[...]
    ms = jnp.mean(x * x, axis=-1, keepdims=True)
    y = x * lax.rsqrt(ms + NORM_EPS) * nw_ref[...]
    uf = y * (1.0 + sc_ref[...]) + sh_ref[...]
    u = uf.astype(BF16)
    dt_ref[...] = jnp.dot(u, wdt_ref[...], preferred_element_type=F32)
    n_chunk = uf.shape[1] // LANE
    for cc in range(n_chunk):
        uc_scr[cc] = uf[:, cc * LANE:(cc + 1) * LANE]
    for o, d in enumerate(ATTN_DILATIONS[1:]):
        rows = tm // d
        for r in range(d):
            for cc in range(n_chunk):
                u_scr[o, r * rows:(r + 1) * rows, cc * LANE:(cc + 1) * LANE] = (
                    uc_scr[cc, pl.ds(r, rows, stride=d), :].astype(BF16))
            for ti, tab in enumerate((c_ref, s1_ref, s2_ref)):
                rope_scr[o, ti, r * rows:(r + 1) * rows, :] = tab[pl.ds(r, rows, stride=d), :]

    for c in range(n_main + 2 * n_qkv):
        order = 0 if c < n_main else (1 if c < n_main + n_qkv else 2)
        jq = c - COL_Q0 // tn if order == 0 else (c - n_main - (order - 1) * n_qkv)
        lhs = u if order == 0 else u_scr[order - 1]
        acc = jnp.dot(lhs, w_ref[:, c * tn:(c + 1) * tn], preferred_element_type=F32)
        if jq in (0, 1):
            scale = q_scale if jq == 0 else 1.0
            tabs = (c_ref, s1_ref, s2_ref) if order == 0 else tuple(rope_scr.at[order - 1, ti] for ti in range(3))
            cs = tabs[0][...] * scale
            s1 = tabs[1][...] * scale
            s2 = tabs[2][...] * scale
            parts = []
            for h in range(tn // HEAD_DIM):
                a = acc[:, h * HEAD_DIM:(h + 1) * HEAD_DIM]
                parts.append(a * cs + pltpu.roll(a, LANE - ROPE_HALF, 1) * s1 + pltpu.roll(a, ROPE_HALF, 1) * s2)
            acc = jnp.concatenate(parts, axis=1)
        val = acc.astype(BF16)
        if order == 0:
            main_ref[:, c * tn:(c + 1) * tn] = val
        else:
            dst = g1_ref if order == 1 else g2_ref
            c0 = (c - n_main - (order - 1) * n_qkv) * tn
            dst[:, :, c0:c0 + tn] = val.reshape(dst.shape[0], dst.shape[1], tn)


def _inproj(x2, sc1, sh1, norm_w, w_packed, w_dt, rope_c, rope_s1, rope_s2, batch, seq):
    t, d = x2.shape
    tm, tn = INPROJ_TM, INPROJ_TN
    n = w_packed.shape[1]
    per_b = seq // tm
    d1, d2 = ATTN_DILATIONS[1], ATTN_DILATIONS[2]
    row = lambda i: (i, 0)
    modrow = lambda i: (i // per_b, 0, 0)
    const = lambda i: (0, 0)
    resident = pl.Buffered(1)
    qkv_map = lambda i: (i // per_b, 0, i % per_b, 0)
    return pl.pallas_call(
        functools.partial(_inproj_kernel, tn=tn, q_scale=1.0 / math.sqrt(HEAD_DIM)),
        grid=(t // tm,),
        in_specs=[pl.BlockSpec((tm, d), row),
                  pl.BlockSpec((None, 1, d), modrow),
                  pl.BlockSpec((None, 1, d), modrow),
                  pl.BlockSpec((1, d), const),
                  pl.BlockSpec((d, n), const, pipeline_mode=resident),
                  pl.BlockSpec((d, LANE), const),
                  pl.BlockSpec((tm, LANE), row),
                  pl.BlockSpec((tm, LANE), row),
                  pl.BlockSpec((tm, LANE), row)],
        out_specs=[pl.BlockSpec((tm, MAIN_W), row),
                   pl.BlockSpec((None, d1, tm // d1, QKV_W), qkv_map),
                   pl.BlockSpec((None, d2, tm // d2, QKV_W), qkv_map),
                   pl.BlockSpec((tm, LANE), row)],
        out_shape=[jax.ShapeDtypeStruct((t, MAIN_W), BF16),
                   jax.ShapeDtypeStruct((batch, d1, seq // d1, QKV_W), BF16),
                   jax.ShapeDtypeStruct((batch, d2, seq // d2, QKV_W), BF16),
                   jax.ShapeDtypeStruct((t, LANE), F32)],
        scratch_shapes=[pltpu.VMEM((2, tm, d), BF16), pltpu.VMEM((2, 3, tm, LANE), F32),
                        pltpu.VMEM((d // LANE, tm, LANE), F32)],
        compiler_params=_cp(("arbitrary",)),
        name="inproj",
    )(x2, sc1, sh1, norm_w, w_packed, w_dt, rope_c, rope_s1, rope_s2)


def _attn_kernel(q_ref, k_ref, v_ref, o_ref, st_ref, *, d, nb):
    blk = ATTN_BLK
    qi = lax.broadcasted_iota(I32, (blk, 2 * blk), 0)
    kj = lax.broadcasted_iota(I32, (blk, 2 * blk), 1)
    band = (kj >= qi) & (kj <= qi + blk)
    qi1 = lax.broadcasted_iota(I32, (blk, blk), 0)
    kj1 = lax.broadcasted_iota(I32, (blk, blk), 1)
    causal = kj1 <= qi1
    lane = kj1

    hsl = [slice(h * HEAD_DIM, (h + 1) * HEAD_DIM) for h in range(HEADS_PER_GROUP)]

    def blocks(items, nk, mask):
        s = jnp.concatenate(
            [lax.dot_general(q_ref[r, pl.ds(q0, blk), hs], k_ref[r, pl.ds(k0, nk), hs], (((1,), (1,)), ((), ())),
                             preferred_element_type=F32) for r, q0, k0 in items for hs in hsl], axis=0)
        s = jnp.where(jnp.concatenate([mask] * (HEADS_PER_GROUP * len(items)), axis=0), s, NEG)
        m = jnp.max(s, axis=-1, keepdims=True)
        p = jnp.exp(s - m)
        l = jnp.sum(p, axis=-1, keepdims=True)
        pb = p.astype(BF16)
        for it, (r, q0, k0) in enumerate(items):
            rows = pl.ds(q0, blk) if d == 1 else pl.ds(q0 * d + r, blk, stride=d)
            st = jnp.zeros((blk, LANE), F32)
            for h, hs in enumerate(hsl):
                hr = slice((it * HEADS_PER_GROUP + h) * blk, (it * HEADS_PER_GROUP + h + 1) * blk)
                o = jnp.dot(pb[hr], v_ref[r, pl.ds(k0, nk), hs], preferred_element_type=F32)
                o_ref[h, rows, :] = o / l[hr]
                st = jnp.where(lane == h, m[hr], st)
                st = jnp.where(lane == HEADS_PER_GROUP + h, l[hr], st)
            st_ref[rows, :] = st

    def band_item(r, n):
        q0 = pl.multiple_of(n * blk, blk)
        return (r, q0, pl.multiple_of(q0 - blk, blk))

    if d == 1:
        blocks([(0, 0, 0)], blk, causal)

        def body(j, c):
            blocks([band_item(0, 1 + 2 * j), band_item(0, 2 + 2 * j)], 2 * blk, band)
            return c
        lax.fori_loop(0, (nb - 1) // 2, body, 0, unroll=2)
        if (nb - 1) % 2:
            blocks([band_item(0, nb - 1)], 2 * blk, band)
    else:
        def body(j, c):
            blocks([(2 * j, 0, 0), (2 * j + 1, 0, 0)], blk, causal)
            for n in range(1, nb):
                blocks([band_item(2 * j, n), band_item(2 * j + 1, n)], 2 * blk, band)
            return c
        lax.fori_loop(0, d // 2, body, 0, unroll=2 if nb == 1 else 1)


def _attention_group(src, g, batch, seq, col0):
    d = ATTN_DILATIONS[g]
    n_sub = seq // d
    nb = n_sub // ATTN_BLK
    spec = lambda c: pl.BlockSpec((None, d, n_sub, GROUP_W), lambda b: (b, 0, 0, c))
    o, st = pl.pallas_call(
        functools.partial(_attn_kernel, d=d, nb=nb),
        grid=(batch,),
        in_specs=[spec(col0), spec(col0 + 1), spec(col0 + 2)],
        out_specs=[pl.BlockSpec((HEADS_PER_GROUP, seq, HEAD_DIM), lambda b: (0, b, 0)),
                   pl.BlockSpec((seq, LANE), lambda b: (b, 0))],
        out_shape=[jax.ShapeDtypeStruct((HEADS_PER_GROUP, batch * seq, HEAD_DIM), F32),
                   jax.ShapeDtypeStruct((batch * seq, LANE), F32)],
        compiler_params=_cp(("arbitrary",)),
        name=f"attn_d{d}",
    )(src, src, src)
    return o, st


def _ssd_kernel(xbc_ref, z_ref, dt_ref, cw_ref, cb_ref, dtb_ref, alog_ref, dsk_ref, nw_ref,
                out_ref, xwin, ystage, state):
    L = SSM_CHUNK
    nsub = xbc_ref.shape[0] // L

    @pl.when(pl.program_id(1) == 0)
    def _():
        xwin[0:L, :] = jnp.zeros((L, xwin.shape[1]), BF16)
        state[...] = jnp.zeros(state.shape, F32)

    xwin[L:(nsub + 1) * L, :] = xbc_ref[...]
    for j in range(nsub):
        _ssd_chunk(slice(j * L, (j + 1) * L), xwin[j * L:(j + 2) * L, :], xbc_ref, z_ref, dt_ref, cw_ref, cb_ref,
                   dtb_ref, alog_ref, dsk_ref, nw_ref, out_ref, ystage, state)
    xwin[0:L, :] = xbc_ref[(nsub - 1) * L:nsub * L, :]


def _ssd_chunk(rs, win, xbc_ref, z_ref, dt_ref, cw_ref, cb_ref, dtb_ref, alog_ref, dsk_ref, nw_ref,
               out_ref, ystage, state):
    L = SSM_CHUNK
    inner = SSM_HEADS * SSM_HEADDIM
    gw = SSM_STATE

    xcur = xbc_ref[rs, :]
    srow = lax.broadcasted_iota(I32, (L, 2 * L), 0)
    scol = lax.broadcasted_iota(I32, (L, 2 * L), 1)
    conv = cb_ref[...] + cw_ref[SSM_CONV - 1:SSM_CONV, :] * xcur.astype(F32)
    for s in range(1, SSM_CONV):
        shift_m = jnp.where(scol == srow + (L - s), 1.0, 0.0).astype(BF16)
        conv = conv + cw_ref[SSM_CONV - 1 - s:SSM_CONV - s, :] * jnp.dot(shift_m, win, preferred_element_type=F32)
    act = _silu(conv)
    xs = act[:, :inner]
    xs_b = xs.astype(BF16)

    lane = lax.broadcasted_iota(I32, (L, LANE), 1)
    row = lax.broadcasted_iota(I32, (L, LANE), 0)
    dtr = dt_ref[rs, :] + dtb_ref[...]
    dt = jnp.maximum(dtr, 0.0) + jnp.log(1.0 + jnp.exp(-jnp.abs(dtr)))
    a_neg = jnp.where(lane < SSM_HEADS, -jnp.exp(alog_ref[...]), 0.0)
    a = dt * a_neg
    cs = a
    shift = 1
    while shift < L:
        cs = cs + jnp.where(row >= shift, pltpu.roll(cs, shift, 0), 0.0)
        shift *= 2
    cs_t = cs.T
    dt_t = dt.T
    tri = row >= lane
    half = lane < SSM_HEADDIM
    zero_b = jnp.zeros((L, LANE), BF16)

    for g in range(SSM_GROUPS):
        bg = act[:, inner + g * gw:inner + (g + 1) * gw]
        cg = act[:, inner + SSM_GROUPS * gw + g * gw:inner + SSM_GROUPS * gw + (g + 1) * gw]
        cg_b = cg.astype(BF16)
        cb = lax.dot_general(cg_b, bg.astype(BF16), (((1,), (1,)), ((), ())), preferred_element_type=F32)
        bg_t = bg.T
        for pair in range(2):
            h0 = g * 4 + pair * 2
            pidx = h0 // 2
            xpp = xs_b[:, pidx * LANE:(pidx + 1) * LANE]
            rhs = jnp.concatenate([jnp.where(half, xpp, zero_b), jnp.where(half, zero_b, xpp)], axis=0)
            dec, dst, eoff, cdec = [], [], [], []
            for h in (h0, h0 + 1):
                cs_col = cs[:, h:h + 1]
                cs_row = cs_t[h:h + 1, :]
                dt_row = dt_t[h:h + 1, :]
                dec.append(cb * (jnp.exp(jnp.where(tri, cs_col - cs_row, NEG)) * dt_row))
                cs_last = cs_row[:, L - 1:L]
                dst.append(bg_t * (jnp.exp(cs_last - cs_row) * dt_row))
                eoff.append(jnp.exp(cs_col))
                cdec.append(jnp.exp(cs_last))
            y_diag = jnp.dot(jnp.concatenate(dec, axis=1).astype(BF16), rhs, preferred_element_type=F32)
            st_new = jnp.dot(jnp.concatenate(dst, axis=1).astype(BF16), rhs, preferred_element_type=F32)
            prev = state[pidx]
            y_off = jnp.dot(cg_b, prev.astype(BF16), preferred_element_type=F32)
            y_off = y_off * jnp.where(half, eoff[0], eoff[1])
            state[pidx] = prev * jnp.where(half, cdec[0], cdec[1]) + st_new
            y = y_diag + y_off + dsk_ref[:, pidx * LANE:(pidx + 1) * LANE] * xs[:, pidx * LANE:(pidx + 1) * LANE]
            out_pair = y * _silu(z_ref[rs, pidx * LANE:(pidx + 1) * LANE].astype(F32))
            ystage[rs, pidx * LANE:(pidx + 1) * LANE] = out_pair

    gsz = inner // SSM_GROUPS
    for g in range(SSM_GROUPS):
        yg = ystage[rs, g * gsz:(g + 1) * gsz]
        ms = jnp.mean(yg * yg, axis=-1, keepdims=True)
        out_ref[rs, g * gsz:(g + 1) * gsz] = (yg * lax.rsqrt(ms + NORM_EPS) * nw_ref[:, g * gsz:(g + 1) * gsz]).astype(BF16)


def _ssd(proj, dt_raw, conv_w, conv_b, dt_bias, a_log, d_skip, ssm_norm_w, batch, seq):
    t = batch * seq
    L = SSM_CHUNK * SSM_CHUNKS_PER_STEP
    nc = seq // L
    inner = SSM_HEADS * SSM_HEADDIM
    cdim = conv_w.shape[1]
    pad16 = lambda v: jnp.pad(v.astype(F32), (0, LANE - SSM_HEADS)).reshape(1, LANE)
    dsk = jnp.repeat(d_skip.astype(F32), SSM_HEADDIM).reshape(1, inner)
    rowc = lambda b, c: (b * nc + c, 0)
    const = lambda b, c: (0, 0)
    return pl.pallas_call(
        _ssd_kernel,
        grid=(batch, nc),
        in_specs=[pl.BlockSpec((L, cdim), lambda b, c: (b * nc + c, COL_XBC // cdim)),
                  pl.BlockSpec((L, inner), lambda b, c: (b * nc + c, COL_Z // inner)),
                  pl.BlockSpec((L, LANE), rowc),
                  pl.BlockSpec((SSM_CONV, cdim), const),
                  pl.BlockSpec((1, cdim), const),
                  pl.BlockSpec((1, LANE), const),
                  pl.BlockSpec((1, LANE), const),
                  pl.BlockSpec((1, inner), const),
                  pl.BlockSpec((1, inner), const)],
        out_specs=pl.BlockSpec((L, inner), rowc),
        out_shape=jax.ShapeDtypeStruct((t, inner), BF16),
        scratch_shapes=[pltpu.VMEM((SSM_CHUNK + L, cdim), BF16), pltpu.VMEM((L, inner), F32),
                        pltpu.VMEM((SSM_HEADS // 2, SSM_STATE, 2 * SSM_HEADDIM), F32)],
        compiler_params=_cp(("arbitrary", "arbitrary")),
        name="ssd",
    )(proj, proj, dt_raw, conv_w.astype(F32), conv_b.reshape(1, cdim).astype(F32), pad16(dt_bias), pad16(a_log),
      dsk, ssm_norm_w.reshape(1, inner).astype(F32))


def _mix_kernel(o0_ref, o1_ref, o2_ref, s0_ref, s1_ref, s2_ref, ssm_ref, ga_ref, gs_ref, x_ref,
                g1_ref, sc2_ref, sh2_ref, g2_ref, nw_ref, wba_ref, wbs_ref, wo_ref, wrt_ref,
                wgs_ref, wus_ref, wds_ref, base_ref, u2p_ref, sct_ref):
    o_refs = (o0_ref, o1_ref, o2_ref)
    s_refs = (s0_ref, s1_ref, s2_ref)
    tm = x_ref.shape[0]
    sub = tm // MIX_SPLIT
    for part in range(MIX_SPLIT):
        rs = slice(part * sub, (part + 1) * sub)
        heads = []
        for h in range(HEADS_PER_GROUP):
            ms = [s[rs, h:h + 1] for s in s_refs]
            ls = [s[rs, HEADS_PER_GROUP + h:HEADS_PER_GROUP + h + 1] for s in s_refs]
            mx = jnp.maximum(jnp.maximum(ms[0], ms[1]), ms[2])
            wts = [l * jnp.exp(m - mx) for m, l in zip(ms, ls)]
            num = wts[0] * o_refs[0][h, rs, :] + wts[1] * o_refs[1][h, rs, :] + wts[2] * o_refs[2][h, rs, :]
            heads.append((num / (wts[0] + wts[1] + wts[2])).astype(BF16))
        attn = jnp.concatenate(heads, axis=1)
        ya = jnp.dot(attn, wba_ref[...], preferred_element_type=F32)
        ys = jnp.dot(ssm_ref[rs, :], wbs_ref[...], preferred_element_type=F32)
        merged = _sigmoid(ga_ref[rs, :].astype(F32)) * ya + _sigmoid(gs_ref[rs, :].astype(F32)) * ys
        mix = jnp.dot(merged.astype(BF16), wo_ref[...], preferred_element_type=F32)
        h1 = x_ref[rs, :] + g1_ref[...] * mix
        ms2 = jnp.mean(h1 * h1, axis=-1, keepdims=True)
        u2 = h1 * lax.rsqrt(ms2 + NORM_EPS) * nw_ref[...] * (1.0 + sc2_ref[...]) + sh2_ref[...]
        for cc, chunk in enumerate(_pack_row_chunks(u2)):
            u2p_ref[pl.ds(part * sub * ROW_PACK + cc, sub, stride=ROW_PACK), :] = chunk
        u2b = u2.astype(BF16)
        u2lo = (u2 - u2b.astype(F32)).astype(BF16)
        nt = (((1,), (1,)), ((), ()))
        logits_t = (lax.dot_general(wrt_ref[0], u2b, nt, preferred_element_type=F32)
                    + lax.dot_general(wrt_ref[0], u2lo, nt, preferred_element_type=F32)
                    + lax.dot_general(wrt_ref[1], u2b, nt, preferred_element_type=F32))
        sct_ref[:, rs] = _sigmoid(logits_t)
        hs_ = (_silu(jnp.dot(u2b, wgs_ref[...], preferred_element_type=F32))
               * jnp.dot(u2b, wus_ref[...], preferred_element_type=F32))
        shared = jnp.dot(hs_.astype(BF16), wds_ref[...], preferred_element_type=F32)
        base_ref[rs, :] = h1 + g2_ref[...] * shared


def _mix(o_list, st_list, ssm, proj, x2, g1, sc2, sh2, g2, norm_w, w_ba, w_bs, w_o, w_rt, w_gs, w_us, w_ds, seq):
    t, d = x2.shape
    tm = MIX_TM
    per_b = seq // tm
    row = lambda i: (i, 0)
    modrow = lambda i: (i // per_b, 0, 0)
    full = lambda a: pl.BlockSpec(a.shape, lambda i: (0,) * a.ndim, pipeline_mode=pl.Buffered(1))
    mod = pl.BlockSpec((None, 1, d), modrow)
    return pl.pallas_call(
        _mix_kernel,
        grid=(t // tm,),
        in_specs=[pl.BlockSpec((HEADS_PER_GROUP, tm, HEAD_DIM), lambda i: (0, i, 0))] * 3
        + [pl.BlockSpec((tm, LANE), row)] * 3 + [
            pl.BlockSpec((tm, d), row),
            pl.BlockSpec((tm, d), lambda i: (i, COL_GA // d)),
            pl.BlockSpec((tm, d), lambda i: (i, COL_GS // d)),
            pl.BlockSpec((tm, d), row),
            mod, mod, mod, mod, full(norm_w), full(w_ba), full(w_bs), full(w_o), full(w_rt),
            full(w_gs), full(w_us), full(w_ds)],
        out_specs=[pl.BlockSpec((tm, d), row), pl.BlockSpec((tm * ROW_PACK, LANE), row),
                   pl.BlockSpec((N_EXPERTS, tm), lambda i: (0, i))],
        out_shape=[jax.ShapeDtypeStruct((t, d), F32), jax.ShapeDtypeStruct((t * ROW_PACK, LANE), U32),
                   jax.ShapeDtypeStruct((N_EXPERTS, t), F32)],
        compiler_params=_cp(("arbitrary",)),
        name="mix",
    )(*o_list, *st_list, ssm, proj, proj, x2, g1, sc2, sh2, g2, norm_w, w_ba, w_bs, w_o, w_rt, w_gs, w_us, w_ds)


def _route_kernel(sct_ref, bias_ref, idx_ref, w_ref, mem_ref, cnt_ref, run):
    i = pl.program_id(0)
    tm = sct_ref.shape[1]
    per_g = N_EXPERTS // N_EXPERT_GROUPS

    @pl.when(i == 0)
    def _():
        run[...] = jnp.zeros(run.shape, F32)

    s = sct_ref[...]
    biased = s + bias_ref[...]
    io_g = lax.broadcasted_iota(I32, (per_g, tm), 0).astype(F32)
    gscore = []
    for g in range(N_EXPERT_GROUPS):
        bgp = biased[g * per_g:(g + 1) * per_g, :]
        m1 = jnp.max(bgp, axis=0, keepdims=True)
        first = jnp.min(jnp.where(bgp == m1, io_g, float(per_g)), axis=0, keepdims=True)
        m2 = jnp.max(jnp.where(io_g == first, NEG, bgp), axis=0, keepdims=True)
        gscore.append(m1 + m2)
    gs = jnp.concatenate(gscore, axis=0)
    io8 = lax.broadcasted_iota(I32, (N_EXPERT_GROUPS, tm), 0).astype(F32)
    gsel = jnp.zeros((N_EXPERT_GROUPS, tm), F32)
    cur = gs
    for _ in range(TOPK_GROUPS):
        mx = jnp.max(cur, axis=0, keepdims=True)
        fi = jnp.min(jnp.where(cur == mx, io8, float(N_EXPERT_GROUPS)), axis=0, keepdims=True)
        hit = io8 == fi
        gsel = jnp.where(hit, 1.0, gsel)
        cur = jnp.where(hit, NEG, cur)
    masked = jnp.concatenate(
        [jnp.where(gsel[g:g + 1, :] > 0.5, biased[g * per_g:(g + 1) * per_g, :], NEG) for g in range(N_EXPERT_GROUPS)],
        axis=0)
    io_e = lax.broadcasted_iota(I32, (N_EXPERTS, tm), 0).astype(F32)
    member = jnp.zeros((N_EXPERTS, tm), F32)
    idxs, ws = [], []
    for _ in range(TOP_K):
        mx = jnp.max(masked, axis=0, keepdims=True)
        fi = jnp.min(jnp.where(masked == mx, io_e, float(N_EXPERTS)), axis=0, keepdims=True)
        hit = io_e == fi
        idxs.append(fi)
        ws.append(jnp.sum(jnp.where(hit, s, 0.0), axis=0, keepdims=True))
        member = jnp.where(hit, 1.0, member)
        masked = jnp.where(hit, NEG, masked)
    wsum = ws[0]
    for k in range(1, TOP_K):
        wsum = wsum + ws[k]
    idx_ref[...] = jnp.concatenate(idxs, axis=0).astype(I32)
    w_ref[...] = jnp.concatenate([w / wsum * ROUTED_SCALE for w in ws], axis=0)
    mem_ref[...] = member.astype(BF16)
    new_run = run[...] + jnp.sum(member, axis=1, keepdims=True)
    run[...] = new_run
    cnt_ref[...] = new_run


def _route(scores_t, router_bias):
    e, t = scores_t.shape
    tm = ROUTE_TM
    tok = pl.BlockSpec((TOP_K, tm), lambda i: (0, i))
    return pl.pallas_call(
        _route_kernel,
        grid=(t // tm,),
        in_specs=[pl.BlockSpec((e, tm), lambda i: (0, i)), pl.BlockSpec((e, 1), lambda i: (0, 0))],
        out_specs=[tok, tok, pl.BlockSpec((e, tm), lambda i: (0, i)), pl.BlockSpec((e, LANE), lambda i: (0, 0))],
        out_shape=[jax.ShapeDtypeStruct((TOP_K, t), I32), jax.ShapeDtypeStruct((TOP_K, t), F32),
                   jax.ShapeDtypeStruct((e, t), BF16), jax.ShapeDtypeStruct((e, LANE), F32)],
        scratch_shapes=[pltpu.VMEM((e, LANE), F32)],
        compiler_params=_cp(("arbitrary",)),
        name="route",
    )(scores_t, router_bias.reshape(e, 1).astype(F32))


def _dest_kernel(mem_ref, idx_ref, start_ref, dest_ref, run):
    i = pl.program_id(0)
    e, tm = mem_ref.shape

    @pl.when(i == 0)
    def _():
        run[...] = jnp.broadcast_to(start_ref[...], run.shape)

    member = mem_ref[...]
    tr = lax.broadcasted_iota(I32, (tm, tm), 0)
    tc = lax.broadcasted_iota(I32, (tm, tm), 1)
    upper = jnp.where(tr < tc, 1.0, 0.0).astype(BF16)
    rank_full = jnp.dot(member, upper, preferred_element_type=F32) + run[:, 0:1]
    io_e = lax.broadcasted_iota(I32, (e, tm), 0)
    idx = idx_ref[...]
    rows = [jnp.sum(jnp.where(io_e == idx[k:k + 1, :], rank_full, 0.0), axis=0, keepdims=True) for k in range(TOP_K)]
    dest_ref[...] = jnp.concatenate(rows, axis=0).astype(I32)
    run[...] = run[...] + jnp.sum(member.astype(F32), axis=1, keepdims=True)


def _dest(member, idx, pad_start):
    e, t = member.shape
    tm = ROUTE_TM
    return pl.pallas_call(
        _dest_kernel,
        grid=(t // tm,),
        in_specs=[pl.BlockSpec((e, tm), lambda i: (0, i)), pl.BlockSpec((TOP_K, tm), lambda i: (0, i)),
                  pl.BlockSpec((e, 1), lambda i: (0, 0))],
        out_specs=pl.BlockSpec((TOP_K, tm), lambda i: (0, i)),
        out_shape=jax.ShapeDtypeStruct((TOP_K, t), I32),
        scratch_shapes=[pltpu.VMEM((e, LANE), F32)],
        compiler_params=_cp(("arbitrary",)),
        name="dest",
    )(member, idx, pad_start.astype(F32).reshape(e, 1))


def _dispatch_kernel(pend_ref, npad_ref, dest_ref, u_ref, xs_ref, zero_scr, sem, zsem):
    i = pl.program_id(0)
    rt = ROW_PACK
    tm = u_ref.shape[0] // rt
    zrows = zero_scr.shape[0]

    @pl.when(i == 0)
    def _():
        zero_scr[...] = jnp.zeros(zero_scr.shape, zero_scr.dtype)

        def zcopy(e):
            start = pl.multiple_of(pend_ref[e] * rt - zrows, zrows)
            return pltpu.make_async_copy(zero_scr, xs_ref.at[pl.ds(start, zrows)], zsem)

        def zstart(e, c):
            @pl.when(npad_ref[e] > 0)
            def _():
                zcopy(e).start()
            return c

        def zwait(e, c):
            @pl.when(npad_ref[e] > 0)
            def _():
                zcopy(e).wait()
            return c

        lax.fori_loop(0, N_EXPERTS, zstart, 0)
        lax.fori_loop(0, N_EXPERTS, zwait, 0)

    def row_copy(t, k):
        src = u_ref.at[pl.ds(pl.multiple_of(t * rt, rt), rt)]
        dst = xs_ref.at[pl.ds(pl.multiple_of(dest_ref[t * TOP_K + k] * rt, rt), rt)]
        return pltpu.make_async_copy(src, dst, sem)

    def start(t, c):
        for k in range(TOP_K):
            row_copy(t, k).start(priority=k % 2)
        return c

    def wait(t, c):
        for k in range(TOP_K):
            row_copy(t, k).wait()
        return c

    lax.fori_loop(0, tm, start, 0)
    lax.fori_loop(0, tm, wait, 0)


def _dispatch(u2p, dest_flat, pad_end, padded, n_rows):
    rt = ROW_PACK
    t = u2p.shape[0] // rt
    tm = DISPATCH_TM
    gs = pltpu.PrefetchScalarGridSpec(
        num_scalar_prefetch=2,
        grid=(t // tm,),
        in_specs=[pl.BlockSpec((tm * TOP_K,), lambda i, pe, npd: (i,), memory_space=pltpu.SMEM),
                  pl.BlockSpec((tm * rt, LANE), lambda i, pe, npd: (i, 0))],
        out_specs=pl.BlockSpec(memory_space=pl.ANY),
        scratch_shapes=[pltpu.VMEM((MOE_ROWS * rt, LANE), U32), pltpu.SemaphoreType.DMA, pltpu.SemaphoreType.DMA],
    )
    return pl.pallas_call(
        _dispatch_kernel,
        grid_spec=gs,
        out_shape=jax.ShapeDtypeStruct((n_rows * rt, LANE), U32),
        compiler_params=_cp(("arbitrary",)),
        name="dispatch",
    )(pad_end, padded, dest_flat, u2p)


def _expert_kernel(bstart_ref, nblk_ref, wg_ref, wu_ref, wd_ref, xs_ref, ys_ref,
                   wg_b, wu_b, wd_b, xbuf, ybuf, in_sem, out_sem):
    e = pl.program_id(0)
    last = pl.num_programs(0) - 1
    rt = ROW_PACK
    n_in, rb = xbuf.shape[0], xbuf.shape[1] // rt
    n_out = ybuf.shape[0]
    n = nblk_ref[e]
    g0 = bstart_ref[e]
    total = bstart_ref[last] + nblk_ref[last]

    def block_rows(g):
        return pl.ds(pl.multiple_of(g * (rb * rt), rb * rt), rb * rt)

    def in_copy(g):
        return pltpu.make_async_copy(xs_ref.at[block_rows(g)], xbuf.at[g % n_in], in_sem.at[g % n_in])

    def out_copy(g):
        return pltpu.make_async_copy(ybuf.at[g % n_out], ys_ref.at[block_rows(g)], out_sem.at[g % n_out])

    @pl.when(e == 0)
    def _():
        for g in range(n_in - 1):
            @pl.when(g < total)
            def _():
                in_copy(g).start()

    wg_b[...] = wg_ref[...].astype(BF16)
    wu_b[...] = wu_ref[...].astype(BF16)
    wd_b[...] = wd_ref[...].astype(BF16)

    def body(b, carry):
        g = g0 + b
        in_copy(g).wait()

        @pl.when(g + n_in - 1 < total)
        def _():
            in_copy(g + n_in - 1).start()

        @pl.when(g >= n_out)
        def _():
            out_copy(g - n_out).wait()

        halves = [_unpack_pair(xbuf[g % n_in, pl.ds(cc, rb, stride=rt), :]) for cc in range(rt)]
        xb = jnp.concatenate([lo.astype(BF16) for lo, _ in halves] + [hi.astype(BF16) for _, hi in halves], axis=1)
        gate = jnp.dot(xb, wg_b[...], preferred_element_type=F32)
        up = jnp.dot(xb, wu_b[...], preferred_element_type=F32)
        h = (_silu(gate) * up).astype(BF16)
        y = jnp.dot(h, wd_b[...], preferred_element_type=F32)
        for cc, chunk in enumerate(_pack_row_chunks(y)):
            ybuf[g % n_out, pl.ds(cc, rb, stride=rt), :] = chunk
        out_copy(g).start()
        return carry

    lax.fori_loop(0, n, body, 0)

    @pl.when(e == last)
    def _():
        for j in range(n_out, 0, -1):
            @pl.when(total >= j)
            def _():
                out_copy(total - j).wait()


def _experts(xs, blk_start, nblk, w_gate_e, w_up_e, w_down_e):
    ne, d, ff = w_gate_e.shape
    rb = MOE_ROWS * ROW_PACK
    gs = pltpu.PrefetchScalarGridSpec(
        num_scalar_prefetch=2,
        grid=(ne,),
        in_specs=[pl.BlockSpec((None, d, ff), lambda e, ps, nb: (e, 0, 0)),
                  pl.BlockSpec((None, d, ff), lambda e, ps, nb: (e, 0, 0)),
                  pl.BlockSpec((None, ff, d), lambda e, ps, nb: (e, 0, 0)),
                  pl.BlockSpec(memory_space=pl.ANY)],
        out_specs=pl.BlockSpec(memory_space=pl.ANY),
        scratch_shapes=[pltpu.VMEM((d, ff), BF16), pltpu.VMEM((d, ff), BF16), pltpu.VMEM((ff, d), BF16),
                        pltpu.VMEM((EXPERT_IN_SLOTS, rb, LANE), U32), pltpu.VMEM((EXPERT_OUT_SLOTS, rb, LANE), U32),
                        pltpu.SemaphoreType.DMA((EXPERT_IN_SLOTS,)), pltpu.SemaphoreType.DMA((EXPERT_OUT_SLOTS,))],
    )
    return pl.pallas_call(
        _expert_kernel,
        grid_spec=gs,
        out_shape=jax.ShapeDtypeStruct(xs.shape, U32),
        compiler_params=_cp(("arbitrary",)),
        name="experts",
    )(blk_start, nblk, w_gate_e, w_up_e, w_down_e, xs)


def _combine_kernel(dest_ref, dnext_ref, wt_ref, base_ref, g2_ref, nfw_ref, ys_ref, out_ref, buf, sem):
    i = pl.program_id(0)
    n = pl.num_programs(0)
    tm = base_ref.shape[0]
    rt = ROW_PACK
    half = i % 2

    def row_copy(d_ref, h, t, k):
        src = ys_ref.at[pl.ds(pl.multiple_of(d_ref[t * TOP_K + k] * rt, rt), rt)]
        return pltpu.make_async_copy(src, buf.at[h, k, pl.ds(pl.multiple_of(t * rt, rt), rt)], sem.at[h])

    def issue(d_ref, h):
        def start(t, c):
            for k in range(TOP_K):
                row_copy(d_ref, h, t, k).start(priority=k % 2)
            return c
        lax.fori_loop(0, tm, start, 0)

    @pl.when(i == 0)
    def _():
        issue(dest_ref, 0)

    @pl.when(i + 1 < n)
    def _():
        issue(dnext_ref, 1 - half)

    def wait(t, c):
        for k in range(TOP_K):
            row_copy(dest_ref, half, t, k).wait()
        return c

    lax.fori_loop(0, tm, wait, 0)

    upper = lax.broadcasted_iota(I32, (2 * rt, LANE), 0) >= rt

    def weigh(t2, c):
        rows = pl.ds(pl.multiple_of(t2 * (2 * rt), 2 * rt), 2 * rt)
        acc_lo = jnp.zeros((2 * rt, LANE), F32)
        acc_hi = jnp.zeros((2 * rt, LANE), F32)
        for k in range(TOP_K):
            lo, hi = _unpack_pair(buf[half, k, rows, :])
            wv = jnp.where(upper, wt_ref[(2 * t2 + 1) * TOP_K + k], wt_ref[2 * t2 * TOP_K + k])
            acc_lo = acc_lo + lo * wv
            acc_hi = acc_hi + hi * wv
        buf[half, 0, rows, :] = lax.bitcast_convert_type(acc_lo, U32)
        buf[half, 1, rows, :] = lax.bitcast_convert_type(acc_hi, U32)
        return c

    lax.fori_loop(0, tm // 2, weigh, 0, unroll=2)
    ssq = jnp.zeros((tm, 1), F32)
    for cc in range(2 * rt):
        cs = slice(cc * LANE, (cc + 1) * LANE)
        routed = lax.bitcast_convert_type(buf[half, cc // rt, pl.ds(cc % rt, tm, stride=rt), :], F32)
        h2 = base_ref[:, cs] + g2_ref[:, cs] * routed
        out_ref[:, cs] = h2
        ssq = ssq + jnp.sum(h2 * h2, axis=-1, keepdims=True)
    inv = lax.rsqrt(ssq / (2 * rt * LANE) + NORM_EPS)
    out_ref[...] = out_ref[...] * inv * nfw_ref[...]


def _combine(ys, dest_flat, w_tok, base, g2, norm_final_w, seq):
    t, d = base.shape
    tm = COMBINE_TM
    per_b = seq // tm
    gs = pltpu.PrefetchScalarGridSpec(
        num_scalar_prefetch=0,
        grid=(t // tm,),
        in_specs=[pl.BlockSpec((tm * TOP_K,), lambda i: (i,), memory_space=pltpu.SMEM),
                  pl.BlockSpec((tm * TOP_K,), lambda i: (jnp.minimum(i + 1, t // tm - 1),), memory_space=pltpu.SMEM),
                  pl.BlockSpec((tm * TOP_K,), lambda i: (i,), memory_space=pltpu.SMEM),
                  pl.BlockSpec((tm, d), lambda i: (i, 0)),
                  pl.BlockSpec((None, 1, d), lambda i: (i // per_b, 0, 0)),
                  pl.BlockSpec((1, d), lambda i: (0, 0)),
                  pl.BlockSpec(memory_space=pl.ANY)],
        out_specs=pl.BlockSpec((tm, d), lambda i: (i, 0)),
        scratch_shapes=[pltpu.VMEM((2, TOP_K, tm * ROW_PACK, LANE), U32), pltpu.SemaphoreType.DMA((2,))],
    )
    return pl.pallas_call(
        _combine_kernel,
        grid_spec=gs,
        out_shape=jax.ShapeDtypeStruct((t, d), F32),
        compiler_params=_cp(("arbitrary",)),
        name="combine",
    )(dest_flat, dest_flat, w_tok, base, g2, norm_final_w, ys)


def kernel(x, c, positions, w_mod, b_mod, norm_mix_w, norm_ffn_w, w_in, conv_w, conv_b, dt_bias, a_log, d_skip,
           ssm_norm_w, w_branch_attn, w_branch_ssm, w_out, w_router, router_bias, w_gate_e, w_up_e, w_down_e,
           w_gate_s, w_up_s, w_down_s, norm_final_w):
    batch, seq, d = x.shape
    t = batch * seq
    assert w_mod.shape[0] == 1, "one layer"
    assert d == 2 * ROW_PACK * LANE and seq % INPROJ_TM == 0 and INPROJ_TM % (ATTN_DILATIONS[-1] * 16) == 0

    mod = _modulation(c, w_mod[0], b_mod[0])
    sh1, sc1, g1, sh2, sc2, g2 = [m.reshape(batch, 1, d) for m in jnp.split(mod, 6, axis=-1)]
    rope_c, rope_s1, rope_s2 = _rope_tables(positions)

    wi = w_in[0]
    q_dim = 3 * GROUP_W
    o_z = 3 * q_dim
    o_xbc = o_z + d
    o_dt = o_xbc + conv_w.shape[2]
    o_g = o_dt + SSM_HEADS
    qkv = lambda g: [wi[:, s * q_dim + g * GROUP_W:s * q_dim + (g + 1) * GROUP_W] for s in range(3)]
    w_packed = jnp.concatenate([wi[:, o_xbc:o_dt], wi[:, o_g:], wi[:, o_z:o_xbc]] + qkv(0) + qkv(1) + qkv(2),
                               axis=1).astype(BF16)
    assert w_packed.shape[1] == MAIN_W + 2 * QKV_W
    w_dt = jnp.pad(wi[:, o_dt:o_g], ((0, 0), (0, LANE - SSM_HEADS))).astype(BF16)

    x2 = x.reshape(t, d)
    proj, qkv1, qkv2, dt_raw = _inproj(x2, sc1, sh1, norm_mix_w.reshape(1, d), w_packed, w_dt,
                                       rope_c, rope_s1, rope_s2, batch, seq)

    srcs = [(proj.reshape(batch, 1, seq, MAIN_W), COL_Q0 // GROUP_W), (qkv1, 0), (qkv2, 0)]
    o_list, st_list = [], []
    for g, (src, col0) in enumerate(srcs):
        o, st = _attention_group(src, g, batch, seq, col0)
        o_list.append(o)
        st_list.append(st)
    ssm = _ssd(proj, dt_raw, conv_w[0], conv_b[0], dt_bias[0], a_log[0], d_skip[0], ssm_norm_w[0], batch, seq)

    wr_t = w_router[0].T
    wr_hi = wr_t.astype(BF16)
    w_rt = jnp.stack([wr_hi, (wr_t - wr_hi.astype(F32)).astype(BF16)])
    base, u2p, scores_t = _mix(
        o_list, st_list, ssm, proj, x2, g1, sc2, sh2, g2, norm_ffn_w.reshape(1, d),
        w_branch_attn[0].astype(BF16), w_branch_ssm[0].astype(BF16), w_out[0].astype(BF16),
        w_rt, w_gate_s[0].astype(BF16), w_up_s[0].astype(BF16), w_down_s[0].astype(BF16), seq)

    idx, w_sel, member, counts = _route(scores_t, router_bias[0])

    rb = MOE_ROWS
    cnt = counts[:, 0].astype(I32)
    padded = (cnt + rb - 1) // rb * rb
    pad_end = jnp.cumsum(padded).astype(I32)
    pad_start = pad_end - padded
    n_rows = t * TOP_K + N_EXPERTS * rb

    dest_flat = _dest(member, idx, pad_start).T.reshape(t * TOP_K)
    xs = _dispatch(u2p, dest_flat, pad_end, padded, n_rows)
    ys = _experts(xs, pad_start // rb, padded // rb, w_gate_e[0], w_up_e[0], w_down_e[0])
    out = _combine(ys, dest_flat, w_sel.T.reshape(t * TOP_K), base, g2, norm_final_w.reshape(1, d), seq)
    return out.reshape(batch, seq, d)
```

```python
import functools
import math

import jax
import jax.numpy as jnp
from jax import lax
from jax.experimental import pallas as pl
from jax.experimental.pallas import tpu as pltpu

F32 = jnp.float32
BF16 = jnp.bfloat16
I32 = jnp.int32
U32 = jnp.uint32

LANE = 128
VMEM_LIMIT = 56 * 1024 * 1024

HEAD_DIM = 128
HEADS_PER_GROUP = 4
GROUP_W = HEADS_PER_GROUP * HEAD_DIM
ATTN_DILATIONS = (1, 4, 16)
ATTN_BLK = 128
ROPE_DIM = 32
ROPE_HALF = 16
ROPE_THETA = 500000.0
SSM_HEADS = 16
SSM_HEADDIM = 64
SSM_GROUPS = 4
SSM_STATE = 128
SSM_CONV = 4
SSM_CHUNK = 128
N_EXPERTS = 256
TOP_K = 8
N_EXPERT_GROUPS = 8
TOPK_GROUPS = 4
ROUTED_SCALE = 2.5
NORM_EPS = 1e-6
NEG = -1e30

COL_XBC, COL_GA, COL_GS, COL_Z, COL_Q0 = 0, 2048, 3072, 4096, 5120
MAIN_W = 6656
QKV_W = 3 * GROUP_W

ROPE_TM = 2048
SSM_CHUNKS_PER_STEP = 4
INPROJ_TM, INPROJ_TN = 512, 512
MIX_TM, MIX_SPLIT = 512, 2
ROUTE_TM = 512
MOE_ROWS = 256
EXPERT_IN_SLOTS, EXPERT_OUT_SLOTS = 6, 4
DISPATCH_TM = 256
COMBINE_TM = 256
ROW_PACK = 4


def _cp(sem, vmem=VMEM_LIMIT):
    return pltpu.CompilerParams(dimension_semantics=sem, vmem_limit_bytes=vmem)


def _sigmoid(x):
    return 1.0 / (1.0 + jnp.exp(-x))


def _silu(x):
    return x * _sigmoid(x)


def _pack_pair(a, b):
    ua = lax.bitcast_convert_type(a.astype(BF16).astype(F32), U32)
    ub = lax.bitcast_convert_type(b.astype(BF16).astype(F32), U32)
    return (ua >> 16) | ub


def _unpack_pair(w):
    lo = lax.bitcast_convert_type(w << 16, F32)
    hi = lax.bitcast_convert_type(w & jnp.uint32(0xFFFF0000), F32)
    return lo, hi


def _pack_row_chunks(v):
    return [_pack_pair(v[:, c * LANE:(c + 1) * LANE], v[:, (c + ROW_PACK) * LANE:(c + ROW_PACK + 1) * LANE])
            for c in range(ROW_PACK)]


def _mod_kernel(c_ref, w_ref, b_ref, o_ref):
    cond = _silu(c_ref[...])
    o_ref[...] = jnp.dot(cond, w_ref[...], preferred_element_type=F32) + b_ref[...]


def _modulation(c, w_mod, b_mod):
    b, d = c.shape
    n = w_mod.shape[1]
    return pl.pallas_call(
        _mod_kernel,
        grid=(n // d,),
        in_specs=[pl.BlockSpec((b, d), lambda j: (0, 0)),
                  pl.BlockSpec((d, d), lambda j: (0, j)),
                  pl.BlockSpec((1, d), lambda j: (0, j))],
        out_specs=pl.BlockSpec((b, d), lambda j: (0, j)),
        out_shape=jax.ShapeDtypeStruct((b, n), F32),
        compiler_params=_cp(("arbitrary",)),
        name="modulation",
    )(c, w_mod, b_mod.reshape(1, n))


def _rope_kernel(pos_ref, inv_ref, c_ref, s1_ref, s2_ref):
    ang = pos_ref[...].astype(F32) * inv_ref[...]
    lane = lax.broadcasted_iota(I32, ang.shape, 1)
    cos = jnp.cos(ang)
    sin = jnp.sin(ang)
    c_ref[...] = jnp.where(lane < ROPE_DIM, cos, 1.0)
    s1_ref[...] = jnp.where(lane < ROPE_HALF, -sin, 0.0)
    s2_ref[...] = jnp.where((lane >= ROPE_HALF) & (lane < ROPE_DIM), sin, 0.0)


def _rope_tables(positions):
    t = positions.size
    tm = ROPE_TM
    inv_freq = ROPE_THETA ** (-jnp.arange(ROPE_HALF, dtype=F32) / ROPE_HALF)
    inv_row = jnp.concatenate([inv_freq, inv_freq, jnp.zeros((LANE - ROPE_DIM,), F32)]).reshape(1, LANE)
    spec = pl.BlockSpec((tm, LANE), lambda i: (i, 0))
    shp = jax.ShapeDtypeStruct((t, LANE), F32)
    return pl.pallas_call(
        _rope_kernel,
        grid=(t // tm,),
        in_specs=[pl.BlockSpec((tm, 1), lambda i: (i, 0)), pl.BlockSpec((1, LANE), lambda i: (0, 0))],
        out_specs=[spec, spec, spec],
        out_shape=[shp, shp, shp],
        compiler_params=_cp(("arbitrary",)),
        name="rope_tables",
    )(positions.reshape(t, 1), inv_row)


def _inproj_kernel(x_ref, sc_ref, sh_ref, nw_ref, w_ref, wdt_ref, c_ref, s1_ref, s2_ref,
                   main_ref, g1_ref, g2_ref, dt_ref, u_scr, rope_scr, uc_scr, *, tn, q_scale):
    tm = x_ref.shape[0]
    n_main = MAIN_W // tn
    n_qkv = QKV_W // tn

    x = x_ref[...]
    ms = jnp.mean(x * x, axis=-1, keepdims=True)
    y = x * lax.rsqrt(ms + NORM_EPS) * nw_ref[...]
    uf = y * (1.0 + sc_ref[...]) + sh_ref[...]
    u = uf.astype(BF16)
    dt_ref[...] = jnp.dot(u, wdt_ref[...], preferred_element_type=F32)
    n_chunk = uf.shape[1] // LANE
    for cc in range(n_chunk):
        uc_scr[cc] = uf[:, cc * LANE:(cc + 1) * LANE]
    for o, d in enumerate(ATTN_DILATIONS[1:]):
        rows = tm // d
        for r in range(d):
            for cc in range(n_chunk):
                u_scr[o, r * rows:(r + 1) * rows, cc * LANE:(cc + 1) * LANE] = (
                    uc_scr[cc, pl.ds(r, rows, stride=d), :].astype(BF16))
            for ti, tab in enumerate((c_ref, s1_ref, s2_ref)):
                rope_scr[o, ti, r * rows:(r + 1) * rows, :] = tab[pl.ds(r, rows, stride=d), :]

    for c in range(n_main + 2 * n_qkv):
        order = 0 if c < n_main else (1 if c < n_main + n_qkv else 2)
        jq = c - COL_Q0 // tn if order == 0 else (c - n_main - (order - 1) * n_qkv)
        lhs = u if order == 0 else u_scr[order - 1]
        acc = jnp.dot(lhs, w_ref[:, c * tn:(c + 1) * tn], preferred_element_type=F32)
        if jq in (0, 1):
            scale = q_scale if jq == 0 else 1.0
            tabs = (c_ref, s1_ref, s2_ref) if order == 0 else tuple(rope_scr.at[order - 1, ti] for ti in range(3))
            cs = tabs[0][...] * scale
            s1 = tabs[1][...] * scale
            s2 = tabs[2][...] * scale
            parts = []
            for h in range(tn // HEAD_DIM):
                a = acc[:, h * HEAD_DIM:(h + 1) * HEAD_DIM]
                parts.append(a * cs + pltpu.roll(a, LANE - ROPE_HALF, 1) * s1 + pltpu.roll(a, ROPE_HALF, 1) * s2)
            acc = jnp.concatenate(parts, axis=1)
        val = acc.astype(BF16)
        if order == 0:
            main_ref[:, c * tn:(c + 1) * tn] = val
        else:
            dst = g1_ref if order == 1 else g2_ref
            c0 = (c - n_main - (order - 1) * n_qkv) * tn
            dst[:, :, c0:c0 + tn] = val.reshape(dst.shape[0], dst.shape[1], tn)


def _inproj(x2, sc1, sh1, norm_w, w_packed, w_dt, rope_c, rope_s1, rope_s2, batch, seq):
    t, d = x2.shape
    tm, tn = INPROJ_TM, INPROJ_TN
    n = w_packed.shape[1]
    per_b = seq // tm
    d1, d2 = ATTN_DILATIONS[1], ATTN_DILATIONS[2]
    row = lambda i: (i, 0)
    modrow = lambda i: (i // per_b, 0, 0)
    const = lambda i: (0, 0)
    resident = pl.Buffered(1)
    qkv_map = lambda i: (i // per_b, 0, i % per_b, 0)
    return pl.pallas_call(
        functools.partial(_inproj_kernel, tn=tn, q_scale=1.0 / math.sqrt(HEAD_DIM)),
        grid=(t // tm,),
        in_specs=[pl.BlockSpec((tm, d), row),
                  pl.BlockSpec((None, 1, d), modrow),
                  pl.BlockSpec((None, 1, d), modrow),
                  pl.BlockSpec((1, d), const),
                  pl.BlockSpec((d, n), const, pipeline_mode=resident),
                  pl.BlockSpec((d, LANE), const),
                  pl.BlockSpec((tm, LANE), row),
                  pl.BlockSpec((tm, LANE), row),
                  pl.BlockSpec((tm, LANE), row)],
        out_specs=[pl.BlockSpec((tm, MAIN_W), row),
                   pl.BlockSpec((None, d1, tm // d1, QKV_W), qkv_map),
                   pl.BlockSpec((None, d2, tm // d2, QKV_W), qkv_map),
                   pl.BlockSpec((tm, LANE), row)],
        out_shape=[jax.ShapeDtypeStruct((t, MAIN_W), BF16),
                   jax.ShapeDtypeStruct((batch, d1, seq // d1, QKV_W), BF16),
                   jax.ShapeDtypeStruct((batch, d2, seq // d2, QKV_W), BF16),
                   jax.ShapeDtypeStruct((t, LANE), F32)],
        scratch_shapes=[pltpu.VMEM((2, tm, d), BF16), pltpu.VMEM((2, 3, tm, LANE), F32),
                        pltpu.VMEM((d // LANE, tm, LANE), F32)],
        compiler_params=_cp(("arbitrary",)),
        name="inproj",
    )(x2, sc1, sh1, norm_w, w_packed, w_dt, rope_c, rope_s1, rope_s2)


def _attn_kernel(q_ref, k_ref, v_ref, o_ref, st_ref, *, d, nb):
    blk = ATTN_BLK
    qi = lax.broadcasted_iota(I32, (blk, 2 * blk), 0)
    kj = lax.broadcasted_iota(I32, (blk, 2 * blk), 1)
    band = (kj >= qi) & (kj <= qi + blk)
    qi1 = lax.broadcasted_iota(I32, (blk, blk), 0)
    kj1 = lax.broadcasted_iota(I32, (blk, blk), 1)
    causal = kj1 <= qi1
    lane = kj1

    hsl = [slice(h * HEAD_DIM, (h + 1) * HEAD_DIM) for h in range(HEADS_PER_GROUP)]

    def blocks(items, nk, mask):
        s = jnp.concatenate(
            [lax.dot_general(q_ref[r, pl.ds(q0, blk), hs], k_ref[r, pl.ds(k0, nk), hs], (((1,), (1,)), ((), ())),
                             preferred_element_type=F32) for r, q0, k0 in items for hs in hsl], axis=0)
        s = jnp.where(jnp.concatenate([mask] * (HEADS_PER_GROUP * len(items)), axis=0), s, NEG)
        m = jnp.max(s, axis=-1, keepdims=True)
        p = jnp.exp(s - m)
        l = jnp.sum(p, axis=-1, keepdims=True)
        pb = p.astype(BF16)
        for it, (r, q0, k0) in enumerate(items):
            rows = pl.ds(q0, blk) if d == 1 else pl.ds(q0 * d + r, blk, stride=d)
            st = jnp.zeros((blk, LANE), F32)
            for h, hs in enumerate(hsl):
                hr = slice((it * HEADS_PER_GROUP + h) * blk, (it * HEADS_PER_GROUP + h + 1) * blk)
                o = jnp.dot(pb[hr], v_ref[r, pl.ds(k0, nk), hs], preferred_element_type=F32)
                o_ref[h, rows, :] = o / l[hr]
                st = jnp.where(lane == h, m[hr], st)
                st = jnp.where(lane == HEADS_PER_GROUP + h, l[hr], st)
            st_ref[rows, :] = st

    def band_item(r, n):
        q0 = pl.multiple_of(n * blk, blk)
        return (r, q0, pl.multiple_of(q0 - blk, blk))

    if d == 1:
        blocks([(0, 0, 0)], blk, causal)

        def body(j, c):
            blocks([band_item(0, 1 + 2 * j), band_item(0, 2 + 2 * j)], 2 * blk, band)
            return c
        lax.fori_loop(0, (nb - 1) // 2, body, 0, unroll=2)
        if (nb - 1) % 2:
            blocks([band_item(0, nb - 1)], 2 * blk, band)
    else:
        def body(j, c):
            blocks([(2 * j, 0, 0), (2 * j + 1, 0, 0)], blk, causal)
            for n in range(1, nb):
                blocks([band_item(2 * j, n), band_item(2 * j + 1, n)], 2 * blk, band)
            return c
        lax.fori_loop(0, d // 2, body, 0, unroll=2 if nb == 1 else 1)


def _attention_group(src, g, batch, seq, col0):
    d = ATTN_DILATIONS[g]
    n_sub = seq // d
    nb = n_sub // ATTN_BLK
    spec = lambda c: pl.BlockSpec((None, d, n_sub, GROUP_W), lambda b: (b, 0, 0, c))
    o, st = pl.pallas_call(
        functools.partial(_attn_kernel, d=d, nb=nb),
        grid=(batch,),
        in_specs=[spec(col0), spec(col0 + 1), spec(col0 + 2)],
        out_specs=[pl.BlockSpec((HEADS_PER_GROUP, seq, HEAD_DIM), lambda b: (0, b, 0)),
                   pl.BlockSpec((seq, LANE), lambda b: (b, 0))],
        out_shape=[jax.ShapeDtypeStruct((HEADS_PER_GROUP, batch * seq, HEAD_DIM), F32),
                   jax.ShapeDtypeStruct((batch * seq, LANE), F32)],
        compiler_params=_cp(("arbitrary",)),
        name=f"attn_d{d}",
    )(src, src, src)
    return o, st


def _ssd_kernel(xbc_ref, z_ref, dt_ref, cw_ref, cb_ref, dtb_ref, alog_ref, dsk_ref, nw_ref,
                out_ref, xwin, ystage, state):
    L = SSM_CHUNK
    nsub = xbc_ref.shape[0] // L

    @pl.when(pl.program_id(1) == 0)
    def _():
        xwin[0:L, :] = jnp.zeros((L, xwin.shape[1]), BF16)
        state[...] = jnp.zeros(state.shape, F32)

    xwin[L:(nsub + 1) * L, :] = xbc_ref[...]
    for j in range(nsub):
        _ssd_chunk(slice(j * L, (j + 1) * L), xwin[j * L:(j + 2) * L, :], xbc_ref, z_ref, dt_ref, cw_ref, cb_ref,
                   dtb_ref, alog_ref, dsk_ref, nw_ref, out_ref, ystage, state)
    xwin[0:L, :] = xbc_ref[(nsub - 1) * L:nsub * L, :]


def _ssd_chunk(rs, win, xbc_ref, z_ref, dt_ref, cw_ref, cb_ref, dtb_ref, alog_ref, dsk_ref, nw_ref,
               out_ref, ystage, state):
    L = SSM_CHUNK
    inner = SSM_HEADS * SSM_HEADDIM
    gw = SSM_STATE

    xcur = xbc_ref[rs, :]
    srow = lax.broadcasted_iota(I32, (L, 2 * L), 0)
    scol = lax.broadcasted_iota(I32, (L, 2 * L), 1)
    conv = cb_ref[...] + cw_ref[SSM_CONV - 1:SSM_CONV, :] * xcur.astype(F32)
    for s in range(1, SSM_CONV):
        shift_m = jnp.where(scol == srow + (L - s), 1.0, 0.0).astype(BF16)
        conv = conv + cw_ref[SSM_CONV - 1 - s:SSM_CONV - s, :] * jnp.dot(shift_m, win, preferred_element_type=F32)
    act = _silu(conv)
    xs = act[:, :inner]
    xs_b = xs.astype(BF16)

    lane = lax.broadcasted_iota(I32, (L, LANE), 1)
    row = lax.broadcasted_iota(I32, (L, LANE), 0)
    dtr = dt_ref[rs, :] + dtb_ref[...]
    dt = jnp.maximum(dtr, 0.0) + jnp.log(1.0 + jnp.exp(-jnp.abs(dtr)))
    a_neg = jnp.where(lane < SSM_HEADS, -jnp.exp(alog_ref[...]), 0.0)
    a = dt * a_neg
    cs = a
    shift = 1
    while shift < L:
        cs = cs + jnp.where(row >= shift, pltpu.roll(cs, shift, 0), 0.0)
        shift *= 2
    cs_t = cs.T
    dt_t = dt.T
    tri = row >= lane
    half = lane < SSM_HEADDIM
    zero_b = jnp.zeros((L, LANE), BF16)

    for g in range(SSM_GROUPS):
        bg = act[:, inner + g * gw:inner + (g + 1) * gw]
        cg = act[:, inner + SSM_GROUPS * gw + g * gw:inner + SSM_GROUPS * gw + (g + 1) * gw]
        cg_b = cg.astype(BF16)
        cb = lax.dot_general(cg_b, bg.astype(BF16), (((1,), (1,)), ((), ())), preferred_element_type=F32)
        bg_t = bg.T
        for pair in range(2):
            h0 = g * 4 + pair * 2
            pidx = h0 // 2
            xpp = xs_b[:, pidx * LANE:(pidx + 1) * LANE]
            rhs = jnp.concatenate([jnp.where(half, xpp, zero_b), jnp.where(half, zero_b, xpp)], axis=0)
            dec, dst, eoff, cdec = [], [], [], []
            for h in (h0, h0 + 1):
                cs_col = cs[:, h:h + 1]
                cs_row = cs_t[h:h + 1, :]
                dt_row = dt_t[h:h + 1, :]
                dec.append(cb * (jnp.exp(jnp.where(tri, cs_col - cs_row, NEG)) * dt_row))
                cs_last = cs_row[:, L - 1:L]
                dst.append(bg_t * (jnp.exp(cs_last - cs_row) * dt_row))
                eoff.append(jnp.exp(cs_col))
                cdec.append(jnp.exp(cs_last))
            y_diag = jnp.dot(jnp.concatenate(dec, axis=1).astype(BF16), rhs, preferred_element_type=F32)
            st_new = jnp.dot(jnp.concatenate(dst, axis=1).astype(BF16), rhs, preferred_element_type=F32)
            prev = state[pidx]
            y_off = jnp.dot(cg_b, prev.astype(BF16), preferred_element_type=F32)
            y_off = y_off * jnp.where(half, eoff[0], eoff[1])
            state[pidx] = prev * jnp.where(half, cdec[0], cdec[1]) + st_new
            y = y_diag + y_off + dsk_ref[:, pidx * LANE:(pidx + 1) * LANE] * xs[:, pidx * LANE:(pidx + 1) * LANE]
            out_pair = y * _silu(z_ref[rs, pidx * LANE:(pidx + 1) * LANE].astype(F32))
            ystage[rs, pidx * LANE:(pidx + 1) * LANE] = out_pair

    gsz = inner // SSM_GROUPS
    for g in range(SSM_GROUPS):
        yg = ystage[rs, g * gsz:(g + 1) * gsz]
        ms = jnp.mean(yg * yg, axis=-1, keepdims=True)
        out_ref[rs, g * gsz:(g + 1) * gsz] = (yg * lax.rsqrt(ms + NORM_EPS) * nw_ref[:, g * gsz:(g + 1) * gsz]).astype(BF16)


def _ssd(proj, dt_raw, conv_w, conv_b, dt_bias, a_log, d_skip, ssm_norm_w, batch, seq):
    t = batch * seq
    L = SSM_CHUNK * SSM_CHUNKS_PER_STEP
    nc = seq // L
    inner = SSM_HEADS * SSM_HEADDIM
    cdim = conv_w.shape[1]
    pad16 = lambda v: jnp.pad(v.astype(F32), (0, LANE - SSM_HEADS)).reshape(1, LANE)
    dsk = jnp.repeat(d_skip.astype(F32), SSM_HEADDIM).reshape(1, inner)
    rowc = lambda b, c: (b * nc + c, 0)
    const = lambda b, c: (0, 0)
    return pl.pallas_call(
        _ssd_kernel,
        grid=(batch, nc),
        in_specs=[pl.BlockSpec((L, cdim), lambda b, c: (b * nc + c, COL_XBC // cdim)),
                  pl.BlockSpec((L, inner), lambda b, c: (b * nc + c, COL_Z // inner)),
                  pl.BlockSpec((L, LANE), rowc),
                  pl.BlockSpec((SSM_CONV, cdim), const),
                  pl.BlockSpec((1, cdim), const),
                  pl.BlockSpec((1, LANE), const),
                  pl.BlockSpec((1, LANE), const),
                  pl.BlockSpec((1, inner), const),
                  pl.BlockSpec((1, inner), const)],
        out_specs=pl.BlockSpec((L, inner), rowc),
        out_shape=jax.ShapeDtypeStruct((t, inner), BF16),
        scratch_shapes=[pltpu.VMEM((SSM_CHUNK + L, cdim), BF16), pltpu.VMEM((L, inner), F32),
                        pltpu.VMEM((SSM_HEADS // 2, SSM_STATE, 2 * SSM_HEADDIM), F32)],
        compiler_params=_cp(("arbitrary", "arbitrary")),
        name="ssd",
    )(proj, proj, dt_raw, conv_w.astype(F32), conv_b.reshape(1, cdim).astype(F32), pad16(dt_bias), pad16(a_log),
      dsk, ssm_norm_w.reshape(1, inner).astype(F32))


def _mix_kernel(o0_ref, o1_ref, o2_ref, s0_ref, s1_ref, s2_ref, ssm_ref, ga_ref, gs_ref, x_ref,
                g1_ref, sc2_ref, sh2_ref, g2_ref, nw_ref, wba_ref, wbs_ref, wo_ref, wrt_ref,
                wgs_ref, wus_ref, wds_ref, base_ref, u2p_ref, sct_ref):
    o_refs = (o0_ref, o1_ref, o2_ref)
    s_refs = (s0_ref, s1_ref, s2_ref)
    tm = x_ref.shape[0]
    sub = tm // MIX_SPLIT
    for part in range(MIX_SPLIT):
        rs = slice(part * sub, (part + 1) * sub)
        heads = []
        for h in range(HEADS_PER_GROUP):
            ms = [s[rs, h:h + 1] for s in s_refs]
            ls = [s[rs, HEADS_PER_GROUP + h:HEADS_PER_GROUP + h + 1] for s in s_refs]
            mx = jnp.maximum(jnp.maximum(ms[0], ms[1]), ms[2])
            wts = [l * jnp.exp(m - mx) for m, l in zip(ms, ls)]
            num = wts[0] * o_refs[0][h, rs, :] + wts[1] * o_refs[1][h, rs, :] + wts[2] * o_refs[2][h, rs, :]
            heads.append((num / (wts[0] + wts[1] + wts[2])).astype(BF16))
        attn = jnp.concatenate(heads, axis=1)
        ya = jnp.dot(attn, wba_ref[...], preferred_element_type=F32)
        ys = jnp.dot(ssm_ref[rs, :], wbs_ref[...], preferred_element_type=F32)
        merged = _sigmoid(ga_ref[rs, :].astype(F32)) * ya + _sigmoid(gs_ref[rs, :].astype(F32)) * ys
        mix = jnp.dot(merged.astype(BF16), wo_ref[...], preferred_element_type=F32)
        h1 = x_ref[rs, :] + g1_ref[...] * mix
        ms2 = jnp.mean(h1 * h1, axis=-1, keepdims=True)
        u2 = h1 * lax.rsqrt(ms2 + NORM_EPS) * nw_ref[...] * (1.0 + sc2_ref[...]) + sh2_ref[...]
        for cc, chunk in enumerate(_pack_row_chunks(u2)):
            u2p_ref[pl.ds(part * sub * ROW_PACK + cc, sub, stride=ROW_PACK), :] = chunk
        u2b = u2.astype(BF16)
        u2lo = (u2 - u2b.astype(F32)).astype(BF16)
        nt = (((1,), (1,)), ((), ()))
        logits_t = (lax.dot_general(wrt_ref[0], u2b, nt, preferred_element_type=F32)
                    + lax.dot_general(wrt_ref[0], u2lo, nt, preferred_element_type=F32)
                    + lax.dot_general(wrt_ref[1], u2b, nt, preferred_element_type=F32))
        sct_ref[:, rs] = _sigmoid(logits_t)
        hs_ = (_silu(jnp.dot(u2b, wgs_ref[...], preferred_element_type=F32))
               * jnp.dot(u2b, wus_ref[...], preferred_element_type=F32))
        shared = jnp.dot(hs_.astype(BF16), wds_ref[...], preferred_element_type=F32)
        base_ref[rs, :] = h1 + g2_ref[...] * shared


def _mix(o_list, st_list, ssm, proj, x2, g1, sc2, sh2, g2, norm_w, w_ba, w_bs, w_o, w_rt, w_gs, w_us, w_ds, seq):
    t, d = x2.shape
    tm = MIX_TM
    per_b = seq // tm
    row = lambda i: (i, 0)
    modrow = lambda i: (i // per_b, 0, 0)
    full = lambda a: pl.BlockSpec(a.shape, lambda i: (0,) * a.ndim, pipeline_mode=pl.Buffered(1))
    mod = pl.BlockSpec((None, 1, d), modrow)
    return pl.pallas_call(
        _mix_kernel,
        grid=(t // tm,),
        in_specs=[pl.BlockSpec((HEADS_PER_GROUP, tm, HEAD_DIM), lambda i: (0, i, 0))] * 3
        + [pl.BlockSpec((tm, LANE), row)] * 3 + [
            pl.BlockSpec((tm, d), row),
            pl.BlockSpec((tm, d), lambda i: (i, COL_GA // d)),
            pl.BlockSpec((tm, d), lambda i: (i, COL_GS // d)),
            pl.BlockSpec((tm, d), row),
            mod, mod, mod, mod, full(norm_w), full(w_ba), full(w_bs), full(w_o), full(w_rt),
            full(w_gs), full(w_us), full(w_ds)],
        out_specs=[pl.BlockSpec((tm, d), row), pl.BlockSpec((tm * ROW_PACK, LANE), row),
                   pl.BlockSpec((N_EXPERTS, tm), lambda i: (0, i))],
        out_shape=[jax.ShapeDtypeStruct((t, d), F32), jax.ShapeDtypeStruct((t * ROW_PACK, LANE), U32),
                   jax.ShapeDtypeStruct((N_EXPERTS, t), F32)],
        compiler_params=_cp(("arbitrary",)),
        name="mix",
    )(*o_list, *st_list, ssm, proj, proj, x2, g1, sc2, sh2, g2, norm_w, w_ba, w_bs, w_o, w_rt, w_gs, w_us, w_ds)


def _route_kernel(sct_ref, bias_ref, idx_ref, w_ref, mem_ref, cnt_ref, run):
    i = pl.program_id(0)
    tm = sct_ref.shape[1]
    per_g = N_EXPERTS // N_EXPERT_GROUPS

    @pl.when(i == 0)
    def _():
        run[...] = jnp.zeros(run.shape, F32)

    s = sct_ref[...]
    biased = s + bias_ref[...]
    io_g = lax.broadcasted_iota(I32, (per_g, tm), 0).astype(F32)
    gscore = []
    for g in range(N_EXPERT_GROUPS):
        bgp = biased[g * per_g:(g + 1) * per_g, :]
        m1 = jnp.max(bgp, axis=0, keepdims=True)
        first = jnp.min(jnp.where(bgp == m1, io_g, float(per_g)), axis=0, keepdims=True)
        m2 = jnp.max(jnp.where(io_g == first, NEG, bgp), axis=0, keepdims=True)
        gscore.append(m1 + m2)
    gs = jnp.concatenate(gscore, axis=0)
    io8 = lax.broadcasted_iota(I32, (N_EXPERT_GROUPS, tm), 0).astype(F32)
    gsel = jnp.zeros((N_EXPERT_GROUPS, tm), F32)
    cur = gs
    for _ in range(TOPK_GROUPS):
        mx = jnp.max(cur, axis=0, keepdims=True)
        fi = jnp.min(jnp.where(cur == mx, io8, float(N_EXPERT_GROUPS)), axis=0, keepdims=True)
        hit = io8 == fi
        gsel = jnp.where(hit, 1.0, gsel)
        cur = jnp.where(hit, NEG, cur)
    masked = jnp.concatenate(
        [jnp.where(gsel[g:g + 1, :] > 0.5, biased[g * per_g:(g + 1) * per_g, :], NEG) for g in range(N_EXPERT_GROUPS)],
        axis=0)
    io_e = lax.broadcasted_iota(I32, (N_EXPERTS, tm), 0).astype(F32)
    member = jnp.zeros((N_EXPERTS, tm), F32)
    idxs, ws = [], []
    for _ in range(TOP_K):
        mx = jnp.max(masked, axis=0, keepdims=True)
        fi = jnp.min(jnp.where(masked == mx, io_e, float(N_EXPERTS)), axis=0, keepdims=True)
        hit = io_e == fi
        idxs.append(fi)
        ws.append(jnp.sum(jnp.where(hit, s, 0.0), axis=0, keepdims=True))
        member = jnp.where(hit, 1.0, member)
        masked = jnp.where(hit, NEG, masked)
    wsum = ws[0]
    for k in range(1, TOP_K):
        wsum = wsum + ws[k]
    idx_ref[...] = jnp.concatenate(idxs, axis=0).astype(I32)
    w_ref[...] = jnp.concatenate([w / wsum * ROUTED_SCALE for w in ws], axis=0)
    mem_ref[...] = member.astype(BF16)
    new_run = run[...] + jnp.sum(member, axis=1, keepdims=True)
    run[...] = new_run
    cnt_ref[...] = new_run


def _route(scores_t, router_bias):
    e, t = scores_t.shape
    tm = ROUTE_TM
    tok = pl.BlockSpec((TOP_K, tm), lambda i: (0, i))
    return pl.pallas_call(
        _route_kernel,
        grid=(t // tm,),
        in_specs=[pl.BlockSpec((e, tm), lambda i: (0, i)), pl.BlockSpec((e, 1), lambda i: (0, 0))],
        out_specs=[tok, tok, pl.BlockSpec((e, tm), lambda i: (0, i)), pl.BlockSpec((e, LANE), lambda i: (0, 0))],
        out_shape=[jax.ShapeDtypeStruct((TOP_K, t), I32), jax.ShapeDtypeStruct((TOP_K, t), F32),
                   jax.ShapeDtypeStruct((e, t), BF16), jax.ShapeDtypeStruct((e, LANE), F32)],
        scratch_shapes=[pltpu.VMEM((e, LANE), F32)],
        compiler_params=_cp(("arbitrary",)),
        name="route",
    )(scores_t, router_bias.reshape(e, 1).astype(F32))


def _dest_kernel(mem_ref, idx_ref, start_ref, dest_ref, run):
    i = pl.program_id(0)
    e, tm = mem_ref.shape

    @pl.when(i == 0)
    def _():
        run[...] = jnp.broadcast_to(start_ref[...], run.shape)

    member = mem_ref[...]
    tr = lax.broadcasted_iota(I32, (tm, tm), 0)
    tc = lax.broadcasted_iota(I32, (tm, tm), 1)
    upper = jnp.where(tr < tc, 1.0, 0.0).astype(BF16)
    rank_full = jnp.dot(member, upper, preferred_element_type=F32) + run[:, 0:1]
    io_e = lax.broadcasted_iota(I32, (e, tm), 0)
    idx = idx_ref[...]
    rows = [jnp.sum(jnp.where(io_e == idx[k:k + 1, :], rank_full, 0.0), axis=0, keepdims=True) for k in range(TOP_K)]
    dest_ref[...] = jnp.concatenate(rows, axis=0).astype(I32)
    run[...] = run[...] + jnp.sum(member.astype(F32), axis=1, keepdims=True)


def _dest(member, idx, pad_start):
    e, t = member.shape
    tm = ROUTE_TM
    return pl.pallas_call(
        _dest_kernel,
        grid=(t // tm,),
        in_specs=[pl.BlockSpec((e, tm), lambda i: (0, i)), pl.BlockSpec((TOP_K, tm), lambda i: (0, i)),
                  pl.BlockSpec((e, 1), lambda i: (0, 0))],
        out_specs=pl.BlockSpec((TOP_K, tm), lambda i: (0, i)),
        out_shape=jax.ShapeDtypeStruct((TOP_K, t), I32),
        scratch_shapes=[pltpu.VMEM((e, LANE), F32)],
        compiler_params=_cp(("arbitrary",)),
        name="dest",
    )(member, idx, pad_start.astype(F32).reshape(e, 1))


def _dispatch_kernel(pend_ref, npad_ref, dest_ref, u_ref, xs_ref, zero_scr, sem, zsem):
    i = pl.program_id(0)
    rt = ROW_PACK
    tm = u_ref.shape[0] // rt
    zrows = zero_scr.shape[0]

    @pl.when(i == 0)
    def _():
        zero_scr[...] = jnp.zeros(zero_scr.shape, zero_scr.dtype)

        def zcopy(e):
            start = pl.multiple_of(pend_ref[e] * rt - zrows, zrows)
            return pltpu.make_async_copy(zero_scr, xs_ref.at[pl.ds(start, zrows)], zsem)

        def zstart(e, c):
            @pl.when(npad_ref[e] > 0)
            def _():
                zcopy(e).start()
            return c

        def zwait(e, c):
            @pl.when(npad_ref[e] > 0)
            def _():
                zcopy(e).wait()
            return c

        lax.fori_loop(0, N_EXPERTS, zstart, 0)
        lax.fori_loop(0, N_EXPERTS, zwait, 0)

    def row_copy(t, k):
        src = u_ref.at[pl.ds(pl.multiple_of(t * rt, rt), rt)]
        dst = xs_ref.at[pl.ds(pl.multiple_of(dest_ref[t * TOP_K + k] * rt, rt), rt)]
        return pltpu.make_async_copy(src, dst, sem)

    def start(t, c):
        for k in range(TOP_K):
            row_copy(t, k).start(priority=k % 2)
        return c

    def wait(t, c):
        for k in range(TOP_K):
            row_copy(t, k).wait()
        return c

    lax.fori_loop(0, tm, start, 0)
    lax.fori_loop(0, tm, wait, 0)


def _dispatch(u2p, dest_flat, pad_end, padded, n_rows):
    rt = ROW_PACK
    t = u2p.shape[0] // rt
    tm = DISPATCH_TM
    gs = pltpu.PrefetchScalarGridSpec(
        num_scalar_prefetch=2,
        grid=(t // tm,),
        in_specs=[pl.BlockSpec((tm * TOP_K,), lambda i, pe, npd: (i,), memory_space=pltpu.SMEM),
                  pl.BlockSpec((tm * rt, LANE), lambda i, pe, npd: (i, 0))],
        out_specs=pl.BlockSpec(memory_space=pl.ANY),
        scratch_shapes=[pltpu.VMEM((MOE_ROWS * rt, LANE), U32), pltpu.SemaphoreType.DMA, pltpu.SemaphoreType.DMA],
    )
    return pl.pallas_call(
        _dispatch_kernel,
        grid_spec=gs,
        out_shape=jax.ShapeDtypeStruct((n_rows * rt, LANE), U32),
        compiler_params=_cp(("arbitrary",)),
        name="dispatch",
    )(pad_end, padded, dest_flat, u2p)


def _expert_kernel(bstart_ref, nblk_ref, wg_ref, wu_ref, wd_ref, xs_ref, ys_ref,
                   wg_b, wu_b, wd_b, xbuf, ybuf, in_sem, out_sem):
    e = pl.program_id(0)
    last = pl.num_programs(0) - 1
    rt = ROW_PACK
    n_in, rb = xbuf.shape[0], xbuf.shape[1] // rt
    n_out = ybuf.shape[0]
    n = nblk_ref[e]
    g0 = bstart_ref[e]
    total = bstart_ref[last] + nblk_ref[last]

    def block_rows(g):
        return pl.ds(pl.multiple_of(g * (rb * rt), rb * rt), rb * rt)

    def in_copy(g):
        return pltpu.make_async_copy(xs_ref.at[block_rows(g)], xbuf.at[g % n_in], in_sem.at[g % n_in])

    def out_copy(g):
        return pltpu.make_async_copy(ybuf.at[g % n_out], ys_ref.at[block_rows(g)], out_sem.at[g % n_out])

    @pl.when(e == 0)
    def _():
        for g in range(n_in - 1):
            @pl.when(g < total)
            def _():
                in_copy(g).start()

    wg_b[...] = wg_ref[...].astype(BF16)
    wu_b[...] = wu_ref[...].astype(BF16)
    wd_b[...] = wd_ref[...].astype(BF16)

    def body(b, carry):
        g = g0 + b
        in_copy(g).wait()

        @pl.when(g + n_in - 1 < total)
        def _():
            in_copy(g + n_in - 1).start()

        @pl.when(g >= n_out)
        def _():
            out_copy(g - n_out).wait()

        halves = [_unpack_pair(xbuf[g % n_in, pl.ds(cc, rb, stride=rt), :]) for cc in range(rt)]
        xb = jnp.concatenate([lo.astype(BF16) for lo, _ in halves] + [hi.astype(BF16) for _, hi in halves], axis=1)
        gate = jnp.dot(xb, wg_b[...], preferred_element_type=F32)
        up = jnp.dot(xb, wu_b[...], preferred_element_type=F32)
        h = (_silu(gate) * up).astype(BF16)
        y = jnp.dot(h, wd_b[...], preferred_element_type=F32)
        for cc, chunk in enumerate(_pack_row_chunks(y)):
            ybuf[g % n_out, pl.ds(cc, rb, stride=rt), :] = chunk
        out_copy(g).start()
        return carry

    lax.fori_loop(0, n, body, 0)

    @pl.when(e == last)
    def _():
        for j in range(n_out, 0, -1):
            @pl.when(total >= j)
            def _():
                out_copy(total - j).wait()


def _experts(xs, blk_start, nblk, w_gate_e, w_up_e, w_down_e):
    ne, d, ff = w_gate_e.shape
    rb = MOE_ROWS * ROW_PACK
    gs = pltpu.PrefetchScalarGridSpec(
        num_scalar_prefetch=2,
        grid=(ne,),
        in_specs=[pl.BlockSpec((None, d, ff), lambda e, ps, nb: (e, 0, 0)),
                  pl.BlockSpec((None, d, ff), lambda e, ps, nb: (e, 0, 0)),
                  pl.BlockSpec((None, ff, d), lambda e, ps, nb: (e, 0, 0)),
                  pl.BlockSpec(memory_space=pl.ANY)],
        out_specs=pl.BlockSpec(memory_space=pl.ANY),
        scratch_shapes=[pltpu.VMEM((d, ff), BF16), pltpu.VMEM((d, ff), BF16), pltpu.VMEM((ff, d), BF16),
                        pltpu.VMEM((EXPERT_IN_SLOTS, rb, LANE), U32), pltpu.VMEM((EXPERT_OUT_SLOTS, rb, LANE), U32),
                        pltpu.SemaphoreType.DMA((EXPERT_IN_SLOTS,)), pltpu.SemaphoreType.DMA((EXPERT_OUT_SLOTS,))],
    )
    return pl.pallas_call(
        _expert_kernel,
        grid_spec=gs,
        out_shape=jax.ShapeDtypeStruct(xs.shape, U32),
        compiler_params=_cp(("arbitrary",)),
        name="experts",
    )(blk_start, nblk, w_gate_e, w_up_e, w_down_e, xs)


def _combine_kernel(dest_ref, dnext_ref, wt_ref, base_ref, g2_ref, nfw_ref, ys_ref, out_ref, buf, sem):
    i = pl.program_id(0)
    n = pl.num_programs(0)
    tm = base_ref.shape[0]
    rt = ROW_PACK
    half = i % 2

    def row_copy(d_ref, h, t, k):
        src = ys_ref.at[pl.ds(pl.multiple_of(d_ref[t * TOP_K + k] * rt, rt), rt)]
        return pltpu.make_async_copy(src, buf.at[h, k, pl.ds(pl.multiple_of(t * rt, rt), rt)], sem.at[h])

    def issue(d_ref, h):
        def start(t, c):
            for k in range(TOP_K):
                row_copy(d_ref, h, t, k).start(priority=k % 2)
            return c
        lax.fori_loop(0, tm, start, 0)

    @pl.when(i == 0)
    def _():
        issue(dest_ref, 0)

    @pl.when(i + 1 < n)
    def _():
        issue(dnext_ref, 1 - half)

    def wait(t, c):
        for k in range(TOP_K):
            row_copy(dest_ref, half, t, k).wait()
        return c

    lax.fori_loop(0, tm, wait, 0)

    upper = lax.broadcasted_iota(I32, (2 * rt, LANE), 0) >= rt

    def weigh(t2, c):
        rows = pl.ds(pl.multiple_of(t2 * (2 * rt), 2 * rt), 2 * rt)
        acc_lo = jnp.zeros((2 * rt, LANE), F32)
        acc_hi = jnp.zeros((2 * rt, LANE), F32)
        for k in range(TOP_K):
            lo, hi = _unpack_pair(buf[half, k, rows, :])
            wv = jnp.where(upper, wt_ref[(2 * t2 + 1) * TOP_K + k], wt_ref[2 * t2 * TOP_K + k])
            acc_lo = acc_lo + lo * wv
            acc_hi = acc_hi + hi * wv
        buf[half, 0, rows, :] = lax.bitcast_convert_type(acc_lo, U32)
        buf[half, 1, rows, :] = lax.bitcast_convert_type(acc_hi, U32)
        return c

    lax.fori_loop(0, tm // 2, weigh, 0, unroll=2)
    ssq = jnp.zeros((tm, 1), F32)
    for cc in range(2 * rt):
        cs = slice(cc * LANE, (cc + 1) * LANE)
        routed = lax.bitcast_convert_type(buf[half, cc // rt, pl.ds(cc % rt, tm, stride=rt), :], F32)
        h2 = base_ref[:, cs] + g2_ref[:, cs] * routed
        out_ref[:, cs] = h2
        ssq = ssq + jnp.sum(h2 * h2, axis=-1, keepdims=True)
    inv = lax.rsqrt(ssq / (2 * rt * LANE) + NORM_EPS)
    out_ref[...] = out_ref[...] * inv * nfw_ref[...]


def _combine(ys, dest_flat, w_tok, base, g2, norm_final_w, seq):
    t, d = base.shape
    tm = COMBINE_TM
    per_b = seq // tm
    gs = pltpu.PrefetchScalarGridSpec(
        num_scalar_prefetch=0,
        grid=(t // tm,),
        in_specs=[pl.BlockSpec((tm * TOP_K,), lambda i: (i,), memory_space=pltpu.SMEM),
                  pl.BlockSpec((tm * TOP_K,), lambda i: (jnp.minimum(i + 1, t // tm - 1),), memory_space=pltpu.SMEM),
                  pl.BlockSpec((tm * TOP_K,), lambda i: (i,), memory_space=pltpu.SMEM),
                  pl.BlockSpec((tm, d), lambda i: (i, 0)),
                  pl.BlockSpec((None, 1, d), lambda i: (i // per_b, 0, 0)),
                  pl.BlockSpec((1, d), lambda i: (0, 0)),
                  pl.BlockSpec(memory_space=pl.ANY)],
        out_specs=pl.BlockSpec((tm, d), lambda i: (i, 0)),
        scratch_shapes=[pltpu.VMEM((2, TOP_K, tm * ROW_PACK, LANE), U32), pltpu.SemaphoreType.DMA((2,))],
    )
    return pl.pallas_call(
        _combine_kernel,
        grid_spec=gs,
        out_shape=jax.ShapeDtypeStruct((t, d), F32),
        compiler_params=_cp(("arbitrary",)),
        name="combine",
    )(dest_flat, dest_flat, w_tok, base, g2, norm_final_w, ys)


def kernel(x, c, positions, w_mod, b_mod, norm_mix_w, norm_ffn_w, w_in, conv_w, conv_b, dt_bias, a_log, d_skip,
           ssm_norm_w, w_branch_attn, w_branch_ssm, w_out, w_router, router_bias, w_gate_e, w_up_e, w_down_e,
           w_gate_s, w_up_s, w_down_s, norm_final_w):
    batch, seq, d = x.shape
    t = batch * seq
    assert w_mod.shape[0] == 1, "one layer"
    assert d == 2 * ROW_PACK * LANE and seq % INPROJ_TM == 0 and INPROJ_TM % (ATTN_DILATIONS[-1] * 16) == 0

    mod = _modulation(c, w_mod[0], b_mod[0])
    sh1, sc1, g1, sh2, sc2, g2 = [m.reshape(batch, 1, d) for m in jnp.split(mod, 6, axis=-1)]
    rope_c, rope_s1, rope_s2 = _rope_tables(positions)

    wi = w_in[0]
    q_dim = 3 * GROUP_W
    o_z = 3 * q_dim
    o_xbc = o_z + d
    o_dt = o_xbc + conv_w.shape[2]
    o_g = o_dt + SSM_HEADS
    qkv = lambda g: [wi[:, s * q_dim + g * GROUP_W:s * q_dim + (g + 1) * GROUP_W] for s in range(3)]
    w_packed = jnp.concatenate([wi[:, o_xbc:o_dt], wi[:, o_g:], wi[:, o_z:o_xbc]] + qkv(0) + qkv(1) + qkv(2),
                               axis=1).astype(BF16)
    assert w_packed.shape[1] == MAIN_W + 2 * QKV_W
    w_dt = jnp.pad(wi[:, o_dt:o_g], ((0, 0), (0, LANE - SSM_HEADS))).astype(BF16)

    x2 = x.reshape(t, d)
    proj, qkv1, qkv2, dt_raw = _inproj(x2, sc1, sh1, norm_mix_w.reshape(1, d), w_packed, w_dt,
                                       rope_c, rope_s1, rope_s2, batch, seq)

    srcs = [(proj.reshape(batch, 1, seq, MAIN_W), COL_Q0 // GROUP_W), (qkv1, 0), (qkv2, 0)]
    o_list, st_list = [], []
    for g, (src, col0) in enumerate(srcs):
        o, st = _attention_group(src, g, batch, seq, col0)
        o_list.append(o)
        st_list.append(st)
    ssm = _ssd(proj, dt_raw, conv_w[0], conv_b[0], dt_bias[0], a_log[0], d_skip[0], ssm_norm_w[0], batch, seq)

    wr_t = w_router[0].T
    wr_hi = wr_t.astype(BF16)
    w_rt = jnp.stack([wr_hi, (wr_t - wr_hi.astype(F32)).astype(BF16)])
    base, u2p, scores_t = _mix(
        o_list, st_list, ssm, proj, x2, g1, sc2, sh2, g2, norm_ffn_w.reshape(1, d),
        w_branch_attn[0].astype(BF16), w_branch_ssm[0].astype(BF16), w_out[0].astype(BF16),
        w_rt, w_gate_s[0].astype(BF16), w_up_s[0].astype(BF16), w_down_s[0].astype(BF16), seq)

    idx, w_sel, member, counts = _route(scores_t, router_bias[0])

    rb = MOE_ROWS
    cnt = counts[:, 0].astype(I32)
    padded = (cnt + rb - 1) // rb * rb
    pad_end = jnp.cumsum(padded).astype(I32)
    pad_start = pad_end - padded
    n_rows = t * TOP_K + N_EXPERTS * rb

    dest_flat = _dest(member, idx, pad_start).T.reshape(t * TOP_K)
    xs = _dispatch(u2p, dest_flat, pad_end, padded, n_rows)
    ys = _experts(xs, pad_start // rb, padded // rb, w_gate_e[0], w_up_e[0], w_down_e[0])
    out = _combine(ys, dest_flat, w_sel.T.reshape(t * TOP_K), base, g2, norm_final_w.reshape(1, d), seq)
    return out.reshape(batch, seq, d)
```

```python
import functools
import math

import jax
import jax.numpy as jnp
from jax import lax
from jax.experimental import pallas as pl
from jax.experimental.pallas import tpu as pltpu

F32 = jnp.float32
BF16 = jnp.bfloat16
I32 = jnp.int32
U32 = jnp.uint32

LANE = 128
VMEM_LIMIT = 56 * 1024 * 1024

HEAD_DIM = 128
HEADS_PER_GROUP = 4
GROUP_W = HEADS_PER_GROUP * HEAD_DIM
ATTN_DILATIONS = (1, 4, 16)
ATTN_BLK = 128
ROPE_DIM = 32
ROPE_HALF = 16
ROPE_THETA = 500000.0
SSM_HEADS = 16
SSM_HEADDIM = 64
SSM_GROUPS = 4
SSM_STATE = 128
SSM_CONV = 4
SSM_CHUNK = 128
N_EXPERTS = 256
TOP_K = 8
N_EXPERT_GROUPS = 8
TOPK_GROUPS = 4
ROUTED_SCALE = 2.5
NORM_EPS = 1e-6
NEG = -1e30

COL_XBC, COL_GA, COL_GS, COL_Z, COL_Q0 = 0, 2048, 3072, 4096, 5120
MAIN_W = 6656
QKV_W = 3 * GROUP_W

ROPE_TM = 2048
SSM_CHUNKS_PER_STEP = 4
INPROJ_TM, INPROJ_TN = 512, 512
MIX_TM, MIX_SPLIT = 512, 2
ROUTE_TM = 512
MOE_ROWS = 256
EXPERT_IN_SLOTS, EXPERT_OUT_SLOTS = 6, 4
DISPATCH_TM = 256
COMBINE_TM = 256
ROW_PACK = 4


def _cp(sem, vmem=VMEM_LIMIT):
    return pltpu.CompilerParams(dimension_semantics=sem, vmem_limit_bytes=vmem)


def _sigmoid(x):
    return 1.0 / (1.0 + jnp.exp(-x))


def _silu(x):
    return x * _sigmoid(x)


def _pack_pair(a, b):
    ua = lax.bitcast_convert_type(a.astype(BF16).astype(F32), U32)
    ub = lax.bitcast_convert_type(b.astype(BF16).astype(F32), U32)
    return (ua >> 16) | ub


def _unpack_pair(w):
    lo = lax.bitcast_convert_type(w << 16, F32)
    hi = lax.bitcast_convert_type(w & jnp.uint32(0xFFFF0000), F32)
    return lo, hi


def _pack_row_chunks(v):
    return [_pack_pair(v[:, c * LANE:(c + 1) * LANE], v[:, (c + ROW_PACK) * LANE:(c + ROW_PACK + 1) * LANE])
            for c in range(ROW_PACK)]


def _mod_kernel(c_ref, w_ref, b_ref, o_ref):
    cond = _silu(c_ref[...])
    o_ref[...] = jnp.dot(cond, w_ref[...], preferred_element_type=F32) + b_ref[...]


def _modulation(c, w_mod, b_mod):
    b, d = c.shape
    n = w_mod.shape[1]
    return pl.pallas_call(
        _mod_kernel,
        grid=(n // d,),
        in_specs=[pl.BlockSpec((b, d), lambda j: (0, 0)),
                  pl.BlockSpec((d, d), lambda j: (0, j)),
                  pl.BlockSpec((1, d), lambda j: (0, j))],
        out_specs=pl.BlockSpec((b, d), lambda j: (0, j)),
        out_shape=jax.ShapeDtypeStruct((b, n), F32),
        compiler_params=_cp(("arbitrary",)),
        name="modulation",
    )(c, w_mod, b_mod.reshape(1, n))


def _rope_kernel(pos_ref, inv_ref, c_ref, s1_ref, s2_ref):
    ang = pos_ref[...].astype(F32) * inv_ref[...]
    lane = lax.broadcasted_iota(I32, ang.shape, 1)
    cos = jnp.cos(ang)
    sin = jnp.sin(ang)
    c_ref[...] = jnp.where(lane < ROPE_DIM, cos, 1.0)
    s1_ref[...] = jnp.where(lane < ROPE_HALF, -sin, 0.0)
    s2_ref[...] = jnp.where((lane >= ROPE_HALF) & (lane < ROPE_DIM), sin, 0.0)


def _rope_tables(positions):
    t = positions.size
    tm = ROPE_TM
    inv_freq = ROPE_THETA ** (-jnp.arange(ROPE_HALF, dtype=F32) / ROPE_HALF)
    inv_row = jnp.concatenate([inv_freq, inv_freq, jnp.zeros((LANE - ROPE_DIM,), F32)]).reshape(1, LANE)
    spec = pl.BlockSpec((tm, LANE), lambda i: (i, 0))
    shp = jax.ShapeDtypeStruct((t, LANE), F32)
    return pl.pallas_call(
        _rope_kernel,
        grid=(t // tm,),
        in_specs=[pl.BlockSpec((tm, 1), lambda i: (i, 0)), pl.BlockSpec((1, LANE), lambda i: (0, 0))],
        out_specs=[spec, spec, spec],
        out_shape=[shp, shp, shp],
        compiler_params=_cp(("arbitrary",)),
        name="rope_tables",
    )(positions.reshape(t, 1), inv_row)


def _inproj_kernel(x_ref, sc_ref, sh_ref, nw_ref, w_ref, wdt_ref, c_ref, s1_ref, s2_ref,
                   main_ref, g1_ref, g2_ref, dt_ref, u_scr, rope_scr, uc_scr, *, tn, q_scale):
    tm = x_ref.shape[0]
    n_main = MAIN_W // tn
    n_qkv = QKV_W // tn

    x = x_ref[...]
    ms = jnp.mean(x * x, axis=-1, keepdims=True)
    y = x * lax.rsqrt(ms + NORM_EPS) * nw_ref[...]
    uf = y * (1.0 + sc_ref[...]) + sh_ref[...]
    u = uf.astype(BF16)
    dt_ref[...] = jnp.dot(u, wdt_ref[...], preferred_element_type=F32)
    n_chunk = uf.shape[1] // LANE
    for cc in range(n_chunk):
        uc_scr[cc] = uf[:, cc * LANE:(cc + 1) * LANE]
    for o, d in enumerate(ATTN_DILATIONS[1:]):
        rows = tm // d
        for r in range(d):
            for cc in range(n_chunk):
                u_scr[o, r * rows:(r + 1) * rows, cc * LANE:(cc + 1) * LANE] = (
                    uc_scr[cc, pl.ds(r, rows, stride=d), :].astype(BF16))
            for ti, tab in enumerate((c_ref, s1_ref, s2_ref)):
                rope_scr[o, ti, r * rows:(r + 1) * rows, :] = tab[pl.ds(r, rows, stride=d), :]

    for c in range(n_main + 2 * n_qkv):
        order = 0 if c < n_main else (1 if c < n_main + n_qkv else 2)
        jq = c - COL_Q0 // tn if order == 0 else (c - n_main - (order - 1) * n_qkv)
        lhs = u if order == 0 else u_scr[order - 1]
        acc = jnp.dot(lhs, w_ref[:, c * tn:(c + 1) * tn], preferred_element_type=F32)
        if jq in (0, 1):
            scale = q_scale if jq == 0 else 1.0
            tabs = (c_ref, s1_ref, s2_ref) if order == 0 else tuple(rope_scr.at[order - 1, ti] for ti in range(3))
            cs = tabs[0][...] * scale
            s1 = tabs[1][...] * scale
            s2 = tabs[2][...] * scale
            parts = []
            for h in range(tn // HEAD_DIM):
                a = acc[:, h * HEAD_DIM:(h + 1) * HEAD_DIM]
                parts.append(a * cs + pltpu.roll(a, LANE - ROPE_HALF, 1) * s1 + pltpu.roll(a, ROPE_HALF, 1) * s2)
            acc = jnp.concatenate(parts, axis=1)
        val = acc.astype(BF16)
        if order == 0:
            main_ref[:, c * tn:(c + 1) * tn] = val
        else:
            dst = g1_ref if order == 1 else g2_ref
            c0 = (c - n_main - (order - 1) * n_qkv) * tn
            dst[:, :, c0:c0 + tn] = val.reshape(dst.shape[0], dst.shape[1], tn)


def _inproj(x2, sc1, sh1, norm_w, w_packed, w_dt, rope_c, rope_s1, rope_s2, batch, seq):
    t, d = x2.shape
    tm, tn = INPROJ_TM, INPROJ_TN
    n = w_packed.shape[1]
    per_b = seq // tm
    d1, d2 = ATTN_DILATIONS[1], ATTN_DILATIONS[2]
    row = lambda i: (i, 0)
    modrow = lambda i: (i // per_b, 0, 0)
    const = lambda i: (0, 0)
    resident = pl.Buffered(1)
    qkv_map = lambda i: (i // per_b, 0, i % per_b, 0)
    return pl.pallas_call(
        functools.partial(_inproj_kernel, tn=tn, q_scale=1.0 / math.sqrt(HEAD_DIM)),
        grid=(t // tm,),
        in_specs=[pl.BlockSpec((tm, d), row),
                  pl.BlockSpec((None, 1, d), modrow),
                  pl.BlockSpec((None, 1, d), modrow),
                  pl.BlockSpec((1, d), const),
                  pl.BlockSpec((d, n), const, pipeline_mode=resident),
                  pl.BlockSpec((d, LANE), const),
                  pl.BlockSpec((tm, LANE), row),
                  pl.BlockSpec((tm, LANE), row),
                  pl.BlockSpec((tm, LANE), row)],
        out_specs=[pl.BlockSpec((tm, MAIN_W), row),
                   pl.BlockSpec((None, d1, tm // d1, QKV_W), qkv_map),
                   pl.BlockSpec((None, d2, tm // d2, QKV_W), qkv_map),
                   pl.BlockSpec((tm, LANE), row)],
        out_shape=[jax.ShapeDtypeStruct((t, MAIN_W), BF16),
                   jax.ShapeDtypeStruct((batch, d1, seq // d1, QKV_W), BF16),
                   jax.ShapeDtypeStruct((batch, d2, seq // d2, QKV_W), BF16),
                   jax.ShapeDtypeStruct((t, LANE), F32)],
        scratch_shapes=[pltpu.VMEM((2, tm, d), BF16), pltpu.VMEM((2, 3, tm, LANE), F32),
                        pltpu.VMEM((d // LANE, tm, LANE), F32)],
        compiler_params=_cp(("arbitrary",)),
        name="inproj",
    )(x2, sc1, sh1, norm_w, w_packed, w_dt, rope_c, rope_s1, rope_s2)


def _attn_kernel(q_ref, k_ref, v_ref, o_ref, st_ref, *, d, nb):
    blk = ATTN_BLK
    qi = lax.broadcasted_iota(I32, (blk, 2 * blk), 0)
    kj = lax.broadcasted_iota(I32, (blk, 2 * blk), 1)
    band = (kj >= qi) & (kj <= qi + blk)
    qi1 = lax.broadcasted_iota(I32, (blk, blk), 0)
    kj1 = lax.broadcasted_iota(I32, (blk, blk), 1)
    causal = kj1 <= qi1
    lane = kj1

    hsl = [slice(h * HEAD_DIM, (h + 1) * HEAD_DIM) for h in range(HEADS_PER_GROUP)]

    def blocks(items, nk, mask):
        s = jnp.concatenate(
            [lax.dot_general(q_ref[r, pl.ds(q0, blk), hs], k_ref[r, pl.ds(k0, nk), hs], (((1,), (1,)), ((), ())),
                             preferred_element_type=F32) for r, q0, k0 in items for hs in hsl], axis=0)
        s = jnp.where(jnp.concatenate([mask] * (HEADS_PER_GROUP * len(items)), axis=0), s, NEG)
        m = jnp.max(s, axis=-1, keepdims=True)
        p = jnp.exp(s - m)
        l = jnp.sum(p, axis=-1, keepdims=True)
        pb = p.astype(BF16)
        for it, (r, q0, k0) in enumerate(items):
            rows = pl.ds(q0, blk) if d == 1 else pl.ds(q0 * d + r, blk, stride=d)
            st = jnp.zeros((blk, LANE), F32)
            for h, hs in enumerate(hsl):
                hr = slice((it * HEADS_PER_GROUP + h) * blk, (it * HEADS_PER_GROUP + h + 1) * blk)
                o = jnp.dot(pb[hr], v_ref[r, pl.ds(k0, nk), hs], preferred_element_type=F32)
                o_ref[h, rows, :] = o / l[hr]
                st = jnp.where(lane == h, m[hr], st)
                st = jnp.where(lane == HEADS_PER_GROUP + h, l[hr], st)
            st_ref[rows, :] = st

    def band_item(r, n):
        q0 = pl.multiple_of(n * blk, blk)
        return (r, q0, pl.multiple_of(q0 - blk, blk))

    if d == 1:
        blocks([(0, 0, 0)], blk, causal)

        def body(j, c):
            blocks([band_item(0, 1 + 2 * j), band_item(0, 2 + 2 * j)], 2 * blk, band)
            return c
        lax.fori_loop(0, (nb - 1) // 2, body, 0, unroll=2)
        if (nb - 1) % 2:
            blocks([band_item(0, nb - 1)], 2 * blk, band)
    else:
        def body(j, c):
            blocks([(2 * j, 0, 0), (2 * j + 1, 0, 0)], blk, causal)
            for n in range(1, nb):
                blocks([band_item(2 * j, n), band_item(2 * j + 1, n)], 2 * blk, band)
            return c
        lax.fori_loop(0, d // 2, body, 0, unroll=2 if nb == 1 else 1)


def _attention_group(src, g, batch, seq, col0):
    d = ATTN_DILATIONS[g]
    n_sub = seq // d
    nb = n_sub // ATTN_BLK
    spec = lambda c: pl.BlockSpec((None, d, n_sub, GROUP_W), lambda b: (b, 0, 0, c))
    o, st = pl.pallas_call(
        functools.partial(_attn_kernel, d=d, nb=nb),
        grid=(batch,),
        in_specs=[spec(col0), spec(col0 + 1), spec(col0 + 2)],
        out_specs=[pl.BlockSpec((HEADS_PER_GROUP, seq, HEAD_DIM), lambda b: (0, b, 0)),
                   pl.BlockSpec((seq, LANE), lambda b: (b, 0))],
        out_shape=[jax.ShapeDtypeStruct((HEADS_PER_GROUP, batch * seq, HEAD_DIM), F32),
                   jax.ShapeDtypeStruct((batch * seq, LANE), F32)],
        compiler_params=_cp(("arbitrary",)),
        name=f"attn_d{d}",
    )(src, src, src)
    return o, st


def _ssd_kernel(xbc_ref, z_ref, dt_ref, cw_ref, cb_ref, dtb_ref, alog_ref, dsk_ref, nw_ref,
                out_ref, xwin, ystage, state):
    L = SSM_CHUNK
    nsub = xbc_ref.shape[0] // L

    @pl.when(pl.program_id(1) == 0)
    def _():
        xwin[0:L, :] = jnp.zeros((L, xwin.shape[1]), BF16)
        state[...] = jnp.zeros(state.shape, F32)

    xwin[L:(nsub + 1) * L, :] = xbc_ref[...]
    for j in range(nsub):
        _ssd_chunk(slice(j * L, (j + 1) * L), xwin[j * L:(j + 2) * L, :], xbc_ref, z_ref, dt_ref, cw_ref, cb_ref,
                   dtb_ref, alog_ref, dsk_ref, nw_ref, out_ref, ystage, state)
    xwin[0:L, :] = xbc_ref[(nsub - 1) * L:nsub * L, :]


def _ssd_chunk(rs, win, xbc_ref, z_ref, dt_ref, cw_ref, cb_ref, dtb_ref, alog_ref, dsk_ref, nw_ref,
               out_ref, ystage, state):
    L = SSM_CHUNK
    inner = SSM_HEADS * SSM_HEADDIM
    gw = SSM_STATE

    xcur = xbc_ref[rs, :]
    srow = lax.broadcasted_iota(I32, (L, 2 * L), 0)
    scol = lax.broadcasted_iota(I32, (L, 2 * L), 1)
    conv = cb_ref[...] + cw_ref[SSM_CONV - 1:SSM_CONV, :] * xcur.astype(F32)
    for s in range(1, SSM_CONV):
        shift_m = jnp.where(scol == srow + (L - s), 1.0, 0.0).astype(BF16)
        conv = conv + cw_ref[SSM_CONV - 1 - s:SSM_CONV - s, :] * jnp.dot(shift_m, win, preferred_element_type=F32)
    act = _silu(conv)
    xs = act[:, :inner]
    xs_b = xs.astype(BF16)

    lane = lax.broadcasted_iota(I32, (L, LANE), 1)
    row = lax.broadcasted_iota(I32, (L, LANE), 0)
    dtr = dt_ref[rs, :] + dtb_ref[...]
    dt = jnp.maximum(dtr, 0.0) + jnp.log(1.0 + jnp.exp(-jnp.abs(dtr)))
    a_neg = jnp.where(lane < SSM_HEADS, -jnp.exp(alog_ref[...]), 0.0)
    a = dt * a_neg
    cs = a
    shift = 1
    while shift < L:
        cs = cs + jnp.where(row >= shift, pltpu.roll(cs, shift, 0), 0.0)
        shift *= 2
    cs_t = cs.T
    dt_t = dt.T
    tri = row >= lane
    half = lane < SSM_HEADDIM
    zero_b = jnp.zeros((L, LANE), BF16)

    for g in range(SSM_GROUPS):
        bg = act[:, inner + g * gw:inner + (g + 1) * gw]
        cg = act[:, inner + SSM_GROUPS * gw + g * gw:inner + SSM_GROUPS * gw + (g + 1) * gw]
        cg_b = cg.astype(BF16)
        cb = lax.dot_general(cg_b, bg.astype(BF16), (((1,), (1,)), ((), ())), preferred_element_type=F32)
        bg_t = bg.T
        for pair in range(2):
            h0 = g * 4 + pair * 2
            pidx = h0 // 2
            xpp = xs_b[:, pidx * LANE:(pidx + 1) * LANE]
            rhs = jnp.concatenate([jnp.where(half, xpp, zero_b), jnp.where(half, zero_b, xpp)], axis=0)
            dec, dst, eoff, cdec = [], [], [], []
            for h in (h0, h0 + 1):
                cs_col = cs[:, h:h + 1]
                cs_row = cs_t[h:h + 1, :]
                dt_row = dt_t[h:h + 1, :]
                dec.append(cb * (jnp.exp(jnp.where(tri, cs_col - cs_row, NEG)) * dt_row))
                cs_last = cs_row[:, L - 1:L]
                dst.append(bg_t * (jnp.exp(cs_last - cs_row) * dt_row))
                eoff.append(jnp.exp(cs_col))
                cdec.append(jnp.exp(cs_last))
            y_diag = jnp.dot(jnp.concatenate(dec, axis=1).astype(BF16), rhs, preferred_element_type=F32)
            st_new = jnp.dot(jnp.concatenate(dst, axis=1).astype(BF16), rhs, preferred_element_type=F32)
            prev = state[pidx]
            y_off = jnp.dot(cg_b, prev.astype(BF16), preferred_element_type=F32)
            y_off = y_off * jnp.where(half, eoff[0], eoff[1])
            state[pidx] = prev * jnp.where(half, cdec[0], cdec[1]) + st_new
            y = y_diag + y_off + dsk_ref[:, pidx * LANE:(pidx + 1) * LANE] * xs[:, pidx * LANE:(pidx + 1) * LANE]
            out_pair = y * _silu(z_ref[rs, pidx * LANE:(pidx + 1) * LANE].astype(F32))
            ystage[rs, pidx * LANE:(pidx + 1) * LANE] = out_pair

    gsz = inner // SSM_GROUPS
    for g in range(SSM_GROUPS):
        yg = ystage[rs, g * gsz:(g + 1) * gsz]
        ms = jnp.mean(yg * yg, axis=-1, keepdims=True)
        out_ref[rs, g * gsz:(g + 1) * gsz] = (yg * lax.rsqrt(ms + NORM_EPS) * nw_ref[:, g * gsz:(g + 1) * gsz]).astype(BF16)


def _ssd(proj, dt_raw, conv_w, conv_b, dt_bias, a_log, d_skip, ssm_norm_w, batch, seq):
    t = batch * seq
    L = SSM_CHUNK * SSM_CHUNKS_PER_STEP
    nc = seq // L
    inner = SSM_HEADS * SSM_HEADDIM
    cdim = conv_w.shape[1]
    pad16 = lambda v: jnp.pad(v.astype(F32), (0, LANE - SSM_HEADS)).reshape(1, LANE)
    dsk = jnp.repeat(d_skip.astype(F32), SSM_HEADDIM).reshape(1, inner)
    rowc = lambda b, c: (b * nc + c, 0)
    const = lambda b, c: (0, 0)
    return pl.pallas_call(
        _ssd_kernel,
        grid=(batch, nc),
        in_specs=[pl.BlockSpec((L, cdim), lambda b, c: (b * nc + c, COL_XBC // cdim)),
                  pl.BlockSpec((L, inner), lambda b, c: (b * nc + c, COL_Z // inner)),
                  pl.BlockSpec((L, LANE), rowc),
                  pl.BlockSpec((SSM_CONV, cdim), const),
                  pl.BlockSpec((1, cdim), const),
                  pl.BlockSpec((1, LANE), const),
                  pl.BlockSpec((1, LANE), const),
                  pl.BlockSpec((1, inner), const),
                  pl.BlockSpec((1, inner), const)],
        out_specs=pl.BlockSpec((L, inner), rowc),
        out_shape=jax.ShapeDtypeStruct((t, inner), BF16),
        scratch_shapes=[pltpu.VMEM((SSM_CHUNK + L, cdim), BF16), pltpu.VMEM((L, inner), F32),
                        pltpu.VMEM((SSM_HEADS // 2, SSM_STATE, 2 * SSM_HEADDIM), F32)],
        compiler_params=_cp(("arbitrary", "arbitrary")),
        name="ssd",
    )(proj, proj, dt_raw, conv_w.astype(F32), conv_b.reshape(1, cdim).astype(F32), pad16(dt_bias), pad16(a_log),
      dsk, ssm_norm_w.reshape(1, inner).astype(F32))


def _mix_kernel(o0_ref, o1_ref, o2_ref, s0_ref, s1_ref, s2_ref, ssm_ref, ga_ref, gs_ref, x_ref,
                g1_ref, sc2_ref, sh2_ref, g2_ref, nw_ref, wba_ref, wbs_ref, wo_ref, wrt_ref,
                wgs_ref, wus_ref, wds_ref, rbias_ref, base_ref, u2p_ref, idx_ref, rw_ref, mem_ref, cnt_ref, run):
    o_refs = (o0_ref, o1_ref, o2_ref)
    s_refs = (s0_ref, s1_ref, s2_ref)
    tm = x_ref.shape[0]
    sub = tm // MIX_SPLIT

    @pl.when(pl.program_id(0) == 0)
    def _():
        run[...] = jnp.zeros(run.shape, F32)

    for part in range(MIX_SPLIT):
        rs = slice(part * sub, (part + 1) * sub)
        heads = []
        for h in range(HEADS_PER_GROUP):
            ms = [s[rs, h:h + 1] for s in s_refs]
            ls = [s[rs, HEADS_PER_GROUP + h:HEADS_PER_GROUP + h + 1] for s in s_refs]
            mx = jnp.maximum(jnp.maximum(ms[0], ms[1]), ms[2])
            wts = [l * jnp.exp(m - mx) for m, l in zip(ms, ls)]
            num = wts[0] * o_refs[0][h, rs, :] + wts[1] * o_refs[1][h, rs, :] + wts[2] * o_refs[2][h, rs, :]
            heads.append((num / (wts[0] + wts[1] + wts[2])).astype(BF16))
        attn = jnp.concatenate(heads, axis=1)
        ya = jnp.dot(attn, wba_ref[...], preferred_element_type=F32)
        ys = jnp.dot(ssm_ref[rs, :], wbs_ref[...], preferred_element_type=F32)
        merged = _sigmoid(ga_ref[rs, :].astype(F32)) * ya + _sigmoid(gs_ref[rs, :].astype(F32)) * ys
        mix = jnp.dot(merged.astype(BF16), wo_ref[...], preferred_element_type=F32)
        h1 = x_ref[rs, :] + g1_ref[...] * mix
        ms2 = jnp.mean(h1 * h1, axis=-1, keepdims=True)
        u2 = h1 * lax.rsqrt(ms2 + NORM_EPS) * nw_ref[...] * (1.0 + sc2_ref[...]) + sh2_ref[...]
        for cc, chunk in enumerate(_pack_row_chunks(u2)):
            u2p_ref[pl.ds(part * sub * ROW_PACK + cc, sub, stride=ROW_PACK), :] = chunk
        u2b = u2.astype(BF16)
        u2lo = (u2 - u2b.astype(F32)).astype(BF16)
        nt = (((1,), (1,)), ((), ()))
        logits_t = (lax.dot_general(wrt_ref[0], u2b, nt, preferred_element_type=F32)
                    + lax.dot_general(wrt_ref[0], u2lo, nt, preferred_element_type=F32)
                    + lax.dot_general(wrt_ref[1], u2b, nt, preferred_element_type=F32))
        idx, rw, member = _route_tile(_sigmoid(logits_t), rbias_ref[...])
        idx_ref[:, rs] = idx
        rw_ref[:, rs] = rw
        mem_ref[:, rs] = member.astype(BF16)
        run[...] = run[...] + jnp.sum(member, axis=1, keepdims=True)
        hs_ = (_silu(jnp.dot(u2b, wgs_ref[...], preferred_element_type=F32))
               * jnp.dot(u2b, wus_ref[...], preferred_element_type=F32))
        shared = jnp.dot(hs_.astype(BF16), wds_ref[...], preferred_element_type=F32)
        base_ref[rs, :] = h1 + g2_ref[...] * shared
    cnt_ref[...] = run[...]


def _mix(o_list, st_list, ssm, proj, x2, g1, sc2, sh2, g2, norm_w, w_ba, w_bs, w_o, w_rt, w_gs, w_us, w_ds, rbias,
         seq):
    t, d = x2.shape
    tm = MIX_TM
    per_b = seq // tm
    row = lambda i: (i, 0)
    modrow = lambda i: (i // per_b, 0, 0)
    full = lambda a: pl.BlockSpec(a.shape, lambda i: (0,) * a.ndim, pipeline_mode=pl.Buffered(1))
    mod = pl.BlockSpec((None, 1, d), modrow)
    return pl.pallas_call(
        _mix_kernel,
        grid=(t // tm,),
        in_specs=[pl.BlockSpec((HEADS_PER_GROUP, tm, HEAD_DIM), lambda i: (0, i, 0))] * 3
        + [pl.BlockSpec((tm, LANE), row)] * 3 + [
            pl.BlockSpec((tm, d), row),
            pl.BlockSpec((tm, d), lambda i: (i, COL_GA // d)),
            pl.BlockSpec((tm, d), lambda i: (i, COL_GS // d)),
            pl.BlockSpec((tm, d), row),
            mod, mod, mod, mod, full(norm_w), full(w_ba), full(w_bs), full(w_o), full(w_rt),
            full(w_gs), full(w_us), full(w_ds), full(rbias)],
        out_specs=[pl.BlockSpec((tm, d), row), pl.BlockSpec((tm * ROW_PACK, LANE), row),
                   pl.BlockSpec((TOP_K, tm), lambda i: (0, i)), pl.BlockSpec((TOP_K, tm), lambda i: (0, i)),
                   pl.BlockSpec((N_EXPERTS, tm), lambda i: (0, i)), pl.BlockSpec((N_EXPERTS, LANE), lambda i: (0, 0))],
        out_shape=[jax.ShapeDtypeStruct((t, d), F32), jax.ShapeDtypeStruct((t * ROW_PACK, LANE), U32),
                   jax.ShapeDtypeStruct((TOP_K, t), I32), jax.ShapeDtypeStruct((TOP_K, t), F32),
                   jax.ShapeDtypeStruct((N_EXPERTS, t), BF16), jax.ShapeDtypeStruct((N_EXPERTS, LANE), F32)],
        scratch_shapes=[pltpu.VMEM((N_EXPERTS, LANE), F32)],
        compiler_params=_cp(("arbitrary",)),
        name="mix",
    )(*o_list, *st_list, ssm, proj, proj, x2, g1, sc2, sh2, g2, norm_w, w_ba, w_bs, w_o, w_rt, w_gs, w_us, w_ds, rbias)


def _route_tile(s, bias):
    tm = s.shape[1]
    per_g = N_EXPERTS // N_EXPERT_GROUPS
    biased = s + bias
    io_g = lax.broadcasted_iota(I32, (per_g, tm), 0).astype(F32)
    gscore = []
    for g in range(N_EXPERT_GROUPS):
        bgp = biased[g * per_g:(g + 1) * per_g, :]
        m1 = jnp.max(bgp, axis=0, keepdims=True)
        first = jnp.min(jnp.where(bgp == m1, io_g, float(per_g)), axis=0, keepdims=True)
        m2 = jnp.max(jnp.where(io_g == first, NEG, bgp), axis=0, keepdims=True)
        gscore.append(m1 + m2)
    gs = jnp.concatenate(gscore, axis=0)
    io8 = lax.broadcasted_iota(I32, (N_EXPERT_GROUPS, tm), 0).astype(F32)
    gsel = jnp.zeros((N_EXPERT_GROUPS, tm), F32)
    cur = gs
    for _ in range(TOPK_GROUPS):
        mx = jnp.max(cur, axis=0, keepdims=True)
        fi = jnp.min(jnp.where(cur == mx, io8, float(N_EXPERT_GROUPS)), axis=0, keepdims=True)
        hit = io8 == fi
        gsel = jnp.where(hit, 1.0, gsel)
        cur = jnp.where(hit, NEG, cur)
    masked = jnp.concatenate(
        [jnp.where(gsel[g:g + 1, :] > 0.5, biased[g * per_g:(g + 1) * per_g, :], NEG) for g in range(N_EXPERT_GROUPS)],
        axis=0)
    io_e = lax.broadcasted_iota(I32, (N_EXPERTS, tm), 0).astype(F32)
    member = jnp.zeros((N_EXPERTS, tm), F32)
    idxs, ws = [], []
    for _ in range(TOP_K):
        mx = jnp.max(masked, axis=0, keepdims=True)
        fi = jnp.min(jnp.where(masked == mx, io_e, float(N_EXPERTS)), axis=0, keepdims=True)
        hit = io_e == fi
        idxs.append(fi)
        ws.append(jnp.sum(jnp.where(hit, s, 0.0), axis=0, keepdims=True))
        member = jnp.where(hit, 1.0, member)
        masked = jnp.where(hit, NEG, masked)
    wsum = ws[0]
    for k in range(1, TOP_K):
        wsum = wsum + ws[k]
    idx = jnp.concatenate(idxs, axis=0).astype(I32)
    w = jnp.concatenate([wk / wsum * ROUTED_SCALE for wk in ws], axis=0)
    return idx, w, member


def _dest_kernel(mem_ref, idx_ref, start_ref, dest_ref, run):
    i = pl.program_id(0)
    e, tm = mem_ref.shape

    @pl.when(i == 0)
    def _():
        run[...] = jnp.broadcast_to(start_ref[...], run.shape)

    member = mem_ref[...]
    tr = lax.broadcasted_iota(I32, (tm, tm), 0)
    tc = lax.broadcasted_iota(I32, (tm, tm), 1)
    upper = jnp.where(tr < tc, 1.0, 0.0).astype(BF16)
    rank_full = jnp.dot(member, upper, preferred_element_type=F32) + run[:, 0:1]
    io_e = lax.broadcasted_iota(I32, (e, tm), 0)
    idx = idx_ref[...]
    rows = [jnp.sum(jnp.where(io_e == idx[k:k + 1, :], rank_full, 0.0), axis=0, keepdims=True) for k in range(TOP_K)]
    dest_ref[...] = jnp.concatenate(rows, axis=0).astype(I32)
    run[...] = run[...] + jnp.sum(member.astype(F32), axis=1, keepdims=True)


def _dest(member, idx, pad_start):
    e, t = member.shape
    tm = ROUTE_TM
    return pl.pallas_call(
        _dest_kernel,
        grid=(t // tm,),
        in_specs=[pl.BlockSpec((e, tm), lambda i: (0, i)), pl.BlockSpec((TOP_K, tm), lambda i: (0, i)),
                  pl.BlockSpec((e, 1), lambda i: (0, 0))],
        out_specs=pl.BlockSpec((TOP_K, tm), lambda i: (0, i)),
        out_shape=jax.ShapeDtypeStruct((TOP_K, t), I32),
        scratch_shapes=[pltpu.VMEM((e, LANE), F32)],
        compiler_params=_cp(("arbitrary",)),
        name="dest",
    )(member, idx, pad_start.astype(F32).reshape(e, 1))


def _dispatch_kernel(pend_ref, npad_ref, dest_ref, u_ref, xs_ref, zero_scr, sem, zsem):
    i = pl.program_id(0)
    rt = ROW_PACK
    tm = u_ref.shape[0] // rt
    zrows = zero_scr.shape[0]

    @pl.when(i == 0)
    def _():
        zero_scr[...] = jnp.zeros(zero_scr.shape, zero_scr.dtype)

        def zcopy(e):
            start = pl.multiple_of(pend_ref[e] * rt - zrows, zrows)
            return pltpu.make_async_copy(zero_scr, xs_ref.at[pl.ds(start, zrows)], zsem)

        def zstart(e, c):
            @pl.when(npad_ref[e] > 0)
            def _():
                zcopy(e).start()
            return c

        def zwait(e, c):
            @pl.when(npad_ref[e] > 0)
            def _():
                zcopy(e).wait()
            return c

        lax.fori_loop(0, N_EXPERTS, zstart, 0)
        lax.fori_loop(0, N_EXPERTS, zwait, 0)

    def row_copy(t, k):
        src = u_ref.at[pl.ds(pl.multiple_of(t * rt, rt), rt)]
        dst = xs_ref.at[pl.ds(pl.multiple_of(dest_ref[t * TOP_K + k] * rt, rt), rt)]
        return pltpu.make_async_copy(src, dst, sem)

    def start(t, c):
        for k in range(TOP_K):
            row_copy(t, k).start(priority=k % 2)
        return c

    def wait(t, c):
        for k in range(TOP_K):
            row_copy(t, k).wait()
        return c

    lax.fori_loop(0, tm, start, 0)
    lax.fori_loop(0, tm, wait, 0)


def _dispatch(u2p, dest_flat, pad_end, padded, n_rows):
    rt = ROW_PACK
    t = u2p.shape[0] // rt
    tm = DISPATCH_TM
    gs = pltpu.PrefetchScalarGridSpec(
        num_scalar_prefetch=2,
        grid=(t // tm,),
        in_specs=[pl.BlockSpec((tm * TOP_K,), lambda i, pe, npd: (i,), memory_space=pltpu.SMEM),
                  pl.BlockSpec((tm * rt, LANE), lambda i, pe, npd: (i, 0))],
        out_specs=pl.BlockSpec(memory_space=pl.ANY),
        scratch_shapes=[pltpu.VMEM((MOE_ROWS * rt, LANE), U32), pltpu.SemaphoreType.DMA, pltpu.SemaphoreType.DMA],
    )
    return pl.pallas_call(
        _dispatch_kernel,
        grid_spec=gs,
        out_shape=jax.ShapeDtypeStruct((n_rows * rt, LANE), U32),
        compiler_params=_cp(("arbitrary",)),
        name="dispatch",
    )(pad_end, padded, dest_flat, u2p)


def _expert_kernel(bstart_ref, nblk_ref, wg_ref, wu_ref, wd_ref, xs_ref, ys_ref,
                   wg_b, wu_b, wd_b, xbuf, ybuf, in_sem, out_sem):
    e = pl.program_id(0)
    last = pl.num_programs(0) - 1
    rt = ROW_PACK
    n_in, rb = xbuf.shape[0], xbuf.shape[1] // rt
    n_out = ybuf.shape[0]
    n = nblk_ref[e]
    g0 = bstart_ref[e]
    total = bstart_ref[last] + nblk_ref[last]

    def block_rows(g):
        return pl.ds(pl.multiple_of(g * (rb * rt), rb * rt), rb * rt)

    def in_copy(g):
        return pltpu.make_async_copy(xs_ref.at[block_rows(g)], xbuf.at[g % n_in], in_sem.at[g % n_in])

    def out_copy(g):
        return pltpu.make_async_copy(ybuf.at[g % n_out], ys_ref.at[block_rows(g)], out_sem.at[g % n_out])

    @pl.when(e == 0)
    def _():
        for g in range(n_in - 1):
            @pl.when(g < total)
            def _():
                in_copy(g).start()

    wg_b[...] = wg_ref[...].astype(BF16)
    wu_b[...] = wu_ref[...].astype(BF16)
    wd_b[...] = wd_ref[...].astype(BF16)

    def body(b, carry):
        g = g0 + b
        in_copy(g).wait()

        @pl.when(g + n_in - 1 < total)
        def _():
            in_copy(g + n_in - 1).start()

        @pl.when(g >= n_out)
        def _():
            out_copy(g - n_out).wait()

        halves = [_unpack_pair(xbuf[g % n_in, pl.ds(cc, rb, stride=rt), :]) for cc in range(rt)]
        xb = jnp.concatenate([lo.astype(BF16) for lo, _ in halves] + [hi.astype(BF16) for _, hi in halves], axis=1)
        gate = jnp.dot(xb, wg_b[...], preferred_element_type=F32)
        up = jnp.dot(xb, wu_b[...], preferred_element_type=F32)
        h = (_silu(gate) * up).astype(BF16)
        y = jnp.dot(h, wd_b[...], preferred_element_type=F32)
        for cc, chunk in enumerate(_pack_row_chunks(y)):
            ybuf[g % n_out, pl.ds(cc, rb, stride=rt), :] = chunk
        out_copy(g).start()
        return carry

    lax.fori_loop(0, n, body, 0)

    @pl.when(e == last)
    def _():
        for j in range(n_out, 0, -1):
            @pl.when(total >= j)
            def _():
                out_copy(total - j).wait()


def _experts(xs, blk_start, nblk, w_gate_e, w_up_e, w_down_e):
    ne, d, ff = w_gate_e.shape
    rb = MOE_ROWS * ROW_PACK
    gs = pltpu.PrefetchScalarGridSpec(
        num_scalar_prefetch=2,
        grid=(ne,),
        in_specs=[pl.BlockSpec((None, d, ff), lambda e, ps, nb: (e, 0, 0)),
                  pl.BlockSpec((None, d, ff), lambda e, ps, nb: (e, 0, 0)),
                  pl.BlockSpec((None, ff, d), lambda e, ps, nb: (e, 0, 0)),
                  pl.BlockSpec(memory_space=pl.ANY)],
        out_specs=pl.BlockSpec(memory_space=pl.ANY),
        scratch_shapes=[pltpu.VMEM((d, ff), BF16), pltpu.VMEM((d, ff), BF16), pltpu.VMEM((ff, d), BF16),
                        pltpu.VMEM((EXPERT_IN_SLOTS, rb, LANE), U32), pltpu.VMEM((EXPERT_OUT_SLOTS, rb, LANE), U32),
                        pltpu.SemaphoreType.DMA((EXPERT_IN_SLOTS,)), pltpu.SemaphoreType.DMA((EXPERT_OUT_SLOTS,))],
    )
    return pl.pallas_call(
        _expert_kernel,
        grid_spec=gs,
        out_shape=jax.ShapeDtypeStruct(xs.shape, U32),
        compiler_params=_cp(("arbitrary",)),
        name="experts",
    )(blk_start, nblk, w_gate_e, w_up_e, w_down_e, xs)


def _combine_kernel(dest_ref, dnext_ref, wt_ref, base_ref, g2_ref, nfw_ref, ys_ref, out_ref, buf, sem):
    i = pl.program_id(0)
    n = pl.num_programs(0)
    tm = base_ref.shape[0]
    rt = ROW_PACK
    half = i % 2

    def row_copy(d_ref, h, t, k):
        src = ys_ref.at[pl.ds(pl.multiple_of(d_ref[t * TOP_K + k] * rt, rt), rt)]
        return pltpu.make_async_copy(src, buf.at[h, k, pl.ds(pl.multiple_of(t * rt, rt), rt)], sem.at[h])

    def issue(d_ref, h):
        def start(t, c):
            for k in range(TOP_K):
                row_copy(d_ref, h, t, k).start(priority=k % 2)
            return c
        lax.fori_loop(0, tm, start, 0)

    @pl.when(i == 0)
    def _():
        issue(dest_ref, 0)

    @pl.when(i + 1 < n)
    def _():
        issue(dnext_ref, 1 - half)

    def wait(t, c):
        for k in range(TOP_K):
            row_copy(dest_ref, half, t, k).wait()
        return c

    lax.fori_loop(0, tm, wait, 0)

    upper = lax.broadcasted_iota(I32, (2 * rt, LANE), 0) >= rt

    def weigh(t2, c):
        rows = pl.ds(pl.multiple_of(t2 * (2 * rt), 2 * rt), 2 * rt)
        acc_lo = jnp.zeros((2 * rt, LANE), F32)
        acc_hi = jnp.zeros((2 * rt, LANE), F32)
        for k in range(TOP_K):
            lo, hi = _unpack_pair(buf[half, k, rows, :])
            wv = jnp.where(upper, wt_ref[(2 * t2 + 1) * TOP_K + k], wt_ref[2 * t2 * TOP_K + k])
            acc_lo = acc_lo + lo * wv
            acc_hi = acc_hi + hi * wv
        buf[half, 0, rows, :] = lax.bitcast_convert_type(acc_lo, U32)
        buf[half, 1, rows, :] = lax.bitcast_convert_type(acc_hi, U32)
        return c

    lax.fori_loop(0, tm // 2, weigh, 0, unroll=2)
    ssq = jnp.zeros((tm, 1), F32)
    for cc in range(2 * rt):
        cs = slice(cc * LANE, (cc + 1) * LANE)
        routed = lax.bitcast_convert_type(buf[half, cc // rt, pl.ds(cc % rt, tm, stride=rt), :], F32)
        h2 = base_ref[:, cs] + g2_ref[:, cs] * routed
        out_ref[:, cs] = h2
        ssq = ssq + jnp.sum(h2 * h2, axis=-1, keepdims=True)
    inv = lax.rsqrt(ssq / (2 * rt * LANE) + NORM_EPS)
    out_ref[...] = out_ref[...] * inv * nfw_ref[...]


def _combine(ys, dest_flat, w_tok, base, g2, norm_final_w, seq):
    t, d = base.shape
    tm = COMBINE_TM
    per_b = seq // tm
    gs = pltpu.PrefetchScalarGridSpec(
        num_scalar_prefetch=0,
        grid=(t // tm,),
        in_specs=[pl.BlockSpec((tm * TOP_K,), lambda i: (i,), memory_space=pltpu.SMEM),
                  pl.BlockSpec((tm * TOP_K,), lambda i: (jnp.minimum(i + 1, t // tm - 1),), memory_space=pltpu.SMEM),
                  pl.BlockSpec((tm * TOP_K,), lambda i: (i,), memory_space=pltpu.SMEM),
                  pl.BlockSpec((tm, d), lambda i: (i, 0)),
                  pl.BlockSpec((None, 1, d), lambda i: (i // per_b, 0, 0)),
                  pl.BlockSpec((1, d), lambda i: (0, 0)),
                  pl.BlockSpec(memory_space=pl.ANY)],
        out_specs=pl.BlockSpec((tm, d), lambda i: (i, 0)),
        scratch_shapes=[pltpu.VMEM((2, TOP_K, tm * ROW_PACK, LANE), U32), pltpu.SemaphoreType.DMA((2,))],
    )
    return pl.pallas_call(
        _combine_kernel,
        grid_spec=gs,
        out_shape=jax.ShapeDtypeStruct((t, d), F32),
        compiler_params=_cp(("arbitrary",)),
        name="combine",
    )(dest_flat, dest_flat, w_tok, base, g2, norm_final_w, ys)


def kernel(x, c, positions, w_mod, b_mod, norm_mix_w, norm_ffn_w, w_in, conv_w, conv_b, dt_bias, a_log, d_skip,
           ssm_norm_w, w_branch_attn, w_branch_ssm, w_out, w_router, router_bias, w_gate_e, w_up_e, w_down_e,
           w_gate_s, w_up_s, w_down_s, norm_final_w):
    batch, seq, d = x.shape
    t = batch * seq
    assert w_mod.shape[0] == 1, "one layer"
    assert d == 2 * ROW_PACK * LANE and seq % INPROJ_TM == 0 and INPROJ_TM % (ATTN_DILATIONS[-1] * 16) == 0

    mod = _modulation(c, w_mod[0], b_mod[0])
    sh1, sc1, g1, sh2, sc2, g2 = [m.reshape(batch, 1, d) for m in jnp.split(mod, 6, axis=-1)]
    rope_c, rope_s1, rope_s2 = _rope_tables(positions)

    wi = w_in[0]
    q_dim = 3 * GROUP_W
    o_z = 3 * q_dim
    o_xbc = o_z + d
    o_dt = o_xbc + conv_w.shape[2]
    o_g = o_dt + SSM_HEADS
    qkv = lambda g: [wi[:, s * q_dim + g * GROUP_W:s * q_dim + (g + 1) * GROUP_W] for s in range(3)]
    w_packed = jnp.concatenate([wi[:, o_xbc:o_dt], wi[:, o_g:], wi[:, o_z:o_xbc]] + qkv(0) + qkv(1) + qkv(2),
                               axis=1).astype(BF16)
    assert w_packed.shape[1] == MAIN_W + 2 * QKV_W
    w_dt = jnp.pad(wi[:, o_dt:o_g], ((0, 0), (0, LANE - SSM_HEADS))).astype(BF16)

    x2 = x.reshape(t, d)
    proj, qkv1, qkv2, dt_raw = _inproj(x2, sc1, sh1, norm_mix_w.reshape(1, d), w_packed, w_dt,
                                       rope_c, rope_s1, rope_s2, batch, seq)

    srcs = [(proj.reshape(batch, 1, seq, MAIN_W), COL_Q0 // GROUP_W), (qkv1, 0), (qkv2, 0)]
    o_list, st_list = [], []
    for g, (src, col0) in enumerate(srcs):
        o, st = _attention_group(src, g, batch, seq, col0)
        o_list.append(o)
        st_list.append(st)
    ssm = _ssd(proj, dt_raw, conv_w[0], conv_b[0], dt_bias[0], a_log[0], d_skip[0], ssm_norm_w[0], batch, seq)

    wr_t = w_router[0].T
    wr_hi = wr_t.astype(BF16)
    w_rt = jnp.stack([wr_hi, (wr_t - wr_hi.astype(F32)).astype(BF16)])
    base, u2p, idx, w_sel, member, counts = _mix(
        o_list, st_list, ssm, proj, x2, g1, sc2, sh2, g2, norm_ffn_w.reshape(1, d),
        w_branch_attn[0].astype(BF16), w_branch_ssm[0].astype(BF16), w_out[0].astype(BF16),
        w_rt, w_gate_s[0].astype(BF16), w_up_s[0].astype(BF16), w_down_s[0].astype(BF16),
        router_bias[0].reshape(N_EXPERTS, 1).astype(F32), seq)

    rb = MOE_ROWS
    cnt = counts[:, 0].astype(I32)
    padded = (cnt + rb - 1) // rb * rb
    pad_end = jnp.cumsum(padded).astype(I32)
    pad_start = pad_end - padded
    n_rows = t * TOP_K + N_EXPERTS * rb

    dest_flat = _dest(member, idx, pad_start).T.reshape(t * TOP_K)
    xs = _dispatch(u2p, dest_flat, pad_end, padded, n_rows)
    ys = _experts(xs, pad_start // rb, padded // rb, w_gate_e[0], w_up_e[0], w_down_e[0])
    out = _combine(ys, dest_flat, w_sel.T.reshape(t * TOP_K), base, g2, norm_final_w.reshape(1, d), seq)
    return out.reshape(batch, seq, d)
```
